```python
import jax, jax.numpy as jnp
from jax import lax
import numpy as np

D_MODEL = 1024
BATCH = 1
SEQ = 16384
DEPTH = 2
DEC_BATCH = 8
DEC_SEQ = 32
PAST_LEN = 4096

CHUNK = 64
EPS = 1e-6
N_ADA = 9
D_FF = 2816
MLA_HEADS = 4
Q_RANK = 256
KV_RANK = 128
NOPE_DIM = 128
ROPE_DIM = 64
V_DIM = 128
ROPE_BASE = 10000.0
Q_BLOCK = 128
ATTN_SCALE = (NOPE_DIM + ROPE_DIM) ** -0.5
CONV_DIM = 256
CONV_W = 3
POOL_WINDOWS = (2, 4, 8, 16)
POOL_GROUPS = 4
POOL_DIM = 256
POOL_GROUP_DIM = POOL_DIM // POOL_GROUPS
POOL_MAX = 16
MIX_DIM = MLA_HEADS * V_DIM + CONV_DIM + POOL_DIM
IN_SIZES = (Q_RANK, KV_RANK, ROPE_DIM, CONV_DIM, CONV_DIM, CONV_DIM, POOL_DIM)
IN_COLS = sum(IN_SIZES)
IN_SPLIT = tuple(np.cumsum(IN_SIZES)[:-1].tolist())

kernel_name = 'hybrid_mla_conv_pool_macaron_adaln_stream_step'


def rms_norm(x, g):
    xf = x.astype(jnp.float32)
    y = xf * lax.rsqrt(jnp.mean(xf * xf, axis=-1, keepdims=True) + EPS)
    return (y * g.astype(jnp.float32)).astype(x.dtype)


def modulate(n, shift, scale):
    return n * (1.0 + scale) + shift


def swiglu(n, w_gu, w_down):
    g, u = jnp.split(n @ w_gu, 2, axis=-1)
    return (jax.nn.silu(g) * u) @ w_down


def rope(x, pos):
    half = ROPE_DIM // 2
    inv = ROPE_BASE ** (-jnp.arange(half, dtype=jnp.float32) / half)
    ang = pos.astype(jnp.float32)[:, None] * inv[None, :]
    cos = jnp.cos(ang)[None, :, None, :]
    sin = jnp.sin(ang)[None, :, None, :]
    xf = x.astype(jnp.float32)
    x1, x2 = xf[..., :half], xf[..., half:]
    return jnp.concatenate([x1 * cos - x2 * sin, x1 * sin + x2 * cos], axis=-1).astype(x.dtype)


def attend_block(q, k, v, q_pos, k_pos):
    s = jnp.einsum('bqhd,bkhd->bhqk', q, k, preferred_element_type=jnp.float32) * ATTN_SCALE
    mask = (k_pos[None, :] // CHUNK) <= (q_pos[:, None] // CHUNK)
    s = jnp.where(mask[None, None], s, -jnp.inf)
    p = jax.nn.softmax(s, axis=-1).astype(v.dtype)
    return jnp.einsum('bhqk,bkhd->bqhd', p, v)


def attend(q, k, v, q_pos, k_pos):
    B, T, H, Dk = q.shape
    if T <= Q_BLOCK:
        return attend_block(q, k, v, q_pos, k_pos)
    nb = T // Q_BLOCK
    qb = q.reshape(B, nb, Q_BLOCK, H, Dk).transpose(1, 0, 2, 3, 4)
    pb = q_pos.reshape(nb, Q_BLOCK)
    ob = lax.map(lambda a: attend_block(a[0], k, v, a[1], k_pos), (qb, pb))
    return ob.transpose(1, 0, 2, 3, 4).reshape(B, T, H, v.shape[-1])


def mla(cq, ckv, kr, pos, past_ckv, past_kr, q_norm, w_uq, kv_norm, w_ukv):
    B, T, _ = cq.shape
    q = (rms_norm(cq, q_norm) @ w_uq).reshape(B, T, MLA_HEADS, NOPE_DIM + ROPE_DIM)
    q = jnp.concatenate([q[..., :NOPE_DIM], rope(q[..., NOPE_DIM:], pos)], axis=-1)
    ckv = rms_norm(ckv, kv_norm)
    kr = rope(kr[:, :, None, :], pos)[:, :, 0, :]
    ckv_all = jnp.concatenate([past_ckv, ckv], axis=1)
    kr_all = jnp.concatenate([past_kr, kr], axis=1)
    S = ckv_all.shape[1]
    kv = (ckv_all @ w_ukv).reshape(B, S, MLA_HEADS, NOPE_DIM + V_DIM)
    k = jnp.concatenate([kv[..., :NOPE_DIM],
                         jnp.broadcast_to(kr_all[:, :, None, :], (B, S, MLA_HEADS, ROPE_DIM))], axis=-1)
    v = kv[..., NOPE_DIM:]
    k_pos = jnp.arange(S, dtype=jnp.int32)
    o = attend(q, k, v, pos, k_pos)
    return o.reshape(B, T, MLA_HEADS * V_DIM), ckv, kr


def short_conv(bg, cg, v, past, w):
    T = v.shape[1]
    u = cg * v
    ext = jnp.concatenate([past, u], axis=1)
    y = ext[:, 0:T] * w[0]
    for j in range(1, CONV_W):
        y = y + ext[:, j:j + T] * w[j]
    return bg * y, ext[:, -(CONV_W - 1):]


def pool_mix(u, pos, past, w_pool, scale):
    B, T, C = u.shape
    L = POOL_MAX - 1
    ext = jnp.concatenate([past, u], axis=1)
    extf = ext.astype(jnp.float32)
    cs = jnp.concatenate([jnp.zeros((B, 1, C), jnp.float32), jnp.cumsum(extf, axis=1)], axis=1)
    hi = cs[:, L + 1:]
    outs = []
    for g, w in enumerate(POOL_WINDOWS):
        sl = slice(g * POOL_GROUP_DIM, (g + 1) * POOL_GROUP_DIM)
        lo = cs[:, L + 1 - w:L + 1 - w + T, sl]
        cnt = jnp.minimum(w, pos + 1).astype(jnp.float32)
        outs.append((hi[..., sl] - lo) / cnt[None, :, None])
    mean = jnp.concatenate(outs, axis=-1)
    d = (mean - u.astype(jnp.float32)).astype(u.dtype).reshape(B, T, POOL_GROUPS, POOL_GROUP_DIM)
    y = jnp.einsum('btgc,gcd->btgd', d, w_pool).reshape(B, T, C) * scale
    return y, ext[:, -L:]


def trunk_layer(x, c, past_ckv, past_kr, past_conv, past_pool, p):
    B, T, _ = x.shape
    pos = past_ckv.shape[1] + jnp.arange(T, dtype=jnp.int32)
    mod = (jax.nn.silu(c) @ p['w_ada'] + p['b_ada']).reshape(B, N_ADA, D_MODEL)[:, :, None, :]
    sh1, sc1, g1, sh2, sc2, g2, sh3, sc3, g3 = [mod[:, i] for i in range(N_ADA)]
    h = x + 0.5 * g1 * swiglu(modulate(rms_norm(x, p['norm_ffn1']), sh1, sc1),
                              p['w_ffn1_gu'], p['w_ffn1_down'])
    n = modulate(rms_norm(h, p['norm_mix']), sh2, sc2)
    cq, ckv, kr, cb, cc, cv, pu = jnp.split(n @ p['w_in'], IN_SPLIT, axis=-1)
    a, ckv_new, kr_new = mla(cq, ckv, kr, pos, past_ckv, past_kr,
                             p['q_norm'], p['w_uq'], p['kv_norm'], p['w_ukv'])
    s, conv_new = short_conv(cb, cc, cv, past_conv, p['conv_w'])
    q, pool_new = pool_mix(pu, pos, past_pool, p['pool_w'], p['pool_scale'])
    h = h + g2 * (jnp.concatenate([a, s, q], axis=-1) @ p['w_out'])
    h = h + 0.5 * g3 * swiglu(modulate(rms_norm(h, p['norm_ffn2']), sh3, sc3),
                              p['w_ffn2_gu'], p['w_ffn2_down'])
    return h, ckv_new, kr_new, conv_new, pool_new


def setup_inputs(seed: int = 0) -> dict:
    key = jax.random.key(seed)
    ks = jax.random.split(key, 32)
    f32 = jnp.float32

    def nrm(k, shape, scale):
        return jax.random.normal(k, shape, f32) * scale

    def gain(k, shape):
        return 1.0 + 0.1 * jax.random.normal(k, shape, f32)

    L = DEPTH
    return {
        'x_prompt': nrm(ks[0], (BATCH, SEQ, D_MODEL), 1.0),
        'x_sample': nrm(ks[1], (DEC_BATCH, DEC_SEQ, D_MODEL), 1.0),
        'c_prompt': nrm(ks[2], (BATCH, D_MODEL), 1.0),
        'c_sample': nrm(ks[3], (DEC_BATCH, D_MODEL), 1.0),
        'cache_ckv': nrm(ks[4], (L, DEC_BATCH, PAST_LEN, KV_RANK), 1.0),
        'cache_krope': nrm(ks[5], (L, DEC_BATCH, PAST_LEN, ROPE_DIM), 1.0),
        'state_conv': nrm(ks[6], (L, DEC_BATCH, CONV_W - 1, CONV_DIM), 1.0),
        'state_pool': nrm(ks[7], (L, DEC_BATCH, POOL_MAX - 1, POOL_DIM), 1.0),
        'w_ada': nrm(ks[8], (L, D_MODEL, N_ADA * D_MODEL), 0.5 * D_MODEL ** -0.5),
        'b_ada': nrm(ks[9], (L, N_ADA * D_MODEL), 0.01),
        'norm_ffn1': gain(ks[10], (L, D_MODEL)),
        'w_ffn1_gu': nrm(ks[11], (L, D_MODEL, 2 * D_FF), D_MODEL ** -0.5),
        'w_ffn1_down': nrm(ks[12], (L, D_FF, D_MODEL), D_FF ** -0.5),
        'norm_mix': gain(ks[13], (L, D_MODEL)),
        'w_in': nrm(ks[14], (L, D_MODEL, IN_COLS), D_MODEL ** -0.5),
        'q_norm': gain(ks[15], (L, Q_RANK)),
        'w_uq': nrm(ks[16], (L, Q_RANK, MLA_HEADS * (NOPE_DIM + ROPE_DIM)), Q_RANK ** -0.5),
        'kv_norm': gain(ks[17], (L, KV_RANK)),
        'w_ukv': nrm(ks[18], (L, KV_RANK, MLA_HEADS * (NOPE_DIM + V_DIM)), KV_RANK ** -0.5),
        'conv_w': nrm(ks[19], (L, CONV_W, CONV_DIM), CONV_W ** -0.5),
        'pool_w': nrm(ks[20], (L, POOL_GROUPS, POOL_GROUP_DIM, POOL_GROUP_DIM), POOL_GROUP_DIM ** -0.5),
        'pool_scale': gain(ks[21], (L, POOL_DIM)),
        'w_out': nrm(ks[22], (L, MIX_DIM, D_MODEL), MIX_DIM ** -0.5),
        'norm_ffn2': gain(ks[23], (L, D_MODEL)),
        'w_ffn2_gu': nrm(ks[24], (L, D_MODEL, 2 * D_FF), D_MODEL ** -0.5),
        'w_ffn2_down': nrm(ks[25], (L, D_FF, D_MODEL), D_FF ** -0.5),
        'norm_final': gain(ks[26], (D_MODEL,)),
    }


def reference(x_prompt, x_sample, c_prompt, c_sample, cache_ckv, cache_krope, state_conv, state_pool,
              w_ada, b_ada, norm_ffn1, w_ffn1_gu, w_ffn1_down, norm_mix, w_in, q_norm, w_uq, kv_norm,
              w_ukv, conv_w, pool_w, pool_scale, w_out, norm_ffn2, w_ffn2_gu, w_ffn2_down, norm_final):
    dt = x_prompt.dtype
    bp = x_prompt.shape[0]
    hp, hs = x_prompt, x_sample
    ckv_p, kr_p, cv_p, pl_p = [], [], [], []
    ckv_s, kr_s, cv_s, pl_s = [], [], [], []
    for l in range(DEPTH):
        p = {'w_ada': w_ada[l], 'b_ada': b_ada[l], 'norm_ffn1': norm_ffn1[l],
             'w_ffn1_gu': w_ffn1_gu[l], 'w_ffn1_down': w_ffn1_down[l], 'norm_mix': norm_mix[l],
             'w_in': w_in[l], 'q_norm': q_norm[l], 'w_uq': w_uq[l], 'kv_norm': kv_norm[l],
             'w_ukv': w_ukv[l], 'conv_w': conv_w[l], 'pool_w': pool_w[l], 'pool_scale': pool_scale[l],
             'w_out': w_out[l], 'norm_ffn2': norm_ffn2[l], 'w_ffn2_gu': w_ffn2_gu[l],
             'w_ffn2_down': w_ffn2_down[l]}
        hp, a1, a2, a3, a4 = trunk_layer(
            hp, c_prompt,
            jnp.zeros((bp, 0, KV_RANK), dt), jnp.zeros((bp, 0, ROPE_DIM), dt),
            jnp.zeros((bp, CONV_W - 1, CONV_DIM), dt), jnp.zeros((bp, POOL_MAX - 1, POOL_DIM), dt), p)
        ckv_p.append(a1); kr_p.append(a2); cv_p.append(a3); pl_p.append(a4)
        hs, b1, b2, b3, b4 = trunk_layer(
            hs, c_sample, cache_ckv[l], cache_krope[l], state_conv[l], state_pool[l], p)
        ckv_s.append(b1); kr_s.append(b2); cv_s.append(b3); pl_s.append(b4)
    y_prompt = rms_norm(hp, norm_final)
    y_sample = rms_norm(hs, norm_final)
    return (y_prompt, y_sample,
            jnp.stack(ckv_p), jnp.stack(kr_p), jnp.stack(cv_p), jnp.stack(pl_p),
            jnp.stack(ckv_s), jnp.stack(kr_s), jnp.stack(cv_s), jnp.stack(pl_s))
```

```python
import functools

import jax
import jax.numpy as jnp
import numpy as np
from jax import lax
from jax.experimental import pallas as pl
from jax.experimental.pallas import tpu as pltpu

F32 = jnp.float32
BF16 = jnp.bfloat16

D_MODEL = 1024
DEPTH = 2
CHUNK = 64
EPS = 1e-6
N_ADA = 9
D_FF = 2816
MLA_HEADS = 4
Q_RANK = 256
KV_RANK = 128
NOPE_DIM = 128
ROPE_DIM = 64
V_DIM = 128
ROPE_BASE = 10000.0
ATTN_SCALE = (NOPE_DIM + ROPE_DIM) ** -0.5
CONV_DIM = 256
CONV_W = 3
POOL_WINDOWS = (2, 4, 8, 16)
POOL_GROUPS = 4
POOL_DIM = 256
POOL_GROUP_DIM = POOL_DIM // POOL_GROUPS
POOL_MAX = 16

LANES = 128
KEY_DIM = 2 * LANES
FF_CHUNK = 256
ADA_ROWS = 16
ADA_COLS = 1024
HALO = 16
NEG_BIG = -1e30
VMEM_LIMIT = 56 * 1024 * 1024

IN_CQ, IN_CKV, IN_KR, IN_KRS, IN_CB, IN_CC, IN_CV, IN_PU, IN_END = (
    0, 256, 384, 512, 640, 896, 1152, 1408, 1664)
UQ_NOPE, UQ_ROPE, UQ_ROPES, UQ_END = 0, 512, 1024, 1536


def _rms(x, g):
    return x * lax.rsqrt(jnp.mean(x * x, axis=-1, keepdims=True) + EPS) * g


def _silu(x):
    return x * jax.nn.sigmoid(x)


def _dot(a, b):
    return jnp.dot(a, b, preferred_element_type=F32)


def _const_spec(shape):
    nd = len(shape)
    return pl.BlockSpec(shape, lambda *_: (0,) * nd, pipeline_mode=pl.Buffered(1))


def _ada_kernel(c_ref, w_ref, b_ref, o_ref):
    a = _silu(c_ref[...]).astype(BF16)
    o_ref[0] = _dot(a, w_ref[0].astype(BF16)) + b_ref[0]


def _ada(c_all, w_ada, b_ada):
    n_l, _, n_cols = w_ada.shape
    return pl.pallas_call(
        _ada_kernel,
        grid=(n_l, n_cols // ADA_COLS),
        in_specs=[
            pl.BlockSpec((ADA_ROWS, D_MODEL), lambda l, j: (0, 0)),
            pl.BlockSpec((1, D_MODEL, ADA_COLS), lambda l, j: (l, 0, j)),
            pl.BlockSpec((1, 1, ADA_COLS), lambda l, j: (l, 0, j)),
        ],
        out_specs=pl.BlockSpec((1, ADA_ROWS, ADA_COLS), lambda l, j: (l, 0, j)),
        out_shape=jax.ShapeDtypeStruct((n_l, ADA_ROWS, n_cols), F32),
        compiler_params=pltpu.CompilerParams(
            dimension_semantics=("arbitrary", "arbitrary"), vmem_limit_bytes=VMEM_LIMIT),
        name="ada",
    )(c_all, w_ada, b_ada.reshape(n_l, 1, n_cols))


def _rope_table_kernel(inv_ref, sign_ref, cos_ref, sin_ref, *, pos0, tm):
    row = lax.broadcasted_iota(jnp.int32, (tm, LANES), 0)
    pos = (pos0 + pl.program_id(0) * tm + row).astype(F32)
    ang = pos * inv_ref[...]
    cos_ref[...] = jnp.cos(ang)
    sin_ref[...] = jnp.sin(ang) * sign_ref[...]


def _rope_tables(n_rows, pos0, tm):
    half = ROPE_DIM // 2
    inv = ROPE_BASE ** (-jnp.arange(half, dtype=F32) / half)
    inv = jnp.tile(inv, LANES // half).reshape(1, LANES)
    sign = jnp.tile(jnp.concatenate([-jnp.ones((half,), F32), jnp.ones((half,), F32)]),
                    LANES // ROPE_DIM).reshape(1, LANES)
    return pl.pallas_call(
        functools.partial(_rope_table_kernel, pos0=pos0, tm=tm),
        grid=(n_rows // tm,),
        in_specs=[_const_spec((1, LANES)), _const_spec((1, LANES))],
        out_specs=[pl.BlockSpec((tm, LANES), lambda i: (i, 0))] * 2,
        out_shape=[jax.ShapeDtypeStruct((n_rows, LANES), F32)] * 2,
        compiler_params=pltpu.CompilerParams(dimension_semantics=("arbitrary",)),
        name="rope_tables",
    )(inv, sign)


def _ffn_kernel(*refs, has_mix, final_norm):
    refs = list(refs)
    h_ref = refs.pop(0)
    if has_mix:
        a_ref, sp_ref, wo_ref = refs.pop(0), refs.pop(0), refs.pop(0)
    mod_ref, ng_ref, wgu_ref, wd_ref = refs.pop(0), refs.pop(0), refs.pop(0), refs.pop(0)
    if final_norm:
        nf_ref = refs.pop(0)
    (o_ref,) = refs

    bt, tt, d = h_ref.shape
    m = bt * tt
    x = h_ref[...]
    mod = mod_ref[...]
    row0 = 0
    if has_mix:
        mix_w = a_ref.shape[-1]
        mix = _dot(a_ref[...].reshape(m, mix_w), wo_ref[0:mix_w, :])
        mix = mix + _dot(sp_ref[...].reshape(m, sp_ref.shape[-1]), wo_ref[mix_w:, :])
        x = x + mod[:, 5:6, :] * mix.reshape(bt, tt, d)
        row0 = 6
    shift, scale, gate = mod[:, row0:row0 + 1, :], mod[:, row0 + 1:row0 + 2, :], mod[:, row0 + 2:row0 + 3, :]
    n = _rms(x, ng_ref[...]) * (1.0 + scale) + shift
    nb = n.reshape(m, d).astype(BF16)
    acc = jnp.zeros((m, d), F32)
    for c in range(D_FF // FF_CHUNK):
        g = _dot(nb, wgu_ref[:, c * FF_CHUNK:(c + 1) * FF_CHUNK])
        u = _dot(nb, wgu_ref[:, D_FF + c * FF_CHUNK:D_FF + (c + 1) * FF_CHUNK])
        act = (_silu(g) * u).astype(BF16)
        acc = acc + _dot(act, wd_ref[c * FF_CHUNK:(c + 1) * FF_CHUNK, :])
    y = x + 0.5 * gate * acc.reshape(bt, tt, d)
    if final_norm:
        y = _rms(y, nf_ref[...])
    o_ref[...] = y


def _ffn(h, mod, norm_g, w_gu, w_down, *, bt, tt, mix=None, norm_final=None):
    n_b, n_t, d = h.shape
    grid = (n_b // bt, n_t // tt)
    tile = lambda w: pl.BlockSpec((bt, tt, w), lambda b, t: (b, t, 0))
    args, specs = [h], [tile(d)]
    if mix is not None:
        a, sp, w_out = mix
        args += [a, sp, w_out]
        specs += [tile(a.shape[-1]), tile(sp.shape[-1]), _const_spec(w_out.shape)]
    args += [mod, norm_g.reshape(1, d), w_gu, w_down]
    specs += [pl.BlockSpec((bt, N_ADA, d), lambda b, t: (b, 0, 0)), _const_spec((1, d)),
              _const_spec(w_gu.shape), _const_spec(w_down.shape)]
    if norm_final is not None:
        args.append(norm_final.reshape(1, d))
        specs.append(_const_spec((1, d)))
    return pl.pallas_call(
        functools.partial(_ffn_kernel, has_mix=mix is not None, final_norm=norm_final is not None),
        grid=grid,
        in_specs=specs,
        out_specs=tile(d),
        out_shape=jax.ShapeDtypeStruct(h.shape, F32),
        compiler_params=pltpu.CompilerParams(
            dimension_semantics=("arbitrary", "arbitrary"), vmem_limit_bytes=VMEM_LIMIT),
        name="ffn_mix" if mix is not None else "ffn",
    )(*args)


def _mix_in_kernel(h_ref, mod_ref, ng_ref, win_ref, qn_ref, wuq_ref, kvn_ref, wukt_ref, convw_ref,
                   poolw_ref, pools_ref, cos_ref, sin_ref, cstate_ref, pstate_ref,
                   q_ref, k_ref, ckv_ref, kr_ref, sp_ref, cnew_ref, pnew_ref,
                   eu_ref, ep_ref, s2_ref, s4_ref, s8_ref, *, tm, pos0):
    t = pl.program_id(1)
    x = h_ref[0]
    mod = mod_ref[0]
    n = _rms(x, ng_ref[...]) * (1.0 + mod[4:5, :]) + mod[3:4, :]
    proj = _dot(n.astype(BF16), win_ref[...])
    cos = cos_ref[...]
    sin = sin_ref[...]

    qn = _rms(proj[:, IN_CQ:IN_CKV], qn_ref[...]).astype(BF16)
    qall = _dot(qn, wuq_ref[...])
    for hd in range(MLA_HEADS):
        q_nope = qall[:, UQ_NOPE + hd * NOPE_DIM:UQ_NOPE + (hd + 1) * NOPE_DIM].astype(BF16)
        q_lat = _dot(q_nope, wukt_ref[hd])
        q_rope = (qall[:, UQ_ROPE + hd * LANES:UQ_ROPE + (hd + 1) * LANES] * cos
                  + qall[:, UQ_ROPES + hd * LANES:UQ_ROPES + (hd + 1) * LANES] * sin)
        q_ref[0, hd] = (jnp.concatenate([q_lat, q_rope], axis=-1) * ATTN_SCALE).astype(BF16)

    ckv = _rms(proj[:, IN_CKV:IN_KR], kvn_ref[...])
    kr = proj[:, IN_KR:IN_KRS] * cos + proj[:, IN_KRS:IN_CB] * sin
    ckv_ref[0] = ckv
    kr_ref[0] = kr[:, :ROPE_DIM]
    k_ref[0] = jnp.concatenate([ckv, kr], axis=-1).astype(BF16)

    @pl.when(t == 0)
    def _():
        eu_ref[0:HALO - (CONV_W - 1), :] = jnp.zeros((HALO - (CONV_W - 1), CONV_DIM), F32)
        eu_ref[HALO - (CONV_W - 1):HALO, :] = cstate_ref[0]
        ep_ref[0:1, :] = jnp.zeros((1, POOL_DIM), F32)
        ep_ref[1:HALO, :] = pstate_ref[0]

    u = proj[:, IN_CC:IN_CV] * proj[:, IN_CV:IN_PU]
    eu_ref[HALO:HALO + tm, :] = u
    cw = convw_ref[...]
    y = u * cw[CONV_W - 1:CONV_W, :]
    for j in range(CONV_W - 1):
        back = CONV_W - 1 - j
        y = y + eu_ref[HALO - back:HALO - back + tm, :] * cw[j:j + 1, :]
    s_out = proj[:, IN_CB:IN_CC] * y
    cnew_ref[0] = eu_ref[HALO + tm - (CONV_W - 1):HALO + tm, :]
    eu_ref[0:HALO, :] = eu_ref[tm:tm + HALO, :]

    pu = proj[:, IN_PU:IN_END]
    ep_ref[HALO:HALO + tm, :] = pu
    n_ext = tm + HALO
    s2_ref[1:n_ext, :] = ep_ref[1:n_ext, :] + ep_ref[0:n_ext - 1, :]
    s4_ref[3:n_ext, :] = s2_ref[3:n_ext, :] + s2_ref[1:n_ext - 2, :]
    s8_ref[7:n_ext, :] = s4_ref[7:n_ext, :] + s4_ref[3:n_ext - 4, :]
    s16 = s8_ref[HALO:n_ext, :] + s8_ref[HALO - 8:n_ext - 8, :]
    lane_grp = lax.broadcasted_iota(jnp.int32, (tm, POOL_DIM), 1) >> (POOL_GROUP_DIM.bit_length() - 1)
    wsum = jnp.where(lane_grp == 0, s2_ref[HALO:n_ext, :],
                     jnp.where(lane_grp == 1, s4_ref[HALO:n_ext, :],
                               jnp.where(lane_grp == 2, s8_ref[HALO:n_ext, :], s16)))
    win = jnp.left_shift(2, lane_grp)
    pos = pos0 + t * tm + lax.broadcasted_iota(jnp.int32, (tm, POOL_DIM), 0)
    cnt = jnp.minimum(win, pos + 1).astype(F32)
    dlt = (wsum / cnt - pu).astype(BF16)
    p_out = _dot(dlt, poolw_ref[...]) * pools_ref[...]
    pnew_ref[0] = ep_ref[tm + 1:tm + HALO, :]
    ep_ref[0:HALO, :] = ep_ref[tm:tm + HALO, :]

    sp_ref[0] = jnp.concatenate([s_out, p_out], axis=-1).astype(BF16)


def _mix_in(h, mod, lw, cos_t, sin_t, conv_state, pool_state, *, tm, pos0):
    n_b, n_t, d = h.shape
    grid = (n_b, n_t // tm)
    rows = lambda w: pl.BlockSpec((1, tm, w), lambda b, t: (b, t, 0))
    per_b = lambda r, w: pl.BlockSpec((1, r, w), lambda b, t: (b, 0, 0))
    in_specs = [
        rows(d), per_b(N_ADA, d), _const_spec((1, d)), _const_spec(lw["w_in"].shape),
        _const_spec((1, Q_RANK)), _const_spec(lw["w_uq"].shape), _const_spec((1, KV_RANK)),
        _const_spec(lw["w_ukt"].shape), _const_spec((CONV_W, CONV_DIM)),
        _const_spec((POOL_DIM, POOL_DIM)), _const_spec((1, POOL_DIM)),
        pl.BlockSpec((tm, LANES), lambda b, t: (t, 0)), pl.BlockSpec((tm, LANES), lambda b, t: (t, 0)),
        per_b(CONV_W - 1, CONV_DIM), per_b(POOL_MAX - 1, POOL_DIM),
    ]
    out_specs = [
        pl.BlockSpec((1, MLA_HEADS, tm, KEY_DIM), lambda b, t: (b, 0, t, 0)),
        rows(KEY_DIM), rows(KV_RANK), rows(ROPE_DIM), rows(CONV_DIM + POOL_DIM),
        per_b(CONV_W - 1, CONV_DIM), per_b(POOL_MAX - 1, POOL_DIM),
    ]
    out_shape = [
        jax.ShapeDtypeStruct((n_b, MLA_HEADS, n_t, KEY_DIM), BF16),
        jax.ShapeDtypeStruct((n_b, n_t, KEY_DIM), BF16),
        jax.ShapeDtypeStruct((n_b, n_t, KV_RANK), F32),
        jax.ShapeDtypeStruct((n_b, n_t, ROPE_DIM), F32),
        jax.ShapeDtypeStruct((n_b, n_t, CONV_DIM + POOL_DIM), BF16),
        jax.ShapeDtypeStruct((n_b, CONV_W - 1, CONV_DIM), F32),
        jax.ShapeDtypeStruct((n_b, POOL_MAX - 1, POOL_DIM), F32),
    ]
    ext = pltpu.VMEM((tm + HALO, POOL_DIM), F32)
    return pl.pallas_call(
        functools.partial(_mix_in_kernel, tm=tm, pos0=pos0),
        grid=grid,
        in_specs=in_specs,
        out_specs=out_specs,
        out_shape=out_shape,
        scratch_shapes=[ext, ext, ext, ext, ext],
        compiler_params=pltpu.CompilerParams(
            dimension_semantics=("arbitrary", "arbitrary"), vmem_limit_bytes=VMEM_LIMIT),
        name="mix_in",
    )(h, mod, lw["norm_mix"], lw["w_in"], lw["q_norm"], lw["w_uq"], lw["kv_norm"], lw["w_ukt"],
      lw["conv_w"], lw["pool_bd"], lw["pool_scale"], cos_t, sin_t, conv_state, pool_state)


def _attn_kernel(*refs, tq, tk_past, past_len):
    refs = list(refs)
    q_ref, knew_ref = refs.pop(0), refs.pop(0)
    kpast_ref = refs.pop(0) if past_len else None
    wuv_ref, o_ref, m_ref, l_ref, acc_ref = refs

    i = pl.program_id(1)
    m_rows = MLA_HEADS * tq
    q = q_ref[0].reshape(m_rows, KEY_DIM)
    m_ref[...] = jnp.full(m_ref.shape, NEG_BIG, F32)
    l_ref[...] = jnp.zeros(l_ref.shape, F32)
    acc_ref[...] = jnp.zeros(acc_ref.shape, F32)

    def step(kblk, mask):
        s = lax.dot_general(q, kblk, (((1,), (1,)), ((), ())), preferred_element_type=F32)
        if mask is not None:
            s = jnp.where(mask, s, NEG_BIG)
        m_prev = m_ref[...]
        m_new = jnp.maximum(m_prev, jnp.max(s, axis=1, keepdims=True))
        alpha = jnp.exp(m_prev - m_new)
        p = jnp.exp(s - m_new)
        l_ref[...] = alpha * l_ref[...] + jnp.sum(p, axis=1, keepdims=True)
        acc_ref[...] = alpha * acc_ref[...] + _dot(p.astype(BF16), kblk[:, :KV_RANK])
        m_ref[...] = m_new

    if past_len:
        def past_body(j, carry):
            step(kpast_ref[0, pl.ds(pl.multiple_of(j * tk_past, tk_past), tk_past), :], None)
            return carry
        lax.fori_loop(0, past_len // tk_past, past_body, 0)

    def new_body(j, carry):
        step(knew_ref[0, pl.ds(pl.multiple_of(j * tq, tq), tq), :], None)
        return carry
    lax.fori_loop(0, i, new_body, 0)

    row = lax.broadcasted_iota(jnp.int32, (m_rows, tq), 0) & (tq - 1)
    col = lax.broadcasted_iota(jnp.int32, (m_rows, tq), 1)
    chunk_shift = CHUNK.bit_length() - 1
    mask = (col >> chunk_shift) <= (row >> chunk_shift)
    step(knew_ref[0, pl.ds(pl.multiple_of(i * tq, tq), tq), :], mask)

    o = (acc_ref[...] / l_ref[...]).astype(BF16)
    o_ref[0] = jnp.concatenate(
        [_dot(o[hd * tq:(hd + 1) * tq, :], wuv_ref[hd]) for hd in range(MLA_HEADS)], axis=-1).astype(BF16)


def _attn(q, k_new, k_past, w_uv, *, tq, tk_past):
    n_b, _, n_t, _ = q.shape
    past_len = 0 if k_past is None else k_past.shape[1]
    assert (past_len % CHUNK == 0) and (tq % CHUNK == 0 or n_t == tq <= CHUNK)
    args = [q, k_new]
    specs = [pl.BlockSpec((1, MLA_HEADS, tq, KEY_DIM), lambda b, i: (b, 0, i, 0)),
             pl.BlockSpec((1, n_t, KEY_DIM), lambda b, i: (b, 0, 0))]
    if past_len:
        args.append(k_past)
        specs.append(pl.BlockSpec((1, past_len, KEY_DIM), lambda b, i: (b, 0, 0)))
    args.append(w_uv)
    specs.append(_const_spec(w_uv.shape))
    m_rows = MLA_HEADS * tq
    return pl.pallas_call(
        functools.partial(_attn_kernel, tq=tq, tk_past=tk_past, past_len=past_len),
        grid=(n_b, n_t // tq),
        in_specs=specs,
        out_specs=pl.BlockSpec((1, tq, MLA_HEADS * V_DIM), lambda b, i: (b, i, 0)),
        out_shape=jax.ShapeDtypeStruct((n_b, n_t, MLA_HEADS * V_DIM), BF16),
        scratch_shapes=[pltpu.VMEM((m_rows, 1), F32), pltpu.VMEM((m_rows, 1), F32),
                        pltpu.VMEM((m_rows, KV_RANK), F32)],
        compiler_params=pltpu.CompilerParams(
            dimension_semantics=("arbitrary", "arbitrary"), vmem_limit_bytes=VMEM_LIMIT),
        name="attn",
    )(*args)


def _prep_layer(l, w_ffn1_gu, w_ffn1_down, norm_ffn1, norm_mix, w_in, q_norm, w_uq, kv_norm, w_ukv,
                conv_w, pool_w, pool_scale, w_out, norm_ffn2, w_ffn2_gu, w_ffn2_down):
    half = ROPE_DIM // 2
    swap = lambda w: jnp.concatenate([w[:, half:], w[:, :half]], axis=1)
    pad = lambda w: jnp.pad(w, ((0, 0), (0, LANES - w.shape[1])))
    wi = w_in[l]
    cq, ckv, kr, rest = (wi[:, :Q_RANK], wi[:, Q_RANK:Q_RANK + KV_RANK],
                         wi[:, Q_RANK + KV_RANK:Q_RANK + KV_RANK + ROPE_DIM],
                         wi[:, Q_RANK + KV_RANK + ROPE_DIM:])
    w_in_p = jnp.concatenate([cq, ckv, pad(kr), pad(swap(kr)), rest], axis=1).astype(BF16)

    wq = w_uq[l].reshape(Q_RANK, MLA_HEADS, NOPE_DIM + ROPE_DIM)
    nope = wq[:, :, :NOPE_DIM].reshape(Q_RANK, MLA_HEADS * NOPE_DIM)
    rope_cols = [wq[:, hd, NOPE_DIM:] for hd in range(MLA_HEADS)]
    w_uq_p = jnp.concatenate([nope] + [pad(r) for r in rope_cols] + [pad(swap(r)) for r in rope_cols],
                             axis=1).astype(BF16)

    wkv = w_ukv[l].reshape(KV_RANK, MLA_HEADS, NOPE_DIM + V_DIM)
    w_ukt = jnp.transpose(wkv[:, :, :NOPE_DIM], (1, 2, 0)).astype(BF16)
    w_uv = jnp.transpose(wkv[:, :, NOPE_DIM:], (1, 0, 2)).astype(BF16)

    pool_bd = jnp.zeros((POOL_GROUPS, POOL_GROUP_DIM, POOL_GROUPS, POOL_GROUP_DIM), F32)
    for g in range(POOL_GROUPS):
        pool_bd = pool_bd.at[g, :, g, :].set(pool_w[l, g])
    pool_bd = pool_bd.reshape(POOL_DIM, POOL_DIM).astype(BF16)

    return {
        "norm_ffn1": norm_ffn1[l], "w_ffn1_gu": w_ffn1_gu[l].astype(BF16),
        "w_ffn1_down": w_ffn1_down[l].astype(BF16),
        "norm_mix": norm_mix[l].reshape(1, D_MODEL), "w_in": w_in_p,
        "q_norm": q_norm[l].reshape(1, Q_RANK), "w_uq": w_uq_p,
        "kv_norm": kv_norm[l].reshape(1, KV_RANK), "w_ukt": w_ukt, "w_uv": w_uv,
        "conv_w": conv_w[l], "pool_bd": pool_bd, "pool_scale": pool_scale[l].reshape(1, POOL_DIM),
        "w_out": w_out[l].astype(BF16),
        "norm_ffn2": norm_ffn2[l], "w_ffn2_gu": w_ffn2_gu[l].astype(BF16),
        "w_ffn2_down": w_ffn2_down[l].astype(BF16),
    }


def _past_keys(cache_ckv_l, cache_krope_l):
    pad = jnp.zeros(cache_ckv_l.shape[:-1] + (KEY_DIM - KV_RANK - ROPE_DIM,), BF16)
    return jnp.concatenate([cache_ckv_l.astype(BF16), cache_krope_l.astype(BF16), pad], axis=-1)


def _trunk_layer(h, mod, lw, tables, conv_state, pool_state, k_past, norm_final, *, bt, tt, tm, tq, pos0):
    h1 = _ffn(h, mod, lw["norm_ffn1"], lw["w_ffn1_gu"], lw["w_ffn1_down"], bt=bt, tt=tt)
    q, k_new, ckv, kr, sp, conv_new, pool_new = _mix_in(
        h1, mod, lw, tables[0], tables[1], conv_state, pool_state, tm=tm, pos0=pos0)
    a = _attn(q, k_new, k_past, lw["w_uv"], tq=tq, tk_past=512)
    h3 = _ffn(h1, mod, lw["norm_ffn2"], lw["w_ffn2_gu"], lw["w_ffn2_down"], bt=bt, tt=tt,
              mix=(a, sp, lw["w_out"]), norm_final=norm_final)
    return h3, ckv, kr, conv_new, pool_new


def kernel(x_prompt, x_sample, c_prompt, c_sample, cache_ckv, cache_krope, state_conv, state_pool, w_ada, b_ada, norm_ffn1, w_ffn1_gu, w_ffn1_down, norm_mix, w_in, q_norm, w_uq, kv_norm, w_ukv, conv_w, pool_w, pool_scale, w_out, norm_ffn2, w_ffn2_gu, w_ffn2_down, norm_final):
    bp, seq, _ = x_prompt.shape
    bs, dec_seq, _ = x_sample.shape
    past_len = cache_ckv.shape[2]
    assert bp + bs <= ADA_ROWS

    c_all = jnp.concatenate(
        [c_prompt, c_sample, jnp.zeros((ADA_ROWS - bp - bs, D_MODEL), F32)], axis=0)
    mods = _ada(c_all, w_ada, b_ada).reshape(DEPTH, ADA_ROWS, N_ADA, D_MODEL)

    tm_p = 512
    tables_p = _rope_tables(seq, 0, tm_p)
    tables_s = _rope_tables(dec_seq, past_len, dec_seq)

    hp, hs = x_prompt, x_sample
    outs_p, outs_s = [], []
    zero_conv = jnp.zeros((bp, CONV_W - 1, CONV_DIM), F32)
    zero_pool = jnp.zeros((bp, POOL_MAX - 1, POOL_DIM), F32)
    for l in range(DEPTH):
        lw = _prep_layer(l, w_ffn1_gu, w_ffn1_down, norm_ffn1, norm_mix, w_in, q_norm, w_uq, kv_norm,
                         w_ukv, conv_w, pool_w, pool_scale, w_out, norm_ffn2, w_ffn2_gu, w_ffn2_down)
        nf = norm_final if l == DEPTH - 1 else None
        hp, *op = _trunk_layer(hp, mods[l, :bp], lw, tables_p, zero_conv, zero_pool, None, nf,
                               bt=1, tt=tm_p, tm=tm_p, tq=256, pos0=0)
        hs, *os_ = _trunk_layer(hs, mods[l, bp:bp + bs], lw, tables_s, state_conv[l], state_pool[l],
                                _past_keys(cache_ckv[l], cache_krope[l]), nf,
                                bt=bs, tt=dec_seq, tm=dec_seq, tq=dec_seq, pos0=past_len)
        outs_p.append(op)
        outs_s.append(os_)

    stack = lambda outs, k: jnp.stack([o[k] for o in outs])
    return (hp, hs,
            stack(outs_p, 0), stack(outs_p, 1), stack(outs_p, 2), stack(outs_p, 3),
            stack(outs_s, 0), stack(outs_s, 1), stack(outs_s, 2), stack(outs_s, 3))
```

```python
import functools

import jax
import jax.numpy as jnp
import numpy as np
from jax import lax
from jax.experimental import pallas as pl
from jax.experimental.pallas import tpu as pltpu

F32 = jnp.float32
BF16 = jnp.bfloat16

D_MODEL = 1024
DEPTH = 2
CHUNK = 64
EPS = 1e-6
N_ADA = 9
D_FF = 2816
MLA_HEADS = 4
Q_RANK = 256
KV_RANK = 128
NOPE_DIM = 128
ROPE_DIM = 64
V_DIM = 128
ROPE_BASE = 10000.0
ATTN_SCALE = (NOPE_DIM + ROPE_DIM) ** -0.5
CONV_DIM = 256
CONV_W = 3
POOL_WINDOWS = (2, 4, 8, 16)
POOL_GROUPS = 4
POOL_DIM = 256
POOL_GROUP_DIM = POOL_DIM // POOL_GROUPS
POOL_MAX = 16

LANES = 128
KEY_DIM = 2 * LANES
FF_CHUNK = 256
ADA_ROWS = 16
ADA_COLS = 1024
HALO = 16
NEG_BIG = -1e30
VMEM_LIMIT = 56 * 1024 * 1024

IN_CQ, IN_CKV, IN_KR, IN_KRS, IN_CB, IN_CC, IN_CV, IN_PU, IN_END = (
    0, 256, 384, 512, 640, 896, 1152, 1408, 1664)
UQ_NOPE, UQ_ROPE, UQ_ROPES, UQ_END = 0, 512, 1024, 1536


def _rms(x, g):
    return x * lax.rsqrt(jnp.mean(x * x, axis=-1, keepdims=True) + EPS) * g


def _silu(x):
    return x * jax.nn.sigmoid(x)


def _dot(a, b):
    return jnp.dot(a, b, preferred_element_type=F32)


def _const_spec(shape):
    nd = len(shape)
    return pl.BlockSpec(shape, lambda *_: (0,) * nd, pipeline_mode=pl.Buffered(1))


def _ada_kernel(c_ref, w_ref, b_ref, o_ref):
    a = _silu(c_ref[...]).astype(BF16)
    o_ref[0] = _dot(a, w_ref[0].astype(BF16)) + b_ref[0]


def _ada(c_all, w_ada, b_ada):
    n_l, _, n_cols = w_ada.shape
    return pl.pallas_call(
        _ada_kernel,
        grid=(n_l, n_cols // ADA_COLS),
        in_specs=[
            pl.BlockSpec((ADA_ROWS, D_MODEL), lambda l, j: (0, 0)),
            pl.BlockSpec((1, D_MODEL, ADA_COLS), lambda l, j: (l, 0, j)),
            pl.BlockSpec((1, 1, ADA_COLS), lambda l, j: (l, 0, j)),
        ],
        out_specs=pl.BlockSpec((1, ADA_ROWS, ADA_COLS), lambda l, j: (l, 0, j)),
        out_shape=jax.ShapeDtypeStruct((n_l, ADA_ROWS, n_cols), F32),
        compiler_params=pltpu.CompilerParams(
            dimension_semantics=("arbitrary", "arbitrary"), vmem_limit_bytes=VMEM_LIMIT),
        name="ada",
    )(c_all, w_ada, b_ada.reshape(n_l, 1, n_cols))


def _rope_table_kernel(inv_ref, sign_ref, cos_ref, sin_ref, *, pos0, tm):
    row = lax.broadcasted_iota(jnp.int32, (tm, LANES), 0)
    pos = (pos0 + pl.program_id(0) * tm + row).astype(F32)
    ang = pos * inv_ref[...]
    cos_ref[...] = jnp.cos(ang)
    sin_ref[...] = jnp.sin(ang) * sign_ref[...]


def _rope_tables(n_rows, pos0, tm):
    half = ROPE_DIM // 2
    inv = ROPE_BASE ** (-jnp.arange(half, dtype=F32) / half)
    inv = jnp.tile(inv, LANES // half).reshape(1, LANES)
    sign = jnp.tile(jnp.concatenate([-jnp.ones((half,), F32), jnp.ones((half,), F32)]),
                    LANES // ROPE_DIM).reshape(1, LANES)
    return pl.pallas_call(
        functools.partial(_rope_table_kernel, pos0=pos0, tm=tm),
        grid=(n_rows // tm,),
        in_specs=[_const_spec((1, LANES)), _const_spec((1, LANES))],
        out_specs=[pl.BlockSpec((tm, LANES), lambda i: (i, 0))] * 2,
        out_shape=[jax.ShapeDtypeStruct((n_rows, LANES), F32)] * 2,
        compiler_params=pltpu.CompilerParams(dimension_semantics=("arbitrary",)),
        name="rope_tables",
    )(inv, sign)


def _ffn_kernel(*refs, has_mix, final_norm):
    refs = list(refs)
    h_ref = refs.pop(0)
    if has_mix:
        a_ref, sp_ref, wo_ref = refs.pop(0), refs.pop(0), refs.pop(0)
    mod_ref, ng_ref, wgu_ref, wd_ref = refs.pop(0), refs.pop(0), refs.pop(0), refs.pop(0)
    if final_norm:
        nf_ref = refs.pop(0)
    (o_ref,) = refs

    bt, tt, d = h_ref.shape
    m = bt * tt
    x = h_ref[...]
    mod = mod_ref[...]
    row0 = 0
    if has_mix:
        mix_w = a_ref.shape[-1]
        mix = _dot(a_ref[...].reshape(m, mix_w), wo_ref[0:mix_w, :])
        mix = mix + _dot(sp_ref[...].reshape(m, sp_ref.shape[-1]), wo_ref[mix_w:, :])
        x = x + mod[:, 5:6, :] * mix.reshape(bt, tt, d)
        row0 = 6
    shift, scale, gate = mod[:, row0:row0 + 1, :], mod[:, row0 + 1:row0 + 2, :], mod[:, row0 + 2:row0 + 3, :]
    n = _rms(x, ng_ref[...]) * (1.0 + scale) + shift
    nb = n.reshape(m, d).astype(BF16)
    acc = jnp.zeros((m, d), F32)
    for c in range(D_FF // FF_CHUNK):
        g = _dot(nb, wgu_ref[:, c * FF_CHUNK:(c + 1) * FF_CHUNK])
        u = _dot(nb, wgu_ref[:, D_FF + c * FF_CHUNK:D_FF + (c + 1) * FF_CHUNK])
        act = (_silu(g) * u).astype(BF16)
        acc = acc + _dot(act, wd_ref[c * FF_CHUNK:(c + 1) * FF_CHUNK, :])
    y = x + 0.5 * gate * acc.reshape(bt, tt, d)
    if final_norm:
        y = _rms(y, nf_ref[...])
    o_ref[...] = y


def _ffn(h, mod, norm_g, w_gu, w_down, *, bt, tt, mix=None, norm_final=None):
    n_b, n_t, d = h.shape
    grid = (n_b // bt, n_t // tt)
    tile = lambda w: pl.BlockSpec((bt, tt, w), lambda b, t: (b, t, 0))
    args, specs = [h], [tile(d)]
    if mix is not None:
        a, sp, w_out = mix
        args += [a, sp, w_out]
        specs += [tile(a.shape[-1]), tile(sp.shape[-1]), _const_spec(w_out.shape)]
    args += [mod, norm_g.reshape(1, d), w_gu, w_down]
    specs += [pl.BlockSpec((bt, N_ADA, d), lambda b, t: (b, 0, 0)), _const_spec((1, d)),
              _const_spec(w_gu.shape), _const_spec(w_down.shape)]
    if norm_final is not None:
        args.append(norm_final.reshape(1, d))
        specs.append(_const_spec((1, d)))
    return pl.pallas_call(
        functools.partial(_ffn_kernel, has_mix=mix is not None, final_norm=norm_final is not None),
        grid=grid,
        in_specs=specs,
        out_specs=tile(d),
        out_shape=jax.ShapeDtypeStruct(h.shape, F32),
        compiler_params=pltpu.CompilerParams(
            dimension_semantics=("arbitrary", "arbitrary"), vmem_limit_bytes=VMEM_LIMIT),
        name="ffn_mix" if mix is not None else "ffn",
    )(*args)


def _mix_in_kernel(h_ref, mod_ref, ng_ref, win_ref, qn_ref, wuq_ref, kvn_ref, wukt_ref, convw_ref,
                   poolw_ref, pools_ref, cos_ref, sin_ref, cstate_ref, pstate_ref,
                   q_ref, k_ref, ckv_ref, kr_ref, sp_ref, cnew_ref, pnew_ref,
                   eu_ref, ep_ref, s2_ref, s4_ref, s8_ref, *, tm, pos0):
    t = pl.program_id(1)
    x = h_ref[0]
    mod = mod_ref[0]
    n = _rms(x, ng_ref[...]) * (1.0 + mod[4:5, :]) + mod[3:4, :]
    proj = _dot(n.astype(BF16), win_ref[...])
    cos = cos_ref[...]
    sin = sin_ref[...]

    qn = _rms(proj[:, IN_CQ:IN_CKV], qn_ref[...]).astype(BF16)
    qall = _dot(qn, wuq_ref[...])
    for hd in range(MLA_HEADS):
        q_nope = qall[:, UQ_NOPE + hd * NOPE_DIM:UQ_NOPE + (hd + 1) * NOPE_DIM].astype(BF16)
        q_lat = _dot(q_nope, wukt_ref[hd])
        q_rope = (qall[:, UQ_ROPE + hd * LANES:UQ_ROPE + (hd + 1) * LANES] * cos
                  + qall[:, UQ_ROPES + hd * LANES:UQ_ROPES + (hd + 1) * LANES] * sin)
        q_ref[0, hd] = (jnp.concatenate([q_lat, q_rope], axis=-1) * ATTN_SCALE).astype(BF16)

    ckv = _rms(proj[:, IN_CKV:IN_KR], kvn_ref[...])
    kr = proj[:, IN_KR:IN_KRS] * cos + proj[:, IN_KRS:IN_CB] * sin
    ckv_ref[0] = ckv
    kr_ref[0] = kr[:, :ROPE_DIM]
    lane = lax.broadcasted_iota(jnp.int32, (tm, LANES), 1)
    k_ref[0] = jnp.concatenate([ckv, jnp.where(lane < ROPE_DIM, kr, 1.0)], axis=-1).astype(BF16)

    @pl.when(t == 0)
    def _():
        eu_ref[0:HALO - (CONV_W - 1), :] = jnp.zeros((HALO - (CONV_W - 1), CONV_DIM), F32)
        eu_ref[HALO - (CONV_W - 1):HALO, :] = cstate_ref[0]
        ep_ref[0:1, :] = jnp.zeros((1, POOL_DIM), F32)
        ep_ref[1:HALO, :] = pstate_ref[0]

    u = proj[:, IN_CC:IN_CV] * proj[:, IN_CV:IN_PU]
    eu_ref[HALO:HALO + tm, :] = u
    cw = convw_ref[...]
    y = u * cw[CONV_W - 1:CONV_W, :]
    for j in range(CONV_W - 1):
        back = CONV_W - 1 - j
        y = y + eu_ref[HALO - back:HALO - back + tm, :] * cw[j:j + 1, :]
    s_out = proj[:, IN_CB:IN_CC] * y
    cnew_ref[0] = eu_ref[HALO + tm - (CONV_W - 1):HALO + tm, :]
    eu_ref[0:HALO, :] = eu_ref[tm:tm + HALO, :]

    pu = proj[:, IN_PU:IN_END]
    ep_ref[HALO:HALO + tm, :] = pu
    n_ext = tm + HALO
    s2_ref[1:n_ext, :] = ep_ref[1:n_ext, :] + ep_ref[0:n_ext - 1, :]
    s4_ref[3:n_ext, :] = s2_ref[3:n_ext, :] + s2_ref[1:n_ext - 2, :]
    s8_ref[7:n_ext, :] = s4_ref[7:n_ext, :] + s4_ref[3:n_ext - 4, :]
    s16 = s8_ref[HALO:n_ext, :] + s8_ref[HALO - 8:n_ext - 8, :]
    lane_grp = lax.broadcasted_iota(jnp.int32, (tm, POOL_DIM), 1) >> (POOL_GROUP_DIM.bit_length() - 1)
    wsum = jnp.where(lane_grp == 0, s2_ref[HALO:n_ext, :],
                     jnp.where(lane_grp == 1, s4_ref[HALO:n_ext, :],
                               jnp.where(lane_grp == 2, s8_ref[HALO:n_ext, :], s16)))
    win = jnp.left_shift(2, lane_grp)
    pos = pos0 + t * tm + lax.broadcasted_iota(jnp.int32, (tm, POOL_DIM), 0)
    cnt = jnp.minimum(win, pos + 1).astype(F32)
    dlt = (wsum / cnt - pu).astype(BF16)
    p_out = _dot(dlt, poolw_ref[...]) * pools_ref[...]
    pnew_ref[0] = ep_ref[tm + 1:tm + HALO, :]
    ep_ref[0:HALO, :] = ep_ref[tm:tm + HALO, :]

    sp_ref[0] = jnp.concatenate([s_out, p_out], axis=-1).astype(BF16)


def _mix_in(h, mod, lw, cos_t, sin_t, conv_state, pool_state, *, tm, pos0):
    n_b, n_t, d = h.shape
    grid = (n_b, n_t // tm)
    rows = lambda w: pl.BlockSpec((1, tm, w), lambda b, t: (b, t, 0))
    per_b = lambda r, w: pl.BlockSpec((1, r, w), lambda b, t: (b, 0, 0))
    in_specs = [
        rows(d), per_b(N_ADA, d), _const_spec((1, d)), _const_spec(lw["w_in"].shape),
        _const_spec((1, Q_RANK)), _const_spec(lw["w_uq"].shape), _const_spec((1, KV_RANK)),
        _const_spec(lw["w_ukt"].shape), _const_spec((CONV_W, CONV_DIM)),
        _const_spec((POOL_DIM, POOL_DIM)), _const_spec((1, POOL_DIM)),
        pl.BlockSpec((tm, LANES), lambda b, t: (t, 0)), pl.BlockSpec((tm, LANES), lambda b, t: (t, 0)),
        per_b(CONV_W - 1, CONV_DIM), per_b(POOL_MAX - 1, POOL_DIM),
    ]
    out_specs = [
        pl.BlockSpec((1, MLA_HEADS, tm, KEY_DIM), lambda b, t: (b, 0, t, 0)),
        rows(KEY_DIM), rows(KV_RANK), rows(ROPE_DIM), rows(CONV_DIM + POOL_DIM),
        per_b(CONV_W - 1, CONV_DIM), per_b(POOL_MAX - 1, POOL_DIM),
    ]
    out_shape = [
        jax.ShapeDtypeStruct((n_b, MLA_HEADS, n_t, KEY_DIM), BF16),
        jax.ShapeDtypeStruct((n_b, n_t, KEY_DIM), BF16),
        jax.ShapeDtypeStruct((n_b, n_t, KV_RANK), F32),
        jax.ShapeDtypeStruct((n_b, n_t, ROPE_DIM), F32),
        jax.ShapeDtypeStruct((n_b, n_t, CONV_DIM + POOL_DIM), BF16),
        jax.ShapeDtypeStruct((n_b, CONV_W - 1, CONV_DIM), F32),
        jax.ShapeDtypeStruct((n_b, POOL_MAX - 1, POOL_DIM), F32),
    ]
    ext = pltpu.VMEM((tm + HALO, POOL_DIM), F32)
    return pl.pallas_call(
        functools.partial(_mix_in_kernel, tm=tm, pos0=pos0),
        grid=grid,
        in_specs=in_specs,
        out_specs=out_specs,
        out_shape=out_shape,
        scratch_shapes=[ext, ext, ext, ext, ext],
        compiler_params=pltpu.CompilerParams(
            dimension_semantics=("arbitrary", "arbitrary"), vmem_limit_bytes=VMEM_LIMIT),
        name="mix_in",
    )(h, mod, lw["norm_mix"], lw["w_in"], lw["q_norm"], lw["w_uq"], lw["kv_norm"], lw["w_ukt"],
      lw["conv_w"], lw["pool_bd"], lw["pool_scale"], cos_t, sin_t, conv_state, pool_state)


def _attn_kernel(*refs, tq, tk, tk_past, past_len, group_rows):
    refs = list(refs)
    q_ref, knew_ref = refs.pop(0), refs.pop(0)
    kpast_ref = refs.pop(0) if past_len else None
    wuv_ref, o_ref, m_ref, acc_ref = refs

    i = pl.program_id(1)
    m_rows = MLA_HEADS * tq
    n_groups = m_rows // group_rows
    group_heads = group_rows // tq
    m_ref[...] = jnp.full(m_ref.shape, NEG_BIG, F32)
    acc_ref[...] = jnp.zeros(acc_ref.shape, F32)

    def lane_tile(x, width):
        return x[:, :width] if width < LANES else pltpu.repeat(x, width // LANES, axis=1)

    def step(kblk, mask):
        width = kblk.shape[0]
        for g in range(n_groups):
            rows = slice(g * group_rows, (g + 1) * group_rows)
            q = q_ref[0, g * group_heads:(g + 1) * group_heads].reshape(group_rows, KEY_DIM)
            s = lax.dot_general(q, kblk, (((1,), (1,)), ((), ())), preferred_element_type=F32)
            if mask is not None:
                s = jnp.where(mask, s, NEG_BIG)
            m_prev = m_ref[rows, :]
            m_new = jnp.maximum(m_prev, jnp.max(s, axis=1, keepdims=True))
            alpha = jnp.exp(m_prev - m_new)
            p = jnp.exp(s - lane_tile(m_new, width)).astype(BF16)
            acc_ref[rows, :] = lane_tile(alpha, KEY_DIM) * acc_ref[rows, :] + _dot(p, kblk)
            m_ref[rows, :] = m_new

    if past_len:
        def past_body(j, carry):
            step(kpast_ref[0, pl.ds(pl.multiple_of(j * tk_past, tk_past), tk_past), :], None)
            return carry
        lax.fori_loop(0, past_len // tk_past, past_body, 0)

    def new_body(j, carry):
        step(knew_ref[0, pl.ds(pl.multiple_of(j * tk, tk), tk), :], None)
        return carry
    lax.fori_loop(0, i * (tq // tk), new_body, 0)

    row = lax.broadcasted_iota(jnp.int32, (group_rows, tq), 0) & (tq - 1)
    col = lax.broadcasted_iota(jnp.int32, (group_rows, tq), 1)
    chunk_shift = CHUNK.bit_length() - 1
    mask = (col >> chunk_shift) <= (row >> chunk_shift)
    step(knew_ref[0, pl.ds(pl.multiple_of(i * tq, tq), tq), :], mask)

    heads = []
    for hd in range(MLA_HEADS):
        acc = acc_ref[hd * tq:(hd + 1) * tq, :]
        denom = acc[:, KV_RANK + ROPE_DIM:KV_RANK + ROPE_DIM + 1]
        heads.append(_dot((acc[:, :KV_RANK] / denom).astype(BF16), wuv_ref[hd]))
    o_ref[0] = jnp.concatenate(heads, axis=-1).astype(BF16)


def _attn(q, k_new, k_past, w_uv, *, tq, tk, tk_past, group_rows):
    n_b, _, n_t, _ = q.shape
    past_len = 0 if k_past is None else k_past.shape[1]
    assert (past_len % CHUNK == 0) and (tq % CHUNK == 0 or n_t == tq <= CHUNK)
    m_rows = MLA_HEADS * tq
    assert tq % tk == 0 and m_rows % group_rows == 0 and group_rows % tq == 0
    args = [q, k_new]
    specs = [pl.BlockSpec((1, MLA_HEADS, tq, KEY_DIM), lambda b, i: (b, 0, i, 0)),
             pl.BlockSpec((1, n_t, KEY_DIM), lambda b, i: (b, 0, 0), pipeline_mode=pl.Buffered(1))]
    if past_len:
        args.append(k_past)
        specs.append(pl.BlockSpec((1, past_len, KEY_DIM), lambda b, i: (b, 0, 0)))
    args.append(w_uv)
    specs.append(_const_spec(w_uv.shape))
    return pl.pallas_call(
        functools.partial(_attn_kernel, tq=tq, tk=tk, tk_past=tk_past, past_len=past_len,
                          group_rows=group_rows),
        grid=(n_b, n_t // tq),
        in_specs=specs,
        out_specs=pl.BlockSpec((1, tq, MLA_HEADS * V_DIM), lambda b, i: (b, i, 0)),
        out_shape=jax.ShapeDtypeStruct((n_b, n_t, MLA_HEADS * V_DIM), BF16),
        scratch_shapes=[pltpu.VMEM((m_rows, LANES), F32), pltpu.VMEM((m_rows, KEY_DIM), F32)],
        compiler_params=pltpu.CompilerParams(
            dimension_semantics=("arbitrary", "arbitrary"), vmem_limit_bytes=VMEM_LIMIT),
        name="attn",
    )(*args)


def _prep_layer(l, w_ffn1_gu, w_ffn1_down, norm_ffn1, norm_mix, w_in, q_norm, w_uq, kv_norm, w_ukv,
                conv_w, pool_w, pool_scale, w_out, norm_ffn2, w_ffn2_gu, w_ffn2_down):
    half = ROPE_DIM // 2
    swap = lambda w: jnp.concatenate([w[:, half:], w[:, :half]], axis=1)
    pad = lambda w: jnp.pad(w, ((0, 0), (0, LANES - w.shape[1])))
    wi = w_in[l]
    cq, ckv, kr, rest = (wi[:, :Q_RANK], wi[:, Q_RANK:Q_RANK + KV_RANK],
                         wi[:, Q_RANK + KV_RANK:Q_RANK + KV_RANK + ROPE_DIM],
                         wi[:, Q_RANK + KV_RANK + ROPE_DIM:])
    w_in_p = jnp.concatenate([cq, ckv, pad(kr), pad(swap(kr)), rest], axis=1).astype(BF16)

    wq = w_uq[l].reshape(Q_RANK, MLA_HEADS, NOPE_DIM + ROPE_DIM)
    nope = wq[:, :, :NOPE_DIM].reshape(Q_RANK, MLA_HEADS * NOPE_DIM)
    rope_cols = [wq[:, hd, NOPE_DIM:] for hd in range(MLA_HEADS)]
    w_uq_p = jnp.concatenate([nope] + [pad(r) for r in rope_cols] + [pad(swap(r)) for r in rope_cols],
                             axis=1).astype(BF16)

    wkv = w_ukv[l].reshape(KV_RANK, MLA_HEADS, NOPE_DIM + V_DIM)
    w_ukt = jnp.transpose(wkv[:, :, :NOPE_DIM], (1, 2, 0)).astype(BF16)
    w_uv = jnp.transpose(wkv[:, :, NOPE_DIM:], (1, 0, 2)).astype(BF16)

    pool_bd = jnp.zeros((POOL_GROUPS, POOL_GROUP_DIM, POOL_GROUPS, POOL_GROUP_DIM), F32)
    for g in range(POOL_GROUPS):
        pool_bd = pool_bd.at[g, :, g, :].set(pool_w[l, g])
    pool_bd = pool_bd.reshape(POOL_DIM, POOL_DIM).astype(BF16)

    return {
        "norm_ffn1": norm_ffn1[l], "w_ffn1_gu": w_ffn1_gu[l].astype(BF16),
        "w_ffn1_down": w_ffn1_down[l].astype(BF16),
        "norm_mix": norm_mix[l].reshape(1, D_MODEL), "w_in": w_in_p,
        "q_norm": q_norm[l].reshape(1, Q_RANK), "w_uq": w_uq_p,
        "kv_norm": kv_norm[l].reshape(1, KV_RANK), "w_ukt": w_ukt, "w_uv": w_uv,
        "conv_w": conv_w[l], "pool_bd": pool_bd, "pool_scale": pool_scale[l].reshape(1, POOL_DIM),
        "w_out": w_out[l].astype(BF16),
        "norm_ffn2": norm_ffn2[l], "w_ffn2_gu": w_ffn2_gu[l].astype(BF16),
        "w_ffn2_down": w_ffn2_down[l].astype(BF16),
    }


def _past_keys(cache_ckv_l, cache_krope_l):
    pad = jnp.ones(cache_ckv_l.shape[:-1] + (KEY_DIM - KV_RANK - ROPE_DIM,), BF16)
    return jnp.concatenate([cache_ckv_l.astype(BF16), cache_krope_l.astype(BF16), pad], axis=-1)


def _trunk_layer(h, mod, lw, tables, conv_state, pool_state, k_past, norm_final, *, bt, tt, tm, tq,
                 group_rows, pos0):
    h1 = _ffn(h, mod, lw["norm_ffn1"], lw["w_ffn1_gu"], lw["w_ffn1_down"], bt=bt, tt=tt)
    q, k_new, ckv, kr, sp, conv_new, pool_new = _mix_in(
        h1, mod, lw, tables[0], tables[1], conv_state, pool_state, tm=tm, pos0=pos0)
    a = _attn(q, k_new, k_past, lw["w_uv"], tq=tq, tk=tq, tk_past=512, group_rows=group_rows)
    h3 = _ffn(h1, mod, lw["norm_ffn2"], lw["w_ffn2_gu"], lw["w_ffn2_down"], bt=bt, tt=tt,
              mix=(a, sp, lw["w_out"]), norm_final=norm_final)
    return h3, ckv, kr, conv_new, pool_new


def kernel(x_prompt, x_sample, c_prompt, c_sample, cache_ckv, cache_krope, state_conv, state_pool, w_ada, b_ada, norm_ffn1, w_ffn1_gu, w_ffn1_down, norm_mix, w_in, q_norm, w_uq, kv_norm, w_ukv, conv_w, pool_w, pool_scale, w_out, norm_ffn2, w_ffn2_gu, w_ffn2_down, norm_final):
    bp, seq, _ = x_prompt.shape
    bs, dec_seq, _ = x_sample.shape
    past_len = cache_ckv.shape[2]
    assert bp + bs <= ADA_ROWS

    c_all = jnp.concatenate(
        [c_prompt, c_sample, jnp.zeros((ADA_ROWS - bp - bs, D_MODEL), F32)], axis=0)
    mods = _ada(c_all, w_ada, b_ada).reshape(DEPTH, ADA_ROWS, N_ADA, D_MODEL)

    tm_p = 512
    tables_p = _rope_tables(seq, 0, tm_p)
    tables_s = _rope_tables(dec_seq, past_len, dec_seq)

    hp, hs = x_prompt, x_sample
    outs_p, outs_s = [], []
    zero_conv = jnp.zeros((bp, CONV_W - 1, CONV_DIM), F32)
    zero_pool = jnp.zeros((bp, POOL_MAX - 1, POOL_DIM), F32)
    for l in range(DEPTH):
        lw = _prep_layer(l, w_ffn1_gu, w_ffn1_down, norm_ffn1, norm_mix, w_in, q_norm, w_uq, kv_norm,
                         w_ukv, conv_w, pool_w, pool_scale, w_out, norm_ffn2, w_ffn2_gu, w_ffn2_down)
        nf = norm_final if l == DEPTH - 1 else None
        hp, *op = _trunk_layer(hp, mods[l, :bp], lw, tables_p, zero_conv, zero_pool, None, nf,
                               bt=1, tt=tm_p, tm=tm_p, tq=512, group_rows=512, pos0=0)
        hs, *os_ = _trunk_layer(hs, mods[l, bp:bp + bs], lw, tables_s, state_conv[l], state_pool[l],
                                _past_keys(cache_ckv[l], cache_krope[l]), nf,
                                bt=bs, tt=dec_seq, tm=dec_seq, tq=dec_seq,
                                group_rows=MLA_HEADS * dec_seq, pos0=past_len)
        outs_p.append(op)
        outs_s.append(os_)

    stack = lambda outs, k: jnp.stack([o[k] for o in outs])
    return (hp, hs,
            stack(outs_p, 0), stack(outs_p, 1), stack(outs_p, 2), stack(outs_p, 3),
            stack(outs_s, 0), stack(outs_s, 1), stack(outs_s, 2), stack(outs_s, 3))
```

```python
import functools

import jax
import jax.numpy as jnp
import numpy as np
from jax import lax
from jax.experimental import pallas as pl
from jax.experimental.pallas import tpu as pltpu

F32 = jnp.float32
BF16 = jnp.bfloat16

D_MODEL = 1024
DEPTH = 2
CHUNK = 64
EPS = 1e-6
N_ADA = 9
D_FF = 2816
MLA_HEADS = 4
Q_RANK = 256
KV_RANK = 128
NOPE_DIM = 128
ROPE_DIM = 64
V_DIM = 128
ROPE_BASE = 10000.0
ATTN_SCALE = (NOPE_DIM + ROPE_DIM) ** -0.5
CONV_DIM = 256
CONV_W = 3
POOL_WINDOWS = (2, 4, 8, 16)
POOL_GROUPS = 4
POOL_DIM = 256
POOL_GROUP_DIM = POOL_DIM // POOL_GROUPS
POOL_MAX = 16

LANES = 128
KEY_DIM = 2 * LANES
FF_CHUNK = 256
ADA_ROWS = 16
ADA_COLS = 1024
HALO = 16
NEG_BIG = -1e30
VMEM_LIMIT = 56 * 1024 * 1024

IN_CQ, IN_CKV, IN_KR, IN_KRS, IN_CB, IN_CC, IN_CV, IN_PU, IN_END = (
    0, 256, 384, 512, 640, 896, 1152, 1408, 1664)
UQ_NOPE, UQ_ROPE, UQ_ROPES, UQ_END = 0, 512, 1024, 1536


def _rms(x, g):
    return x * lax.rsqrt(jnp.mean(x * x, axis=-1, keepdims=True) + EPS) * g


def _silu(x):
    return x * jax.nn.sigmoid(x)


def _dot(a, b):
    return jnp.dot(a, b, preferred_element_type=F32)


def _const_spec(shape):
    nd = len(shape)
    return pl.BlockSpec(shape, lambda *_: (0,) * nd, pipeline_mode=pl.Buffered(1))


def _ada_kernel(c_ref, w_ref, b_ref, o_ref):
    a = _silu(c_ref[...]).astype(BF16)
    o_ref[0] = _dot(a, w_ref[0].astype(BF16)) + b_ref[0]


def _ada(c_all, w_ada, b_ada):
    n_l, _, n_cols = w_ada.shape
    return pl.pallas_call(
        _ada_kernel,
        grid=(n_l, n_cols // ADA_COLS),
        in_specs=[
            pl.BlockSpec((ADA_ROWS, D_MODEL), lambda l, j: (0, 0)),
            pl.BlockSpec((1, D_MODEL, ADA_COLS), lambda l, j: (l, 0, j)),
            pl.BlockSpec((1, 1, ADA_COLS), lambda l, j: (l, 0, j)),
        ],
        out_specs=pl.BlockSpec((1, ADA_ROWS, ADA_COLS), lambda l, j: (l, 0, j)),
        out_shape=jax.ShapeDtypeStruct((n_l, ADA_ROWS, n_cols), F32),
        compiler_params=pltpu.CompilerParams(
            dimension_semantics=("arbitrary", "arbitrary"), vmem_limit_bytes=VMEM_LIMIT),
        name="ada",
    )(c_all, w_ada, b_ada.reshape(n_l, 1, n_cols))


def _rope_table_kernel(inv_ref, sign_ref, cos_ref, sin_ref, *, pos0, tm):
    row = lax.broadcasted_iota(jnp.int32, (tm, LANES), 0)
    pos = (pos0 + pl.program_id(0) * tm + row).astype(F32)
    ang = pos * inv_ref[...]
    cos_ref[...] = jnp.cos(ang)
    sin_ref[...] = jnp.sin(ang) * sign_ref[...]


def _rope_tables(n_rows, pos0, tm):
    half = ROPE_DIM // 2
    inv = ROPE_BASE ** (-jnp.arange(half, dtype=F32) / half)
    inv = jnp.tile(inv, LANES // half).reshape(1, LANES)
    sign = jnp.tile(jnp.concatenate([-jnp.ones((half,), F32), jnp.ones((half,), F32)]),
                    LANES // ROPE_DIM).reshape(1, LANES)
    return pl.pallas_call(
        functools.partial(_rope_table_kernel, pos0=pos0, tm=tm),
        grid=(n_rows // tm,),
        in_specs=[_const_spec((1, LANES)), _const_spec((1, LANES))],
        out_specs=[pl.BlockSpec((tm, LANES), lambda i: (i, 0))] * 2,
        out_shape=[jax.ShapeDtypeStruct((n_rows, LANES), F32)] * 2,
        compiler_params=pltpu.CompilerParams(dimension_semantics=("arbitrary",)),
        name="rope_tables",
    )(inv, sign)


def _ffn_kernel(*refs, has_mix, final_norm):
    refs = list(refs)
    h_ref = refs.pop(0)
    if has_mix:
        a_ref, sp_ref, wo_ref = refs.pop(0), refs.pop(0), refs.pop(0)
    mod_ref, ng_ref, wgu_ref, wd_ref = refs.pop(0), refs.pop(0), refs.pop(0), refs.pop(0)
    if final_norm:
        nf_ref = refs.pop(0)
    (o_ref,) = refs

    bt, tt, d = h_ref.shape
    m = bt * tt
    x = h_ref[...]
    mod = mod_ref[...]
    row0 = 0
    if has_mix:
        mix_w = a_ref.shape[-1]
        mix = _dot(a_ref[...].reshape(m, mix_w), wo_ref[0:mix_w, :])
        mix = mix + _dot(sp_ref[...].reshape(m, sp_ref.shape[-1]), wo_ref[mix_w:, :])
        x = x + mod[:, 5:6, :] * mix.reshape(bt, tt, d)
        row0 = 6
    shift, scale, gate = mod[:, row0:row0 + 1, :], mod[:, row0 + 1:row0 + 2, :], mod[:, row0 + 2:row0 + 3, :]
    n = _rms(x, ng_ref[...]) * (1.0 + scale) + shift
    nb = n.reshape(m, d).astype(BF16)
    acc = jnp.zeros((m, d), F32)
    for c in range(D_FF // FF_CHUNK):
        g = _dot(nb, wgu_ref[:, c * FF_CHUNK:(c + 1) * FF_CHUNK])
        u = _dot(nb, wgu_ref[:, D_FF + c * FF_CHUNK:D_FF + (c + 1) * FF_CHUNK])
        act = (_silu(g) * u).astype(BF16)
        acc = acc + _dot(act, wd_ref[c * FF_CHUNK:(c + 1) * FF_CHUNK, :])
    y = x + 0.5 * gate * acc.reshape(bt, tt, d)
    if final_norm:
        y = _rms(y, nf_ref[...])
    o_ref[...] = y


def _ffn(h, mod, norm_g, w_gu, w_down, *, bt, tt, mix=None, norm_final=None):
    n_b, n_t, d = h.shape
    grid = (n_b // bt, n_t // tt)
    tile = lambda w: pl.BlockSpec((bt, tt, w), lambda b, t: (b, t, 0))
    args, specs = [h], [tile(d)]
    if mix is not None:
        a, sp, w_out = mix
        args += [a, sp, w_out]
        specs += [tile(a.shape[-1]), tile(sp.shape[-1]), _const_spec(w_out.shape)]
    args += [mod, norm_g.reshape(1, d), w_gu, w_down]
    specs += [pl.BlockSpec((bt, N_ADA, d), lambda b, t: (b, 0, 0)), _const_spec((1, d)),
              _const_spec(w_gu.shape), _const_spec(w_down.shape)]
    if norm_final is not None:
        args.append(norm_final.reshape(1, d))
        specs.append(_const_spec((1, d)))
    return pl.pallas_call(
        functools.partial(_ffn_kernel, has_mix=mix is not None, final_norm=norm_final is not None),
        grid=grid,
        in_specs=specs,
        out_specs=tile(d),
        out_shape=jax.ShapeDtypeStruct(h.shape, F32),
        compiler_params=pltpu.CompilerParams(
            dimension_semantics=("arbitrary", "arbitrary"), vmem_limit_bytes=VMEM_LIMIT),
        name="ffn_mix" if mix is not None else "ffn",
    )(*args)


def _mix_in_kernel(h_ref, mod_ref, ng_ref, win_ref, qn_ref, wuq_ref, kvn_ref, wukt_ref, convw_ref,
                   poolw_ref, pools_ref, cos_ref, sin_ref, cstate_ref, pstate_ref,
                   q_ref, k_ref, ckv_ref, kr_ref, sp_ref, cnew_ref, pnew_ref,
                   eu_ref, ep_ref, s2_ref, s4_ref, s8_ref, *, tm, pos0):
    t = pl.program_id(1)
    x = h_ref[0]
    mod = mod_ref[0]
    n = _rms(x, ng_ref[...]) * (1.0 + mod[4:5, :]) + mod[3:4, :]
    proj = _dot(n.astype(BF16), win_ref[...])
    cos = cos_ref[...]
    sin = sin_ref[...]

    qn = _rms(proj[:, IN_CQ:IN_CKV], qn_ref[...]).astype(BF16)
    qall = _dot(qn, wuq_ref[...])
    for hd in range(MLA_HEADS):
        q_nope = qall[:, UQ_NOPE + hd * NOPE_DIM:UQ_NOPE + (hd + 1) * NOPE_DIM].astype(BF16)
        q_lat = _dot(q_nope, wukt_ref[hd])
        q_rope = (qall[:, UQ_ROPE + hd * LANES:UQ_ROPE + (hd + 1) * LANES] * cos
                  + qall[:, UQ_ROPES + hd * LANES:UQ_ROPES + (hd + 1) * LANES] * sin)
        q_ref[0, hd] = (jnp.concatenate([q_lat, q_rope], axis=-1) * ATTN_SCALE).astype(BF16)

    ckv = _rms(proj[:, IN_CKV:IN_KR], kvn_ref[...])
    kr = proj[:, IN_KR:IN_KRS] * cos + proj[:, IN_KRS:IN_CB] * sin
    ckv_ref[0] = ckv
    kr_ref[0] = kr[:, :ROPE_DIM]
    lane = lax.broadcasted_iota(jnp.int32, (tm, LANES), 1)
    k_ref[0] = jnp.concatenate([ckv, jnp.where(lane < ROPE_DIM, kr, 1.0)], axis=-1).astype(BF16)

    @pl.when(t == 0)
    def _():
        eu_ref[0:HALO - (CONV_W - 1), :] = jnp.zeros((HALO - (CONV_W - 1), CONV_DIM), F32)
        eu_ref[HALO - (CONV_W - 1):HALO, :] = cstate_ref[0]
        ep_ref[0:1, :] = jnp.zeros((1, POOL_DIM), F32)
        ep_ref[1:HALO, :] = pstate_ref[0]

    u = proj[:, IN_CC:IN_CV] * proj[:, IN_CV:IN_PU]
    eu_ref[HALO:HALO + tm, :] = u
    cw = convw_ref[...]
    y = u * cw[CONV_W - 1:CONV_W, :]
    for j in range(CONV_W - 1):
        back = CONV_W - 1 - j
        y = y + eu_ref[HALO - back:HALO - back + tm, :] * cw[j:j + 1, :]
    s_out = proj[:, IN_CB:IN_CC] * y
    cnew_ref[0] = eu_ref[HALO + tm - (CONV_W - 1):HALO + tm, :]
    eu_ref[0:HALO, :] = eu_ref[tm:tm + HALO, :]

    pu = proj[:, IN_PU:IN_END]
    ep_ref[HALO:HALO + tm, :] = pu
    n_ext = tm + HALO
    s2_ref[1:n_ext, :] = ep_ref[1:n_ext, :] + ep_ref[0:n_ext - 1, :]
    s4_ref[3:n_ext, :] = s2_ref[3:n_ext, :] + s2_ref[1:n_ext - 2, :]
    s8_ref[7:n_ext, :] = s4_ref[7:n_ext, :] + s4_ref[3:n_ext - 4, :]
    s16 = s8_ref[HALO:n_ext, :] + s8_ref[HALO - 8:n_ext - 8, :]
    lane_grp = lax.broadcasted_iota(jnp.int32, (tm, POOL_DIM), 1) >> (POOL_GROUP_DIM.bit_length() - 1)
    wsum = jnp.where(lane_grp == 0, s2_ref[HALO:n_ext, :],
                     jnp.where(lane_grp == 1, s4_ref[HALO:n_ext, :],
                               jnp.where(lane_grp == 2, s8_ref[HALO:n_ext, :], s16)))
    win = jnp.left_shift(2, lane_grp)
    pos = pos0 + t * tm + lax.broadcasted_iota(jnp.int32, (tm, POOL_DIM), 0)
    cnt = jnp.minimum(win, pos + 1).astype(F32)
    dlt = (wsum / cnt - pu).astype(BF16)
    p_out = _dot(dlt, poolw_ref[...]) * pools_ref[...]
    pnew_ref[0] = ep_ref[tm + 1:tm + HALO, :]
    ep_ref[0:HALO, :] = ep_ref[tm:tm + HALO, :]

    sp_ref[0] = jnp.concatenate([s_out, p_out], axis=-1).astype(BF16)


def _mix_in(h, mod, lw, cos_t, sin_t, conv_state, pool_state, *, tm, pos0):
    n_b, n_t, d = h.shape
    grid = (n_b, n_t // tm)
    rows = lambda w: pl.BlockSpec((1, tm, w), lambda b, t: (b, t, 0))
    per_b = lambda r, w: pl.BlockSpec((1, r, w), lambda b, t: (b, 0, 0))
    in_specs = [
        rows(d), per_b(N_ADA, d), _const_spec((1, d)), _const_spec(lw["w_in"].shape),
        _const_spec((1, Q_RANK)), _const_spec(lw["w_uq"].shape), _const_spec((1, KV_RANK)),
        _const_spec(lw["w_ukt"].shape), _const_spec((CONV_W, CONV_DIM)),
        _const_spec((POOL_DIM, POOL_DIM)), _const_spec((1, POOL_DIM)),
        pl.BlockSpec((tm, LANES), lambda b, t: (t, 0)), pl.BlockSpec((tm, LANES), lambda b, t: (t, 0)),
        per_b(CONV_W - 1, CONV_DIM), per_b(POOL_MAX - 1, POOL_DIM),
    ]
    out_specs = [
        pl.BlockSpec((1, MLA_HEADS, tm, KEY_DIM), lambda b, t: (b, 0, t, 0)),
        rows(KEY_DIM), rows(KV_RANK), rows(ROPE_DIM), rows(CONV_DIM + POOL_DIM),
        per_b(CONV_W - 1, CONV_DIM), per_b(POOL_MAX - 1, POOL_DIM),
    ]
    out_shape = [
        jax.ShapeDtypeStruct((n_b, MLA_HEADS, n_t, KEY_DIM), BF16),
        jax.ShapeDtypeStruct((n_b, n_t, KEY_DIM), BF16),
        jax.ShapeDtypeStruct((n_b, n_t, KV_RANK), F32),
        jax.ShapeDtypeStruct((n_b, n_t, ROPE_DIM), F32),
        jax.ShapeDtypeStruct((n_b, n_t, CONV_DIM + POOL_DIM), BF16),
        jax.ShapeDtypeStruct((n_b, CONV_W - 1, CONV_DIM), F32),
        jax.ShapeDtypeStruct((n_b, POOL_MAX - 1, POOL_DIM), F32),
    ]
    ext = pltpu.VMEM((tm + HALO, POOL_DIM), F32)
    return pl.pallas_call(
        functools.partial(_mix_in_kernel, tm=tm, pos0=pos0),
        grid=grid,
        in_specs=in_specs,
        out_specs=out_specs,
        out_shape=out_shape,
        scratch_shapes=[ext, ext, ext, ext, ext],
        compiler_params=pltpu.CompilerParams(
            dimension_semantics=("arbitrary", "arbitrary"), vmem_limit_bytes=VMEM_LIMIT),
        name="mix_in",
    )(h, mod, lw["norm_mix"], lw["w_in"], lw["q_norm"], lw["w_uq"], lw["kv_norm"], lw["w_ukt"],
      lw["conv_w"], lw["pool_bd"], lw["pool_scale"], cos_t, sin_t, conv_state, pool_state)


def _attn_kernel(*refs, tq, tk_past, past_len, group_rows):
    refs = list(refs)
    q_ref, knew_ref = refs.pop(0), refs.pop(0)
    kpast_ref = refs.pop(0) if past_len else None
    wuv_ref, o_ref, m_ref, acc_ref = refs[:4]
    s_ref = None if past_len else refs[4]

    i = pl.program_id(1)
    m_rows = MLA_HEADS * tq
    n_groups = m_rows // group_rows
    group_heads = group_rows // tq
    m_ref[...] = jnp.full(m_ref.shape, NEG_BIG, F32)
    acc_ref[...] = jnp.zeros(acc_ref.shape, F32)

    def lane_tile(x, width):
        return x[:, :width] if width < LANES else jnp.concatenate([x] * (width // LANES), axis=1)

    def group(g):
        return slice(g * group_rows, (g + 1) * group_rows)

    def scores(g, kblk):
        q = q_ref[0, g * group_heads:(g + 1) * group_heads].reshape(group_rows, KEY_DIM)
        return lax.dot_general(q, kblk, (((1,), (1,)), ((), ())), preferred_element_type=F32)

    def absorb(g, s, kblk, mask):
        width = kblk.shape[0]
        rows = group(g)
        if mask is not None:
            s = jnp.where(mask, s, NEG_BIG)
        m_prev = m_ref[rows, :]
        m_new = jnp.maximum(m_prev, jnp.max(s, axis=1, keepdims=True))
        alpha = jnp.exp(m_prev - m_new)
        p = jnp.exp(s - lane_tile(m_new, width)).astype(BF16)
        acc_ref[rows, :] = lane_tile(alpha, KEY_DIM) * acc_ref[rows, :] + _dot(p, kblk)
        m_ref[rows, :] = m_new

    row = lax.broadcasted_iota(jnp.int32, (group_rows, tq), 0) & (tq - 1)
    col = lax.broadcasted_iota(jnp.int32, (group_rows, tq), 1)
    chunk_shift = CHUNK.bit_length() - 1
    mask = (col >> chunk_shift) <= (row >> chunk_shift)

    def new_block(j):
        return knew_ref[0, pl.ds(pl.multiple_of(j * tq, tq), tq), :]

    if past_len:
        def past_body(j, carry):
            kblk = kpast_ref[0, pl.ds(pl.multiple_of(j * tk_past, tk_past), tk_past), :]
            for g in range(n_groups):
                absorb(g, scores(g, kblk), kblk, None)
            return carry
        lax.fori_loop(0, past_len // tk_past, past_body, 0)
        kblk = new_block(i)
        for g in range(n_groups):
            absorb(g, scores(g, kblk), kblk, mask)
    else:
        first = new_block(0)
        for g in range(n_groups):
            s_ref[group(g), :] = scores(g, first)

        def body(j, carry):
            kblk, knext = new_block(j), new_block(j + 1)
            for g in range(n_groups):
                s = s_ref[group(g), :]
                s_ref[group(g), :] = scores(g, knext)
                absorb(g, s, kblk, None)
            return carry
        lax.fori_loop(0, i, body, 0)
        kblk = new_block(i)
        for g in range(n_groups):
            absorb(g, s_ref[group(g), :], kblk, mask)

    heads = []
    for hd in range(MLA_HEADS):
        acc = acc_ref[hd * tq:(hd + 1) * tq, :]
        denom = acc[:, KV_RANK + ROPE_DIM:KV_RANK + ROPE_DIM + 1]
        heads.append(_dot((acc[:, :KV_RANK] / denom).astype(BF16), wuv_ref[hd]))
    o_ref[0] = jnp.concatenate(heads, axis=-1).astype(BF16)


def _attn(q, k_new, k_past, w_uv, *, tq, tk_past, group_rows):
    n_b, _, n_t, _ = q.shape
    past_len = 0 if k_past is None else k_past.shape[1]
    assert (past_len % CHUNK == 0) and (tq % CHUNK == 0 or n_t == tq <= CHUNK)
    assert past_len == 0 or (n_t == tq and past_len % tk_past == 0)
    m_rows = MLA_HEADS * tq
    assert m_rows % group_rows == 0 and group_rows % tq == 0
    scratch = [pltpu.VMEM((m_rows, LANES), F32), pltpu.VMEM((m_rows, KEY_DIM), F32)]
    if not past_len:
        scratch.append(pltpu.VMEM((m_rows, tq), F32))
    args = [q, k_new]
    specs = [pl.BlockSpec((1, MLA_HEADS, tq, KEY_DIM), lambda b, i: (b, 0, i, 0)),
             pl.BlockSpec((1, n_t, KEY_DIM), lambda b, i: (b, 0, 0), pipeline_mode=pl.Buffered(1))]
    if past_len:
        args.append(k_past)
        specs.append(pl.BlockSpec((1, past_len, KEY_DIM), lambda b, i: (b, 0, 0)))
    args.append(w_uv)
    specs.append(_const_spec(w_uv.shape))
    return pl.pallas_call(
        functools.partial(_attn_kernel, tq=tq, tk_past=tk_past, past_len=past_len,
                          group_rows=group_rows),
        grid=(n_b, n_t // tq),
        in_specs=specs,
        out_specs=pl.BlockSpec((1, tq, MLA_HEADS * V_DIM), lambda b, i: (b, i, 0)),
        out_shape=jax.ShapeDtypeStruct((n_b, n_t, MLA_HEADS * V_DIM), BF16),
        scratch_shapes=scratch,
        compiler_params=pltpu.CompilerParams(
            dimension_semantics=("arbitrary", "arbitrary"), vmem_limit_bytes=VMEM_LIMIT),
        name="attn",
    )(*args)


def _prep_layer(l, w_ffn1_gu, w_ffn1_down, norm_ffn1, norm_mix, w_in, q_norm, w_uq, kv_norm, w_ukv,
                conv_w, pool_w, pool_scale, w_out, norm_ffn2, w_ffn2_gu, w_ffn2_down):
    half = ROPE_DIM // 2
    swap = lambda w: jnp.concatenate([w[:, half:], w[:, :half]], axis=1)
    pad = lambda w: jnp.pad(w, ((0, 0), (0, LANES - w.shape[1])))
    wi = w_in[l]
    cq, ckv, kr, rest = (wi[:, :Q_RANK], wi[:, Q_RANK:Q_RANK + KV_RANK],
                         wi[:, Q_RANK + KV_RANK:Q_RANK + KV_RANK + ROPE_DIM],
                         wi[:, Q_RANK + KV_RANK + ROPE_DIM:])
    w_in_p = jnp.concatenate([cq, ckv, pad(kr), pad(swap(kr)), rest], axis=1).astype(BF16)

    wq = w_uq[l].reshape(Q_RANK, MLA_HEADS, NOPE_DIM + ROPE_DIM)
    nope = wq[:, :, :NOPE_DIM].reshape(Q_RANK, MLA_HEADS * NOPE_DIM)
    rope_cols = [wq[:, hd, NOPE_DIM:] for hd in range(MLA_HEADS)]
    w_uq_p = jnp.concatenate([nope] + [pad(r) for r in rope_cols] + [pad(swap(r)) for r in rope_cols],
                             axis=1).astype(BF16)

    wkv = w_ukv[l].reshape(KV_RANK, MLA_HEADS, NOPE_DIM + V_DIM)
    w_ukt = jnp.transpose(wkv[:, :, :NOPE_DIM], (1, 2, 0)).astype(BF16)
    w_uv = jnp.transpose(wkv[:, :, NOPE_DIM:], (1, 0, 2)).astype(BF16)

    pool_bd = jnp.zeros((POOL_GROUPS, POOL_GROUP_DIM, POOL_GROUPS, POOL_GROUP_DIM), F32)
    for g in range(POOL_GROUPS):
        pool_bd = pool_bd.at[g, :, g, :].set(pool_w[l, g])
    pool_bd = pool_bd.reshape(POOL_DIM, POOL_DIM).astype(BF16)

    return {
        "norm_ffn1": norm_ffn1[l], "w_ffn1_gu": w_ffn1_gu[l].astype(BF16),
        "w_ffn1_down": w_ffn1_down[l].astype(BF16),
        "norm_mix": norm_mix[l].reshape(1, D_MODEL), "w_in": w_in_p,
        "q_norm": q_norm[l].reshape(1, Q_RANK), "w_uq": w_uq_p,
        "kv_norm": kv_norm[l].reshape(1, KV_RANK), "w_ukt": w_ukt, "w_uv": w_uv,
        "conv_w": conv_w[l], "pool_bd": pool_bd, "pool_scale": pool_scale[l].reshape(1, POOL_DIM),
        "w_out": w_out[l].astype(BF16),
        "norm_ffn2": norm_ffn2[l], "w_ffn2_gu": w_ffn2_gu[l].astype(BF16),
        "w_ffn2_down": w_ffn2_down[l].astype(BF16),
    }


def _past_keys(cache_ckv_l, cache_krope_l):
    pad = jnp.ones(cache_ckv_l.shape[:-1] + (KEY_DIM - KV_RANK - ROPE_DIM,), BF16)
    return jnp.concatenate([cache_ckv_l.astype(BF16), cache_krope_l.astype(BF16), pad], axis=-1)


def _trunk_layer(h, mod, lw, tables, conv_state, pool_state, k_past, norm_final, *, bt, tt, tm, tq,
                 group_rows, pos0):
    h1 = _ffn(h, mod, lw["norm_ffn1"], lw["w_ffn1_gu"], lw["w_ffn1_down"], bt=bt, tt=tt)
    q, k_new, ckv, kr, sp, conv_new, pool_new = _mix_in(
        h1, mod, lw, tables[0], tables[1], conv_state, pool_state, tm=tm, pos0=pos0)
    a = _attn(q, k_new, k_past, lw["w_uv"], tq=tq, tk_past=2048, group_rows=group_rows)
    h3 = _ffn(h1, mod, lw["norm_ffn2"], lw["w_ffn2_gu"], lw["w_ffn2_down"], bt=bt, tt=tt,
              mix=(a, sp, lw["w_out"]), norm_final=norm_final)
    return h3, ckv, kr, conv_new, pool_new


def kernel(x_prompt, x_sample, c_prompt, c_sample, cache_ckv, cache_krope, state_conv, state_pool, w_ada, b_ada, norm_ffn1, w_ffn1_gu, w_ffn1_down, norm_mix, w_in, q_norm, w_uq, kv_norm, w_ukv, conv_w, pool_w, pool_scale, w_out, norm_ffn2, w_ffn2_gu, w_ffn2_down, norm_final):
    bp, seq, _ = x_prompt.shape
    bs, dec_seq, _ = x_sample.shape
    past_len = cache_ckv.shape[2]
    assert bp + bs <= ADA_ROWS

    c_all = jnp.concatenate(
        [c_prompt, c_sample, jnp.zeros((ADA_ROWS - bp - bs, D_MODEL), F32)], axis=0)
    mods = _ada(c_all, w_ada, b_ada).reshape(DEPTH, ADA_ROWS, N_ADA, D_MODEL)

    tm_p = 512
    tables_p = _rope_tables(seq, 0, tm_p)
    tables_s = _rope_tables(dec_seq, past_len, dec_seq)

    hp, hs = x_prompt, x_sample
    outs_p, outs_s = [], []
    zero_conv = jnp.zeros((bp, CONV_W - 1, CONV_DIM), F32)
    zero_pool = jnp.zeros((bp, POOL_MAX - 1, POOL_DIM), F32)
    for l in range(DEPTH):
        lw = _prep_layer(l, w_ffn1_gu, w_ffn1_down, norm_ffn1, norm_mix, w_in, q_norm, w_uq, kv_norm,
                         w_ukv, conv_w, pool_w, pool_scale, w_out, norm_ffn2, w_ffn2_gu, w_ffn2_down)
        nf = norm_final if l == DEPTH - 1 else None
        hp, *op = _trunk_layer(hp, mods[l, :bp], lw, tables_p, zero_conv, zero_pool, None, nf,
                               bt=1, tt=tm_p, tm=tm_p, tq=512, group_rows=512, pos0=0)
        hs, *os_ = _trunk_layer(hs, mods[l, bp:bp + bs], lw, tables_s, state_conv[l], state_pool[l],
                                _past_keys(cache_ckv[l], cache_krope[l]), nf,
                                bt=bs, tt=dec_seq, tm=dec_seq, tq=dec_seq,
                                group_rows=MLA_HEADS * dec_seq, pos0=past_len)
        outs_p.append(op)
        outs_s.append(os_)

    stack = lambda outs, k: jnp.stack([o[k] for o in outs])
    return (hp, hs,
            stack(outs_p, 0), stack(outs_p, 1), stack(outs_p, 2), stack(outs_p, 3),
            stack(outs_s, 0), stack(outs_s, 1), stack(outs_s, 2), stack(outs_s, 3))
```

```python
import functools

import jax
import jax.numpy as jnp
import numpy as np
from jax import lax
from jax.experimental import pallas as pl
from jax.experimental.pallas import tpu as pltpu

F32 = jnp.float32
BF16 = jnp.bfloat16

D_MODEL = 1024
DEPTH = 2
CHUNK = 64
EPS = 1e-6
N_ADA = 9
D_FF = 2816
MLA_HEADS = 4
Q_RANK = 256
KV_RANK = 128
NOPE_DIM = 128
ROPE_DIM = 64
V_DIM = 128
ROPE_BASE = 10000.0
ATTN_SCALE = (NOPE_DIM + ROPE_DIM) ** -0.5
CONV_DIM = 256
CONV_W = 3
POOL_WINDOWS = (2, 4, 8, 16)
POOL_GROUPS = 4
POOL_DIM = 256
POOL_GROUP_DIM = POOL_DIM // POOL_GROUPS
POOL_MAX = 16

LANES = 128
KEY_DIM = 2 * LANES
FF_CHUNK = 256
ADA_ROWS = 16
ADA_COLS = 1024
HALO = 16
NEG_BIG = -1e30
VMEM_LIMIT = 56 * 1024 * 1024

IN_CQ, IN_CKV, IN_KR, IN_CB, IN_CC, IN_CV, IN_PU, IN_END = 0, 256, 384, 512, 768, 1024, 1280, 1536
UQ_NOPE, UQ_ROPE, UQ_END = 0, 512, 1024


def _rms(x, g):
    return x * lax.rsqrt(jnp.mean(x * x, axis=-1, keepdims=True) + EPS) * g


def _silu(x):
    return x * jax.nn.sigmoid(x)


def _dot(a, b):
    return jnp.dot(a, b, preferred_element_type=F32)


def _const_spec(shape):
    nd = len(shape)
    return pl.BlockSpec(shape, lambda *_: (0,) * nd, pipeline_mode=pl.Buffered(1))


def _layer_spec(stacked, layer):
    rest = stacked.shape[1:]
    return pl.BlockSpec((None,) + rest, lambda *_: (layer,) + (0,) * len(rest),
                        pipeline_mode=pl.Buffered(1))


def _ada_kernel(c_ref, w_ref, b_ref, o_ref):
    a = _silu(c_ref[...]).astype(BF16)
    o_ref[0] = _dot(a, w_ref[0].astype(BF16)) + b_ref[0]


def _ada(c_all, w_ada, b_ada):
    n_l, _, n_cols = w_ada.shape
    return pl.pallas_call(
        _ada_kernel,
        grid=(n_l, n_cols // ADA_COLS),
        in_specs=[
            pl.BlockSpec((ADA_ROWS, D_MODEL), lambda l, j: (0, 0)),
            pl.BlockSpec((1, D_MODEL, ADA_COLS), lambda l, j: (l, 0, j)),
            pl.BlockSpec((1, 1, ADA_COLS), lambda l, j: (l, 0, j)),
        ],
        out_specs=pl.BlockSpec((1, ADA_ROWS, ADA_COLS), lambda l, j: (l, 0, j)),
        out_shape=jax.ShapeDtypeStruct((n_l, ADA_ROWS, n_cols), F32),
        compiler_params=pltpu.CompilerParams(
            dimension_semantics=("arbitrary", "arbitrary"), vmem_limit_bytes=VMEM_LIMIT),
        name="ada",
    )(c_all, w_ada, b_ada.reshape(n_l, 1, n_cols))


def _rope_table_kernel(inv_ref, sign_ref, cos_ref, sin_ref, *, pos0, tm):
    row = lax.broadcasted_iota(jnp.int32, (tm, LANES), 0)
    pos = (pos0 + pl.program_id(0) * tm + row).astype(F32)
    ang = pos * inv_ref[...]
    cos_ref[...] = jnp.cos(ang)
    sin_ref[...] = jnp.sin(ang) * sign_ref[...]


def _rope_tables(n_rows, pos0, tm):
    half = ROPE_DIM // 2
    inv = ROPE_BASE ** (-jnp.arange(half, dtype=F32) / half)
    inv = jnp.tile(inv, LANES // half).reshape(1, LANES)
    sign = jnp.tile(jnp.concatenate([-jnp.ones((half,), F32), jnp.ones((half,), F32)]),
                    LANES // ROPE_DIM).reshape(1, LANES)
    return pl.pallas_call(
        functools.partial(_rope_table_kernel, pos0=pos0, tm=tm),
        grid=(n_rows // tm,),
        in_specs=[_const_spec((1, LANES)), _const_spec((1, LANES))],
        out_specs=[pl.BlockSpec((tm, LANES), lambda i: (i, 0))] * 2,
        out_shape=[jax.ShapeDtypeStruct((n_rows, LANES), F32)] * 2,
        compiler_params=pltpu.CompilerParams(dimension_semantics=("arbitrary",)),
        name="rope_tables",
    )(inv, sign)


def _ffn_kernel(*refs, has_mix, final_norm):
    refs = list(refs)
    h_ref = refs.pop(0)
    if has_mix:
        a_ref, sp_ref, wo_ref = refs.pop(0), refs.pop(0), refs.pop(0)
    mod_ref, ng_ref, wgu_ref, wd_ref = refs.pop(0), refs.pop(0), refs.pop(0), refs.pop(0)
    if final_norm:
        nf_ref = refs.pop(0)
    (o_ref,) = refs

    bt, tt, d = h_ref.shape
    m = bt * tt
    x = h_ref[...]
    mod = mod_ref[...]
    row0 = 0
    if has_mix:
        mix_w = a_ref.shape[-1]
        mix = _dot(a_ref[...].reshape(m, mix_w), wo_ref[0:mix_w, :])
        mix = mix + _dot(sp_ref[...].reshape(m, sp_ref.shape[-1]), wo_ref[mix_w:, :])
        x = x + mod[:, 5:6, :] * mix.reshape(bt, tt, d)
        row0 = 6
    shift, scale, gate = mod[:, row0:row0 + 1, :], mod[:, row0 + 1:row0 + 2, :], mod[:, row0 + 2:row0 + 3, :]
    n = _rms(x, ng_ref[...]) * (1.0 + scale) + shift
    nb = n.reshape(m, d).astype(BF16)
    acc = jnp.zeros((m, d), F32)
    for c in range(D_FF // FF_CHUNK):
        g = _dot(nb, wgu_ref[:, c * FF_CHUNK:(c + 1) * FF_CHUNK])
        u = _dot(nb, wgu_ref[:, D_FF + c * FF_CHUNK:D_FF + (c + 1) * FF_CHUNK])
        act = (_silu(g) * u).astype(BF16)
        acc = acc + _dot(act, wd_ref[c * FF_CHUNK:(c + 1) * FF_CHUNK, :])
    y = x + 0.5 * gate * acc.reshape(bt, tt, d)
    if final_norm:
        y = _rms(y, nf_ref[...])
    o_ref[...] = y


def _ffn(h, mod, layer, norm_g, w_gu, w_down, *, bt, tt, mix=None, norm_final=None):
    n_b, n_t, d = h.shape
    grid = (n_b // bt, n_t // tt)
    tile = lambda w: pl.BlockSpec((bt, tt, w), lambda b, t: (b, t, 0))
    args, specs = [h], [tile(d)]
    if mix is not None:
        a, sp, w_out = mix
        args += [a, sp, w_out]
        specs += [tile(a.shape[-1]), tile(sp.shape[-1]), _layer_spec(w_out, layer)]
    args += [mod, norm_g, w_gu, w_down]
    specs += [pl.BlockSpec((bt, N_ADA, d), lambda b, t: (b, 0, 0)), _layer_spec(norm_g, layer),
              _layer_spec(w_gu, layer), _layer_spec(w_down, layer)]
    if norm_final is not None:
        args.append(norm_final.reshape(1, d))
        specs.append(_const_spec((1, d)))
    return pl.pallas_call(
        functools.partial(_ffn_kernel, has_mix=mix is not None, final_norm=norm_final is not None),
        grid=grid,
        in_specs=specs,
        out_specs=tile(d),
        out_shape=jax.ShapeDtypeStruct(h.shape, F32),
        compiler_params=pltpu.CompilerParams(
            dimension_semantics=("arbitrary", "arbitrary"), vmem_limit_bytes=VMEM_LIMIT),
        name="ffn_mix" if mix is not None else "ffn",
    )(*args)


def _mix_in_kernel(h_ref, mod_ref, ng_ref, win_ref, qn_ref, wuq_ref, kvn_ref, wukt_ref, convw_ref,
                   poolw_ref, pools_ref, cos_ref, sin_ref, cstate_ref, pstate_ref,
                   q_ref, k_ref, ckv_ref, kr_ref, sp_ref, cnew_ref, pnew_ref,
                   eu_ref, ep_ref, s2_ref, s4_ref, s8_ref, *, tm, pos0):
    t = pl.program_id(1)
    x = h_ref[0]
    mod = mod_ref[0]
    n = _rms(x, ng_ref[...]) * (1.0 + mod[4:5, :]) + mod[3:4, :]
    proj = _dot(n.astype(BF16), win_ref[...])
    cos = cos_ref[...]
    sin = sin_ref[...]
    lane = lax.broadcasted_iota(jnp.int32, (tm, LANES), 1)

    def rope(packed):
        return packed * cos + pltpu.roll(packed, ROPE_DIM, axis=1) * sin

    qn = _rms(proj[:, IN_CQ:IN_CKV], qn_ref[...]).astype(BF16)
    qall = _dot(qn, wuq_ref[...])
    for hd in range(MLA_HEADS):
        q_nope = qall[:, UQ_NOPE + hd * NOPE_DIM:UQ_NOPE + (hd + 1) * NOPE_DIM].astype(BF16)
        q_lat = _dot(q_nope, wukt_ref[hd])
        q_rope = jnp.where(lane < ROPE_DIM, rope(qall[:, UQ_ROPE + hd * LANES:UQ_ROPE + (hd + 1) * LANES]), 0.0)
        q_ref[0, hd] = (jnp.concatenate([q_lat, q_rope], axis=-1) * ATTN_SCALE).astype(BF16)

    ckv = _rms(proj[:, IN_CKV:IN_KR], kvn_ref[...])
    kr = rope(proj[:, IN_KR:IN_CB])
    ckv_ref[0] = ckv
    kr_ref[0] = kr[:, :ROPE_DIM]
    k_ref[0] = jnp.concatenate([ckv, jnp.where(lane < ROPE_DIM, kr, 1.0)], axis=-1).astype(BF16)

    @pl.when(t == 0)
    def _():
        eu_ref[0:HALO - (CONV_W - 1), :] = jnp.zeros((HALO - (CONV_W - 1), CONV_DIM), F32)
        eu_ref[HALO - (CONV_W - 1):HALO, :] = cstate_ref[0]
        ep_ref[0:1, :] = jnp.zeros((1, POOL_DIM), F32)
        ep_ref[1:HALO, :] = pstate_ref[0]

    u = proj[:, IN_CC:IN_CV] * proj[:, IN_CV:IN_PU]
    eu_ref[HALO:HALO + tm, :] = u
    cw = convw_ref[...]
    y = u * cw[CONV_W - 1:CONV_W, :]
    for j in range(CONV_W - 1):
        back = CONV_W - 1 - j
        y = y + eu_ref[HALO - back:HALO - back + tm, :] * cw[j:j + 1, :]
    sp_ref[0, :, 0:CONV_DIM] = (proj[:, IN_CB:IN_CC] * y).astype(BF16)
    cnew_ref[0] = eu_ref[HALO + tm - (CONV_W - 1):HALO + tm, :]
    eu_ref[0:HALO, :] = eu_ref[tm:tm + HALO, :]

    pu = proj[:, IN_PU:IN_END]
    ep_ref[HALO:HALO + tm, :] = pu
    n_ext = tm + HALO
    s2_ref[1:n_ext, :] = ep_ref[1:n_ext, :] + ep_ref[0:n_ext - 1, :]
    s4_ref[3:n_ext, :] = s2_ref[3:n_ext, :] + s2_ref[1:n_ext - 2, :]
    s8_ref[7:n_ext, :] = s4_ref[7:n_ext, :] + s4_ref[3:n_ext - 4, :]
    s16 = s8_ref[HALO:n_ext, :] + s8_ref[HALO - 8:n_ext - 8, :]
    lane_grp = lax.broadcasted_iota(jnp.int32, (tm, POOL_DIM), 1) >> (POOL_GROUP_DIM.bit_length() - 1)
    wsum = jnp.where(lane_grp == 0, s2_ref[HALO:n_ext, :],
                     jnp.where(lane_grp == 1, s4_ref[HALO:n_ext, :],
                               jnp.where(lane_grp == 2, s8_ref[HALO:n_ext, :], s16)))
    win = jnp.left_shift(2, lane_grp)
    pos = pos0 + t * tm + lax.broadcasted_iota(jnp.int32, (tm, POOL_DIM), 0)
    cnt = jnp.minimum(win, pos + 1).astype(F32)
    dlt = (wsum / cnt - pu).astype(BF16)
    sp_ref[0, :, CONV_DIM:] = (_dot(dlt, poolw_ref[...]) * pools_ref[...]).astype(BF16)
    pnew_ref[0] = ep_ref[tm + 1:tm + HALO, :]
    ep_ref[0:HALO, :] = ep_ref[tm:tm + HALO, :]


def _mix_in(h, mod, layer, lw, cos_t, sin_t, conv_state, pool_state, *, tm, pos0):
    n_b, n_t, d = h.shape
    grid = (n_b, n_t // tm)
    rows = lambda w: pl.BlockSpec((1, tm, w), lambda b, t: (b, t, 0))
    per_b = lambda r, w: pl.BlockSpec((1, r, w), lambda b, t: (b, 0, 0))
    weights = [lw[k] for k in ("norm_mix", "w_in", "q_norm", "w_uq", "kv_norm", "w_ukt", "conv_w",
                               "pool_bd", "pool_scale")]
    in_specs = [rows(d), per_b(N_ADA, d)] + [_layer_spec(w, layer) for w in weights] + [
        pl.BlockSpec((tm, LANES), lambda b, t: (t, 0)), pl.BlockSpec((tm, LANES), lambda b, t: (t, 0)),
        per_b(CONV_W - 1, CONV_DIM), per_b(POOL_MAX - 1, POOL_DIM),
    ]
    out_specs = [
        pl.BlockSpec((1, MLA_HEADS, tm, KEY_DIM), lambda b, t: (b, 0, t, 0)),
        rows(KEY_DIM), rows(KV_RANK), rows(ROPE_DIM), rows(CONV_DIM + POOL_DIM),
        per_b(CONV_W - 1, CONV_DIM), per_b(POOL_MAX - 1, POOL_DIM),
    ]
    out_shape = [
        jax.ShapeDtypeStruct((n_b, MLA_HEADS, n_t, KEY_DIM), BF16),
        jax.ShapeDtypeStruct((n_b, n_t, KEY_DIM), BF16),
        jax.ShapeDtypeStruct((n_b, n_t, KV_RANK), F32),
        jax.ShapeDtypeStruct((n_b, n_t, ROPE_DIM), F32),
        jax.ShapeDtypeStruct((n_b, n_t, CONV_DIM + POOL_DIM), BF16),
        jax.ShapeDtypeStruct((n_b, CONV_W - 1, CONV_DIM), F32),
        jax.ShapeDtypeStruct((n_b, POOL_MAX - 1, POOL_DIM), F32),
    ]
    ext = pltpu.VMEM((tm + HALO, POOL_DIM), F32)
    return pl.pallas_call(
        functools.partial(_mix_in_kernel, tm=tm, pos0=pos0),
        grid=grid,
        in_specs=in_specs,
        out_specs=out_specs,
        out_shape=out_shape,
        scratch_shapes=[ext, ext, ext, ext, ext],
        compiler_params=pltpu.CompilerParams(
            dimension_semantics=("arbitrary", "arbitrary"), vmem_limit_bytes=VMEM_LIMIT),
        name="mix_in",
    )(h, mod, *weights, cos_t, sin_t, conv_state, pool_state)


def _attn_kernel(*refs, tq, tk_past, past_len, group_rows):
    refs = list(refs)
    q_ref, knew_ref = refs.pop(0), refs.pop(0)
    kpast_ref = refs.pop(0) if past_len else None
    wuv_ref, o_ref, m_ref, acc_ref = refs[:4]
    s_ref = None if past_len else refs[4]

    i = pl.program_id(1)
    m_rows = MLA_HEADS * tq
    n_groups = m_rows // group_rows
    group_heads = group_rows // tq
    m_ref[...] = jnp.full(m_ref.shape, NEG_BIG, F32)
    acc_ref[...] = jnp.zeros(acc_ref.shape, F32)

    def lane_tile(x, width):
        return x[:, :width] if width < LANES else jnp.concatenate([x] * (width // LANES), axis=1)

    def group(g):
        return slice(g * group_rows, (g + 1) * group_rows)

    def scores(g, kblk):
        q = q_ref[0, g * group_heads:(g + 1) * group_heads].reshape(group_rows, KEY_DIM)
        return lax.dot_general(q, kblk, (((1,), (1,)), ((), ())), preferred_element_type=F32)

    def absorb(g, s, kblk, mask):
        width = kblk.shape[0]
        rows = group(g)
        if mask is not None:
            s = jnp.where(mask, s, NEG_BIG)
        m_prev = m_ref[rows, :]
        m_new = jnp.maximum(m_prev, jnp.max(s, axis=1, keepdims=True))
        alpha = jnp.exp(m_prev - m_new)
        p = jnp.exp(s - lane_tile(m_new, width)).astype(BF16)
        acc_ref[rows, :] = lane_tile(alpha, KEY_DIM) * acc_ref[rows, :] + _dot(p, kblk)
        m_ref[rows, :] = m_new

    row = lax.broadcasted_iota(jnp.int32, (group_rows, tq), 0) & (tq - 1)
    col = lax.broadcasted_iota(jnp.int32, (group_rows, tq), 1)
    chunk_shift = CHUNK.bit_length() - 1
    mask = (col >> chunk_shift) <= (row >> chunk_shift)

    def new_block(j):
        return knew_ref[0, pl.ds(pl.multiple_of(j * tq, tq), tq), :]

    if past_len:
        def past_body(j, carry):
            kblk = kpast_ref[0, pl.ds(pl.multiple_of(j * tk_past, tk_past), tk_past), :]
            for g in range(n_groups):
                absorb(g, scores(g, kblk), kblk, None)
            return carry
        lax.fori_loop(0, past_len // tk_past, past_body, 0)
        kblk = new_block(i)
        for g in range(n_groups):
            absorb(g, scores(g, kblk), kblk, mask)
    else:
        first = new_block(0)
        for g in range(n_groups):
            s_ref[group(g), :] = scores(g, first)

        def body(j, carry):
            kblk, knext = new_block(j), new_block(j + 1)
            for g in range(n_groups):
                s = s_ref[group(g), :]
                s_ref[group(g), :] = scores(g, knext)
                absorb(g, s, kblk, None)
            return carry
        lax.fori_loop(0, i, body, 0)
        kblk = new_block(i)
        for g in range(n_groups):
            absorb(g, s_ref[group(g), :], kblk, mask)

    heads = []
    for hd in range(MLA_HEADS):
        acc = acc_ref[hd * tq:(hd + 1) * tq, :]
        denom = acc[:, KV_RANK + ROPE_DIM:KV_RANK + ROPE_DIM + 1]
        heads.append(_dot((acc[:, :KV_RANK] / denom).astype(BF16), wuv_ref[hd]))
    o_ref[0] = jnp.concatenate(heads, axis=-1).astype(BF16)


def _attn(q, k_new, k_past, layer, w_uv, *, tq, tk_past, group_rows):
    n_b, _, n_t, _ = q.shape
    past_len = 0 if k_past is None else k_past.shape[2]
    assert (past_len % CHUNK == 0) and (tq % CHUNK == 0 or n_t == tq <= CHUNK)
    assert past_len == 0 or (n_t == tq and past_len % tk_past == 0)
    m_rows = MLA_HEADS * tq
    assert m_rows % group_rows == 0 and group_rows % tq == 0
    scratch = [pltpu.VMEM((m_rows, LANES), F32), pltpu.VMEM((m_rows, KEY_DIM), F32)]
    if not past_len:
        scratch.append(pltpu.VMEM((m_rows, tq), F32))
    args = [q, k_new]
    specs = [pl.BlockSpec((1, MLA_HEADS, tq, KEY_DIM), lambda b, i: (b, 0, i, 0)),
             pl.BlockSpec((1, n_t, KEY_DIM), lambda b, i: (b, 0, 0), pipeline_mode=pl.Buffered(1))]
    if past_len:
        args.append(k_past)
        specs.append(pl.BlockSpec((None, 1, past_len, KEY_DIM), lambda b, i: (layer, b, 0, 0)))
    args.append(w_uv)
    specs.append(_layer_spec(w_uv, layer))
    return pl.pallas_call(
        functools.partial(_attn_kernel, tq=tq, tk_past=tk_past, past_len=past_len,
                          group_rows=group_rows),
        grid=(n_b, n_t // tq),
        in_specs=specs,
        out_specs=pl.BlockSpec((1, tq, MLA_HEADS * V_DIM), lambda b, i: (b, i, 0)),
        out_shape=jax.ShapeDtypeStruct((n_b, n_t, MLA_HEADS * V_DIM), BF16),
        scratch_shapes=scratch,
        compiler_params=pltpu.CompilerParams(
            dimension_semantics=("arbitrary", "arbitrary"), vmem_limit_bytes=VMEM_LIMIT),
        name="attn",
    )(*args)


def _prep_weights(w_ffn1_gu, w_ffn1_down, norm_ffn1, norm_mix, w_in, q_norm, w_uq, kv_norm, w_ukv,
                  conv_w, pool_w, pool_scale, w_out, norm_ffn2, w_ffn2_gu, w_ffn2_down):
    n_l = w_in.shape[0]
    half = ROPE_DIM // 2
    packed = lambda w: jnp.concatenate([w, w[..., half:], w[..., :half]], axis=-1)
    vec = lambda v: v.reshape(n_l, 1, v.shape[-1])
    cq, ckv, kr, rest = (w_in[..., :Q_RANK], w_in[..., Q_RANK:Q_RANK + KV_RANK],
                         w_in[..., Q_RANK + KV_RANK:Q_RANK + KV_RANK + ROPE_DIM],
                         w_in[..., Q_RANK + KV_RANK + ROPE_DIM:])
    w_in_p = jnp.concatenate([cq, ckv, packed(kr), rest], axis=-1).astype(BF16)

    wq = w_uq.reshape(n_l, Q_RANK, MLA_HEADS, NOPE_DIM + ROPE_DIM)
    nope = wq[..., :NOPE_DIM].reshape(n_l, Q_RANK, MLA_HEADS * NOPE_DIM)
    rope_cols = [wq[:, :, hd, NOPE_DIM:] for hd in range(MLA_HEADS)]
    w_uq_p = jnp.concatenate([nope] + [packed(r) for r in rope_cols], axis=-1).astype(BF16)

    wkv = w_ukv.reshape(n_l, KV_RANK, MLA_HEADS, NOPE_DIM + V_DIM)
    w_ukt = jnp.transpose(wkv[..., :NOPE_DIM], (0, 2, 3, 1)).astype(BF16)
    w_uv = jnp.transpose(wkv[..., NOPE_DIM:], (0, 2, 1, 3)).astype(BF16)

    eye = jnp.eye(POOL_GROUPS, dtype=F32)
    pool_bd = (pool_w[:, :, :, None, :] * eye[None, :, None, :, None]).reshape(
        n_l, POOL_DIM, POOL_DIM).astype(BF16)

    return {
        "norm_ffn1": vec(norm_ffn1), "w_ffn1_gu": w_ffn1_gu.astype(BF16),
        "w_ffn1_down": w_ffn1_down.astype(BF16),
        "norm_mix": vec(norm_mix), "w_in": w_in_p, "q_norm": vec(q_norm), "w_uq": w_uq_p,
        "kv_norm": vec(kv_norm), "w_ukt": w_ukt, "w_uv": w_uv,
        "conv_w": conv_w, "pool_bd": pool_bd, "pool_scale": vec(pool_scale),
        "w_out": w_out.astype(BF16),
        "norm_ffn2": vec(norm_ffn2), "w_ffn2_gu": w_ffn2_gu.astype(BF16),
        "w_ffn2_down": w_ffn2_down.astype(BF16),
    }


def _past_keys(cache_ckv, cache_krope):
    pad = jnp.ones(cache_ckv.shape[:-1] + (KEY_DIM - KV_RANK - ROPE_DIM,), BF16)
    return jnp.concatenate([cache_ckv.astype(BF16), cache_krope.astype(BF16), pad], axis=-1)


def _trunk_layer(h, mod, layer, lw, tables, conv_state, pool_state, k_past, norm_final, *, bt, tt, tm,
                 tq, group_rows, pos0):
    h1 = _ffn(h, mod, layer, lw["norm_ffn1"], lw["w_ffn1_gu"], lw["w_ffn1_down"], bt=bt, tt=tt)
    q, k_new, ckv, kr, sp, conv_new, pool_new = _mix_in(
        h1, mod, layer, lw, tables[0], tables[1], conv_state, pool_state, tm=tm, pos0=pos0)
    a = _attn(q, k_new, k_past, layer, lw["w_uv"], tq=tq, tk_past=2048, group_rows=group_rows)
    h3 = _ffn(h1, mod, layer, lw["norm_ffn2"], lw["w_ffn2_gu"], lw["w_ffn2_down"], bt=bt, tt=tt,
              mix=(a, sp, lw["w_out"]), norm_final=norm_final)
    return h3, ckv, kr, conv_new, pool_new


def kernel(x_prompt, x_sample, c_prompt, c_sample, cache_ckv, cache_krope, state_conv, state_pool, w_ada, b_ada, norm_ffn1, w_ffn1_gu, w_ffn1_down, norm_mix, w_in, q_norm, w_uq, kv_norm, w_ukv, conv_w, pool_w, pool_scale, w_out, norm_ffn2, w_ffn2_gu, w_ffn2_down, norm_final):
    bp, seq, _ = x_prompt.shape
    bs, dec_seq, _ = x_sample.shape
    past_len = cache_ckv.shape[2]
    assert bp + bs <= ADA_ROWS

    c_all = jnp.concatenate(
        [c_prompt, c_sample, jnp.zeros((ADA_ROWS - bp - bs, D_MODEL), F32)], axis=0)
    mods = _ada(c_all, w_ada, b_ada).reshape(DEPTH, ADA_ROWS, N_ADA, D_MODEL)

    tm_p = 512
    tables_p = _rope_tables(seq, 0, tm_p)
    tables_s = _rope_tables(dec_seq, past_len, dec_seq)
    lw = _prep_weights(w_ffn1_gu, w_ffn1_down, norm_ffn1, norm_mix, w_in, q_norm, w_uq, kv_norm,
                       w_ukv, conv_w, pool_w, pool_scale, w_out, norm_ffn2, w_ffn2_gu, w_ffn2_down)
    k_past = _past_keys(cache_ckv, cache_krope)

    hp, hs = x_prompt, x_sample
    outs_p, outs_s = [], []
    zero_conv = jnp.zeros((bp, CONV_W - 1, CONV_DIM), F32)
    zero_pool = jnp.zeros((bp, POOL_MAX - 1, POOL_DIM), F32)
    for l in range(DEPTH):
        nf = norm_final if l == DEPTH - 1 else None
        hp, *op = _trunk_layer(hp, mods[l, :bp], l, lw, tables_p, zero_conv, zero_pool, None, nf,
                               bt=1, tt=tm_p, tm=tm_p, tq=512, group_rows=512, pos0=0)
        hs, *os_ = _trunk_layer(hs, mods[l, bp:bp + bs], l, lw, tables_s, state_conv[l], state_pool[l],
                                k_past, nf, bt=bs, tt=dec_seq, tm=dec_seq, tq=dec_seq,
                                group_rows=MLA_HEADS * dec_seq, pos0=past_len)
        outs_p.append(op)
        outs_s.append(os_)

    stack = lambda outs, k: jnp.stack([o[k] for o in outs])
    return (hp, hs,
            stack(outs_p, 0), stack(outs_p, 1), stack(outs_p, 2), stack(outs_p, 3),
            stack(outs_s, 0), stack(outs_s, 1), stack(outs_s, 2), stack(outs_s, 3))
```

```python
import functools

import jax
import jax.numpy as jnp
import numpy as np
from jax import lax
from jax.experimental import pallas as pl
from jax.experimental.pallas import tpu as pltpu

F32 = jnp.float32
BF16 = jnp.bfloat16

D_MODEL = 1024
DEPTH = 2
CHUNK = 64
EPS = 1e-6
N_ADA = 9
D_FF = 2816
MLA_HEADS = 4
Q_RANK = 256
KV_RANK = 128
NOPE_DIM = 128
ROPE_DIM = 64
V_DIM = 128
ROPE_BASE = 10000.0
ATTN_SCALE = (NOPE_DIM + ROPE_DIM) ** -0.5
LOG2_E = 1.4426950408889634
CONV_DIM = 256
CONV_W = 3
POOL_WINDOWS = (2, 4, 8, 16)
POOL_GROUPS = 4
POOL_DIM = 256
POOL_GROUP_DIM = POOL_DIM // POOL_GROUPS
POOL_MAX = 16

LANES = 128
KEY_DIM = 2 * LANES
FF_CHUNK = 256
ADA_ROWS = 16
ADA_COLS = 1024
HALO = 16
NEG_BIG = -1e30
VMEM_LIMIT = 56 * 1024 * 1024

IN_CQ, IN_CKV, IN_KR, IN_CB, IN_CC, IN_CV, IN_PU, IN_END = 0, 256, 384, 512, 768, 1024, 1280, 1536
UQ_NOPE, UQ_ROPE, UQ_END = 0, 512, 1024


def _rms(x, g):
    return x * lax.rsqrt(jnp.mean(x * x, axis=-1, keepdims=True) + EPS) * g


def _silu(x):
    return x * jax.nn.sigmoid(x)


def _dot(a, b):
    return jnp.dot(a, b, preferred_element_type=F32)


def _const_spec(shape):
    nd = len(shape)
    return pl.BlockSpec(shape, lambda *_: (0,) * nd, pipeline_mode=pl.Buffered(1))


def _layer_spec(stacked, layer):
    rest = stacked.shape[1:]
    return pl.BlockSpec((None,) + rest, lambda *_: (layer,) + (0,) * len(rest),
                        pipeline_mode=pl.Buffered(1))


def _ada_kernel(c_ref, w_ref, b_ref, o_ref):
    a = _silu(c_ref[...]).astype(BF16)
    o_ref[0] = _dot(a, w_ref[0].astype(BF16)) + b_ref[0]


def _ada(c_all, w_ada, b_ada):
    n_l, _, n_cols = w_ada.shape
    return pl.pallas_call(
        _ada_kernel,
        grid=(n_l, n_cols // ADA_COLS),
        in_specs=[
            pl.BlockSpec((ADA_ROWS, D_MODEL), lambda l, j: (0, 0)),
            pl.BlockSpec((1, D_MODEL, ADA_COLS), lambda l, j: (l, 0, j)),
            pl.BlockSpec((1, 1, ADA_COLS), lambda l, j: (l, 0, j)),
        ],
        out_specs=pl.BlockSpec((1, ADA_ROWS, ADA_COLS), lambda l, j: (l, 0, j)),
        out_shape=jax.ShapeDtypeStruct((n_l, ADA_ROWS, n_cols), F32),
        compiler_params=pltpu.CompilerParams(
            dimension_semantics=("arbitrary", "arbitrary"), vmem_limit_bytes=VMEM_LIMIT),
        name="ada",
    )(c_all, w_ada, b_ada.reshape(n_l, 1, n_cols))


def _rope_table_kernel(inv_ref, sign_ref, cos_ref, sin_ref, *, pos0, tm):
    row = lax.broadcasted_iota(jnp.int32, (tm, LANES), 0)
    pos = (pos0 + pl.program_id(0) * tm + row).astype(F32)
    ang = pos * inv_ref[...]
    cos_ref[...] = jnp.cos(ang)
    sin_ref[...] = jnp.sin(ang) * sign_ref[...]


def _rope_tables(n_rows, pos0, tm):
    half = ROPE_DIM // 2
    inv = ROPE_BASE ** (-jnp.arange(half, dtype=F32) / half)
    inv = jnp.tile(inv, LANES // half).reshape(1, LANES)
    sign = jnp.tile(jnp.concatenate([-jnp.ones((half,), F32), jnp.ones((half,), F32)]),
                    LANES // ROPE_DIM).reshape(1, LANES)
    return pl.pallas_call(
        functools.partial(_rope_table_kernel, pos0=pos0, tm=tm),
        grid=(n_rows // tm,),
        in_specs=[_const_spec((1, LANES)), _const_spec((1, LANES))],
        out_specs=[pl.BlockSpec((tm, LANES), lambda i: (i, 0))] * 2,
        out_shape=[jax.ShapeDtypeStruct((n_rows, LANES), F32)] * 2,
        compiler_params=pltpu.CompilerParams(dimension_semantics=("arbitrary",)),
        name="rope_tables",
    )(inv, sign)


def _ffn_kernel(*refs, has_mix, final_norm):
    refs = list(refs)
    h_ref = refs.pop(0)
    if has_mix:
        a_ref, sp_ref, wo_ref = refs.pop(0), refs.pop(0), refs.pop(0)
    mod_ref, ng_ref, wgu_ref, wd_ref = refs.pop(0), refs.pop(0), refs.pop(0), refs.pop(0)
    if final_norm:
        nf_ref = refs.pop(0)
    (o_ref,) = refs

    bt, tt, d = h_ref.shape
    m = bt * tt
    x = h_ref[...]
    mod = mod_ref[...]
    row0 = 0
    if has_mix:
        mix_w = a_ref.shape[-1]
        mix = _dot(a_ref[...].reshape(m, mix_w), wo_ref[0:mix_w, :])
        mix = mix + _dot(sp_ref[...].reshape(m, sp_ref.shape[-1]), wo_ref[mix_w:, :])
        x = x + mod[:, 5:6, :] * mix.reshape(bt, tt, d)
        row0 = 6
    shift, scale, gate = mod[:, row0:row0 + 1, :], mod[:, row0 + 1:row0 + 2, :], mod[:, row0 + 2:row0 + 3, :]
    n = _rms(x, ng_ref[...]) * (1.0 + scale) + shift
    nb = n.reshape(m, d).astype(BF16)
    acc = jnp.zeros((m, d), F32)
    for c in range(D_FF // FF_CHUNK):
        g = _dot(nb, wgu_ref[:, c * FF_CHUNK:(c + 1) * FF_CHUNK])
        u = _dot(nb, wgu_ref[:, D_FF + c * FF_CHUNK:D_FF + (c + 1) * FF_CHUNK])
        act = (_silu(g) * u).astype(BF16)
        acc = acc + _dot(act, wd_ref[c * FF_CHUNK:(c + 1) * FF_CHUNK, :])
    y = x + 0.5 * gate * acc.reshape(bt, tt, d)
    if final_norm:
        y = _rms(y, nf_ref[...])
    o_ref[...] = y


def _ffn(h, mod, layer, norm_g, w_gu, w_down, *, bt, tt, mix=None, norm_final=None):
    n_b, n_t, d = h.shape
    grid = (n_b // bt, n_t // tt)
    tile = lambda w: pl.BlockSpec((bt, tt, w), lambda b, t: (b, t, 0))
    args, specs = [h], [tile(d)]
    if mix is not None:
        a, sp, w_out = mix
        args += [a, sp, w_out]
        specs += [tile(a.shape[-1]), tile(sp.shape[-1]), _layer_spec(w_out, layer)]
    args += [mod, norm_g, w_gu, w_down]
    specs += [pl.BlockSpec((bt, N_ADA, d), lambda b, t: (b, 0, 0)), _layer_spec(norm_g, layer),
              _layer_spec(w_gu, layer), _layer_spec(w_down, layer)]
    if norm_final is not None:
        args.append(norm_final.reshape(1, d))
        specs.append(_const_spec((1, d)))
    return pl.pallas_call(
        functools.partial(_ffn_kernel, has_mix=mix is not None, final_norm=norm_final is not None),
        grid=grid,
        in_specs=specs,
        out_specs=tile(d),
        out_shape=jax.ShapeDtypeStruct(h.shape, F32),
        compiler_params=pltpu.CompilerParams(
            dimension_semantics=("arbitrary", "arbitrary"), vmem_limit_bytes=VMEM_LIMIT),
        name="ffn_mix" if mix is not None else "ffn",
    )(*args)


def _mix_in_kernel(h_ref, mod_ref, ng_ref, win_ref, qn_ref, wuq_ref, kvn_ref, wukt_ref, convw_ref,
                   poolw_ref, pools_ref, cos_ref, sin_ref, cstate_ref, pstate_ref,
                   q_ref, k_ref, ckv_ref, kr_ref, sp_ref, cnew_ref, pnew_ref,
                   eu_ref, ep_ref, s2_ref, s4_ref, s8_ref, *, tm, pos0):
    t = pl.program_id(1)
    x = h_ref[0]
    mod = mod_ref[0]
    n = _rms(x, ng_ref[...]) * (1.0 + mod[4:5, :]) + mod[3:4, :]
    proj = _dot(n.astype(BF16), win_ref[...])
    cos = cos_ref[...]
    sin = sin_ref[...]
    lane = lax.broadcasted_iota(jnp.int32, (tm, LANES), 1)

    def rope(packed):
        return packed * cos + pltpu.roll(packed, ROPE_DIM, axis=1) * sin

    qn = _rms(proj[:, IN_CQ:IN_CKV], qn_ref[...]).astype(BF16)
    qall = _dot(qn, wuq_ref[...])
    for hd in range(MLA_HEADS):
        q_nope = qall[:, UQ_NOPE + hd * NOPE_DIM:UQ_NOPE + (hd + 1) * NOPE_DIM].astype(BF16)
        q_lat = _dot(q_nope, wukt_ref[hd])
        q_rope = jnp.where(lane < ROPE_DIM, rope(qall[:, UQ_ROPE + hd * LANES:UQ_ROPE + (hd + 1) * LANES]), 0.0)
        q_ref[0, hd] = (jnp.concatenate([q_lat, q_rope], axis=-1) * (ATTN_SCALE * LOG2_E)).astype(BF16)

    ckv = _rms(proj[:, IN_CKV:IN_KR], kvn_ref[...])
    kr = rope(proj[:, IN_KR:IN_CB])
    ckv_ref[0] = ckv
    kr_ref[0] = kr[:, :ROPE_DIM]
    k_ref[0] = jnp.concatenate([ckv, jnp.where(lane < ROPE_DIM, kr, 1.0)], axis=-1).astype(BF16)

    @pl.when(t == 0)
    def _():
        eu_ref[0:HALO - (CONV_W - 1), :] = jnp.zeros((HALO - (CONV_W - 1), CONV_DIM), F32)
        eu_ref[HALO - (CONV_W - 1):HALO, :] = cstate_ref[0]
        ep_ref[0:1, :] = jnp.zeros((1, POOL_DIM), F32)
        ep_ref[1:HALO, :] = pstate_ref[0]

    u = proj[:, IN_CC:IN_CV] * proj[:, IN_CV:IN_PU]
    eu_ref[HALO:HALO + tm, :] = u
    cw = convw_ref[...]
    y = u * cw[CONV_W - 1:CONV_W, :]
    for j in range(CONV_W - 1):
        back = CONV_W - 1 - j
        y = y + eu_ref[HALO - back:HALO - back + tm, :] * cw[j:j + 1, :]
    sp_ref[0, :, 0:CONV_DIM] = (proj[:, IN_CB:IN_CC] * y).astype(BF16)
    cnew_ref[0] = eu_ref[HALO + tm - (CONV_W - 1):HALO + tm, :]
    eu_ref[0:HALO, :] = eu_ref[tm:tm + HALO, :]

    pu = proj[:, IN_PU:IN_END]
    ep_ref[HALO:HALO + tm, :] = pu
    n_ext = tm + HALO
    s2_ref[1:n_ext, :] = ep_ref[1:n_ext, :] + ep_ref[0:n_ext - 1, :]
    s4_ref[3:n_ext, :] = s2_ref[3:n_ext, :] + s2_ref[1:n_ext - 2, :]
    s8_ref[7:n_ext, :] = s4_ref[7:n_ext, :] + s4_ref[3:n_ext - 4, :]
    s16 = s8_ref[HALO:n_ext, :] + s8_ref[HALO - 8:n_ext - 8, :]
    lane_grp = lax.broadcasted_iota(jnp.int32, (tm, POOL_DIM), 1) >> (POOL_GROUP_DIM.bit_length() - 1)
    wsum = jnp.where(lane_grp == 0, s2_ref[HALO:n_ext, :],
                     jnp.where(lane_grp == 1, s4_ref[HALO:n_ext, :],
                               jnp.where(lane_grp == 2, s8_ref[HALO:n_ext, :], s16)))
    win = jnp.left_shift(2, lane_grp)
    pos = pos0 + t * tm + lax.broadcasted_iota(jnp.int32, (tm, POOL_DIM), 0)
    cnt = jnp.minimum(win, pos + 1).astype(F32)
    dlt = (wsum / cnt - pu).astype(BF16)
    sp_ref[0, :, CONV_DIM:] = (_dot(dlt, poolw_ref[...]) * pools_ref[...]).astype(BF16)
    pnew_ref[0] = ep_ref[tm + 1:tm + HALO, :]
    ep_ref[0:HALO, :] = ep_ref[tm:tm + HALO, :]


def _mix_in(h, mod, layer, lw, cos_t, sin_t, conv_state, pool_state, *, tm, pos0):
    n_b, n_t, d = h.shape
    grid = (n_b, n_t // tm)
    rows = lambda w: pl.BlockSpec((1, tm, w), lambda b, t: (b, t, 0))
    per_b = lambda r, w: pl.BlockSpec((1, r, w), lambda b, t: (b, 0, 0))
    weights = [lw[k] for k in ("norm_mix", "w_in", "q_norm", "w_uq", "kv_norm", "w_ukt", "conv_w",
                               "pool_bd", "pool_scale")]
    in_specs = [rows(d), per_b(N_ADA, d)] + [_layer_spec(w, layer) for w in weights] + [
        pl.BlockSpec((tm, LANES), lambda b, t: (t, 0)), pl.BlockSpec((tm, LANES), lambda b, t: (t, 0)),
        per_b(CONV_W - 1, CONV_DIM), per_b(POOL_MAX - 1, POOL_DIM),
    ]
    out_specs = [
        pl.BlockSpec((1, MLA_HEADS, tm, KEY_DIM), lambda b, t: (b, 0, t, 0)),
        rows(KEY_DIM), rows(KV_RANK), rows(ROPE_DIM), rows(CONV_DIM + POOL_DIM),
        per_b(CONV_W - 1, CONV_DIM), per_b(POOL_MAX - 1, POOL_DIM),
    ]
    out_shape = [
        jax.ShapeDtypeStruct((n_b, MLA_HEADS, n_t, KEY_DIM), BF16),
        jax.ShapeDtypeStruct((n_b, n_t, KEY_DIM), BF16),
        jax.ShapeDtypeStruct((n_b, n_t, KV_RANK), F32),
        jax.ShapeDtypeStruct((n_b, n_t, ROPE_DIM), F32),
        jax.ShapeDtypeStruct((n_b, n_t, CONV_DIM + POOL_DIM), BF16),
        jax.ShapeDtypeStruct((n_b, CONV_W - 1, CONV_DIM), F32),
        jax.ShapeDtypeStruct((n_b, POOL_MAX - 1, POOL_DIM), F32),
    ]
    ext = pltpu.VMEM((tm + HALO, POOL_DIM), F32)
    return pl.pallas_call(
        functools.partial(_mix_in_kernel, tm=tm, pos0=pos0),
        grid=grid,
        in_specs=in_specs,
        out_specs=out_specs,
        out_shape=out_shape,
        scratch_shapes=[ext, ext, ext, ext, ext],
        compiler_params=pltpu.CompilerParams(
            dimension_semantics=("arbitrary", "arbitrary"), vmem_limit_bytes=VMEM_LIMIT),
        name="mix_in",
    )(h, mod, *weights, cos_t, sin_t, conv_state, pool_state)


def _attn_kernel(*refs, tq, tk, tk_past, past_len, group_rows):
    refs = list(refs)
    q_ref, knew_ref = refs.pop(0), refs.pop(0)
    kpast_ref = refs.pop(0) if past_len else None
    wuv_ref, o_ref, m_ref, acc_ref = refs[:4]
    s_ref = None if past_len else refs[4]

    i = pl.program_id(1)
    m_rows = MLA_HEADS * tq
    n_groups = m_rows // group_rows
    m_ref[...] = jnp.full(m_ref.shape, NEG_BIG, F32)
    acc_ref[...] = jnp.zeros(acc_ref.shape, F32)

    def lane_tile(x, width):
        return x[:, :width] if width < LANES else jnp.concatenate([x] * (width // LANES), axis=1)

    def group(g):
        return slice(g * group_rows, (g + 1) * group_rows)

    def scores(g, kblk):
        if group_rows >= tq:
            heads = group_rows // tq
            q = q_ref[0, g * heads:(g + 1) * heads].reshape(group_rows, KEY_DIM)
        else:
            per_head = tq // group_rows
            r0 = (g % per_head) * group_rows
            q = q_ref[0, g // per_head, r0:r0 + group_rows, :]
        return lax.dot_general(q, kblk, (((1,), (1,)), ((), ())), preferred_element_type=F32)

    def absorb(g, s, kblk, mask):
        width = kblk.shape[0]
        rows = group(g)
        if mask is not None:
            s = jnp.where(mask, s, NEG_BIG)
        m_prev = m_ref[rows, :]
        m_new = jnp.maximum(m_prev, jnp.max(s, axis=1, keepdims=True))
        alpha = jnp.exp2(m_prev - m_new)
        p = jnp.exp2(s - lane_tile(m_new, width)).astype(BF16)
        acc_ref[rows, :] = lane_tile(alpha, KEY_DIM) * acc_ref[rows, :] + _dot(p, kblk)
        m_ref[rows, :] = m_new

    chunk_shift = CHUNK.bit_length() - 1

    def diag_mask(g, col0, width):
        r0, n_r = ((g * group_rows) % tq, group_rows) if group_rows < tq else (0, tq)
        if (col0 + width - 1) >> chunk_shift <= r0 >> chunk_shift:
            return "all"
        if col0 >> chunk_shift > (r0 + n_r - 1) >> chunk_shift:
            return "none"
        row = (lax.broadcasted_iota(jnp.int32, (group_rows, width), 0) + r0) & (tq - 1)
        col = lax.broadcasted_iota(jnp.int32, (group_rows, width), 1) + col0
        return (col >> chunk_shift) <= (row >> chunk_shift)

    def mask_arg(vis):
        return None if isinstance(vis, str) else vis

    def any_visible(vis):
        return not (isinstance(vis, str) and vis == "none")

    if past_len:
        def past_body(j, carry):
            kblk = kpast_ref[0, pl.ds(pl.multiple_of(j * tk_past, tk_past), tk_past), :]
            for g in range(n_groups):
                absorb(g, scores(g, kblk), kblk, None)
            return carry
        lax.fori_loop(0, past_len // tk_past, past_body, 0)
        kblk = knew_ref[0]
        for g in range(n_groups):
            absorb(g, scores(g, kblk), kblk, mask_arg(diag_mask(g, 0, tq)))
    else:
        def new_block(j):
            return knew_ref[0, pl.ds(pl.multiple_of(j * tk, tk), tk), :]

        n_diag = tq // tk
        n_full = i * n_diag
        first = new_block(0)
        for g in range(n_groups):
            s_ref[group(g), :] = scores(g, first)

        def body(j, carry):
            kblk, knext = new_block(j), new_block(j + 1)
            for g in range(n_groups):
                s = s_ref[group(g), :]
                s_ref[group(g), :] = scores(g, knext)
                absorb(g, s, kblk, None)
            return carry
        lax.fori_loop(0, n_full, body, 0)

        for d in range(n_diag):
            kblk = new_block(n_full + d)
            knext = new_block(n_full + d + 1) if d + 1 < n_diag else None
            for g in range(n_groups):
                vis = diag_mask(g, d * tk, tk)
                if any_visible(vis):
                    s = s_ref[group(g), :]
                if knext is not None and any_visible(diag_mask(g, (d + 1) * tk, tk)):
                    s_ref[group(g), :] = scores(g, knext)
                if any_visible(vis):
                    absorb(g, s, kblk, mask_arg(vis))

    heads = []
    for hd in range(MLA_HEADS):
        acc = acc_ref[hd * tq:(hd + 1) * tq, :]
        denom = acc[:, KV_RANK + ROPE_DIM:KV_RANK + ROPE_DIM + 1]
        heads.append(_dot((acc[:, :KV_RANK] / denom).astype(BF16), wuv_ref[hd]))
    o_ref[0] = jnp.concatenate(heads, axis=-1).astype(BF16)


def _attn(q, k_new, k_past, layer, w_uv, *, tq, tk, tk_past, group_rows):
    n_b, _, n_t, _ = q.shape
    past_len = 0 if k_past is None else k_past.shape[2]
    assert (past_len % CHUNK == 0) and (tq % CHUNK == 0 or n_t == tq <= CHUNK)
    assert past_len == 0 or (n_t == tq and past_len % tk_past == 0)
    m_rows = MLA_HEADS * tq
    assert m_rows % group_rows == 0 and (group_rows % tq == 0 or tq % group_rows == 0)
    assert past_len or (tq % tk == 0 and tk % CHUNK == 0)
    scratch = [pltpu.VMEM((m_rows, LANES), F32), pltpu.VMEM((m_rows, KEY_DIM), F32)]
    if not past_len:
        scratch.append(pltpu.VMEM((m_rows, tk), F32))
    args = [q, k_new]
    specs = [pl.BlockSpec((1, MLA_HEADS, tq, KEY_DIM), lambda b, i: (b, 0, i, 0)),
             pl.BlockSpec((1, n_t, KEY_DIM), lambda b, i: (b, 0, 0), pipeline_mode=pl.Buffered(1))]
    if past_len:
        args.append(k_past)
        specs.append(pl.BlockSpec((None, 1, past_len, KEY_DIM), lambda b, i: (layer, b, 0, 0)))
    args.append(w_uv)
    specs.append(_layer_spec(w_uv, layer))
    return pl.pallas_call(
        functools.partial(_attn_kernel, tq=tq, tk=tk, tk_past=tk_past, past_len=past_len,
                          group_rows=group_rows),
        grid=(n_b, n_t // tq),
        in_specs=specs,
        out_specs=pl.BlockSpec((1, tq, MLA_HEADS * V_DIM), lambda b, i: (b, i, 0)),
        out_shape=jax.ShapeDtypeStruct((n_b, n_t, MLA_HEADS * V_DIM), BF16),
        scratch_shapes=scratch,
        compiler_params=pltpu.CompilerParams(
            dimension_semantics=("arbitrary", "arbitrary"), vmem_limit_bytes=VMEM_LIMIT),
        name="attn",
    )(*args)


def _prep_weights(w_ffn1_gu, w_ffn1_down, norm_ffn1, norm_mix, w_in, q_norm, w_uq, kv_norm, w_ukv,
                  conv_w, pool_w, pool_scale, w_out, norm_ffn2, w_ffn2_gu, w_ffn2_down):
    n_l = w_in.shape[0]
    half = ROPE_DIM // 2
    packed = lambda w: jnp.concatenate([w, w[..., half:], w[..., :half]], axis=-1)
    vec = lambda v: v.reshape(n_l, 1, v.shape[-1])
    cq, ckv, kr, rest = (w_in[..., :Q_RANK], w_in[..., Q_RANK:Q_RANK + KV_RANK],
                         w_in[..., Q_RANK + KV_RANK:Q_RANK + KV_RANK + ROPE_DIM],
                         w_in[..., Q_RANK + KV_RANK + ROPE_DIM:])
    w_in_p = jnp.concatenate([cq, ckv, packed(kr), rest], axis=-1).astype(BF16)

    wq = w_uq.reshape(n_l, Q_RANK, MLA_HEADS, NOPE_DIM + ROPE_DIM)
    nope = wq[..., :NOPE_DIM].reshape(n_l, Q_RANK, MLA_HEADS * NOPE_DIM)
    rope_cols = [wq[:, :, hd, NOPE_DIM:] for hd in range(MLA_HEADS)]
    w_uq_p = jnp.concatenate([nope] + [packed(r) for r in rope_cols], axis=-1).astype(BF16)

    wkv = w_ukv.reshape(n_l, KV_RANK, MLA_HEADS, NOPE_DIM + V_DIM)
    w_ukt = jnp.transpose(wkv[..., :NOPE_DIM], (0, 2, 3, 1)).astype(BF16)
    w_uv = jnp.transpose(wkv[..., NOPE_DIM:], (0, 2, 1, 3)).astype(BF16)

    eye = jnp.eye(POOL_GROUPS, dtype=F32)
    pool_bd = (pool_w[:, :, :, None, :] * eye[None, :, None, :, None]).reshape(
        n_l, POOL_DIM, POOL_DIM).astype(BF16)

    return {
        "norm_ffn1": vec(norm_ffn1), "w_ffn1_gu": w_ffn1_gu.astype(BF16),
        "w_ffn1_down": w_ffn1_down.astype(BF16),
        "norm_mix": vec(norm_mix), "w_in": w_in_p, "q_norm": vec(q_norm), "w_uq": w_uq_p,
        "kv_norm": vec(kv_norm), "w_ukt": w_ukt, "w_uv": w_uv,
        "conv_w": conv_w, "pool_bd": pool_bd, "pool_scale": vec(pool_scale),
        "w_out": w_out.astype(BF16),
        "norm_ffn2": vec(norm_ffn2), "w_ffn2_gu": w_ffn2_gu.astype(BF16),
        "w_ffn2_down": w_ffn2_down.astype(BF16),
    }


def _past_keys(cache_ckv, cache_krope):
    pad = jnp.ones(cache_ckv.shape[:-1] + (KEY_DIM - KV_RANK - ROPE_DIM,), BF16)
    return jnp.concatenate([cache_ckv.astype(BF16), cache_krope.astype(BF16), pad], axis=-1)


def _trunk_layer(h, mod, layer, lw, tables, conv_state, pool_state, k_past, norm_final, *, bt, tt, tm,
                 tq, group_rows, pos0):
    h1 = _ffn(h, mod, layer, lw["norm_ffn1"], lw["w_ffn1_gu"], lw["w_ffn1_down"], bt=bt, tt=tt)
    q, k_new, ckv, kr, sp, conv_new, pool_new = _mix_in(
        h1, mod, layer, lw, tables[0], tables[1], conv_state, pool_state, tm=tm, pos0=pos0)
    a = _attn(q, k_new, k_past, layer, lw["w_uv"], tq=tq, tk=min(tq, 512), tk_past=2048,
              group_rows=group_rows)
    h3 = _ffn(h1, mod, layer, lw["norm_ffn2"], lw["w_ffn2_gu"], lw["w_ffn2_down"], bt=bt, tt=tt,
              mix=(a, sp, lw["w_out"]), norm_final=norm_final)
    return h3, ckv, kr, conv_new, pool_new


def kernel(x_prompt, x_sample, c_prompt, c_sample, cache_ckv, cache_krope, state_conv, state_pool, w_ada, b_ada, norm_ffn1, w_ffn1_gu, w_ffn1_down, norm_mix, w_in, q_norm, w_uq, kv_norm, w_ukv, conv_w, pool_w, pool_scale, w_out, norm_ffn2, w_ffn2_gu, w_ffn2_down, norm_final):
    bp, seq, _ = x_prompt.shape
    bs, dec_seq, _ = x_sample.shape
    past_len = cache_ckv.shape[2]
    assert bp + bs <= ADA_ROWS

    c_all = jnp.concatenate(
        [c_prompt, c_sample, jnp.zeros((ADA_ROWS - bp - bs, D_MODEL), F32)], axis=0)
    mods = _ada(c_all, w_ada, b_ada).reshape(DEPTH, ADA_ROWS, N_ADA, D_MODEL)

    tm_p = 512
    tables_p = _rope_tables(seq, 0, tm_p)
    tables_s = _rope_tables(dec_seq, past_len, dec_seq)
    lw = _prep_weights(w_ffn1_gu, w_ffn1_down, norm_ffn1, norm_mix, w_in, q_norm, w_uq, kv_norm,
                       w_ukv, conv_w, pool_w, pool_scale, w_out, norm_ffn2, w_ffn2_gu, w_ffn2_down)
    k_past = _past_keys(cache_ckv, cache_krope)

    hp, hs = x_prompt, x_sample
    outs_p, outs_s = [], []
    zero_conv = jnp.zeros((bp, CONV_W - 1, CONV_DIM), F32)
    zero_pool = jnp.zeros((bp, POOL_MAX - 1, POOL_DIM), F32)
    for l in range(DEPTH):
        nf = norm_final if l == DEPTH - 1 else None
        hp, *op = _trunk_layer(hp, mods[l, :bp], l, lw, tables_p, zero_conv, zero_pool, None, nf,
                               bt=1, tt=tm_p, tm=tm_p, tq=1024, group_rows=512, pos0=0)
        hs, *os_ = _trunk_layer(hs, mods[l, bp:bp + bs], l, lw, tables_s, state_conv[l], state_pool[l],
                                k_past, nf, bt=bs, tt=dec_seq, tm=dec_seq, tq=dec_seq,
                                group_rows=MLA_HEADS * dec_seq, pos0=past_len)
        outs_p.append(op)
        outs_s.append(os_)

    stack = lambda outs, k: jnp.stack([o[k] for o in outs])
    return (hp, hs,
            stack(outs_p, 0), stack(outs_p, 1), stack(outs_p, 2), stack(outs_p, 3),
            stack(outs_s, 0), stack(outs_s, 1), stack(outs_s, 2), stack(outs_s, 3))
```

```python
import functools

import jax
import jax.numpy as jnp
import numpy as np
from jax import lax
from jax.experimental import pallas as pl
from jax.experimental.pallas import tpu as pltpu

F32 = jnp.float32
BF16 = jnp.bfloat16

D_MODEL = 1024
DEPTH = 2
CHUNK = 64
EPS = 1e-6
N_ADA = 9
D_FF = 2816
MLA_HEADS = 4
Q_RANK = 256
KV_RANK = 128
NOPE_DIM = 128
ROPE_DIM = 64
V_DIM = 128
ROPE_BASE = 10000.0
ATTN_SCALE = (NOPE_DIM + ROPE_DIM) ** -0.5
LOG2_E = 1.4426950408889634
CONV_DIM = 256
CONV_W = 3
POOL_WINDOWS = (2, 4, 8, 16)
POOL_GROUPS = 4
POOL_DIM = 256
POOL_GROUP_DIM = POOL_DIM // POOL_GROUPS
POOL_MAX = 16

LANES = 128
KEY_DIM = 2 * LANES
FF_CHUNK = 256
ADA_ROWS = 16
ADA_COLS = 1024
HALO = 16
NEG_BIG = -1e30
VMEM_LIMIT = 56 * 1024 * 1024

IN_CQ, IN_CKV, IN_KR, IN_CB, IN_CC, IN_CV, IN_PU, IN_END = 0, 256, 384, 512, 768, 1024, 1280, 1536
UQ_NOPE, UQ_ROPE, UQ_END = 0, 512, 1024


def _rms(x, g):
    return x * lax.rsqrt(jnp.mean(x * x, axis=-1, keepdims=True) + EPS) * g


def _silu(x):
    return x * jax.nn.sigmoid(x)


def _dot(a, b):
    return jnp.dot(a, b, preferred_element_type=F32)


def _const_spec(shape):
    nd = len(shape)
    return pl.BlockSpec(shape, lambda *_: (0,) * nd, pipeline_mode=pl.Buffered(1))


def _layer_spec(stacked, layer):
    rest = stacked.shape[1:]
    return pl.BlockSpec((None,) + rest, lambda *_: (layer,) + (0,) * len(rest),
                        pipeline_mode=pl.Buffered(1))


def _ada_kernel(c_ref, w_ref, b_ref, o_ref):
    a = _silu(c_ref[...]).astype(BF16)
    o_ref[0] = _dot(a, w_ref[0].astype(BF16)) + b_ref[0]


def _ada(c_all, w_ada, b_ada):
    n_l, _, n_cols = w_ada.shape
    return pl.pallas_call(
        _ada_kernel,
        grid=(n_l, n_cols // ADA_COLS),
        in_specs=[
            pl.BlockSpec((ADA_ROWS, D_MODEL), lambda l, j: (0, 0)),
            pl.BlockSpec((1, D_MODEL, ADA_COLS), lambda l, j: (l, 0, j)),
            pl.BlockSpec((1, 1, ADA_COLS), lambda l, j: (l, 0, j)),
        ],
        out_specs=pl.BlockSpec((1, ADA_ROWS, ADA_COLS), lambda l, j: (l, 0, j)),
        out_shape=jax.ShapeDtypeStruct((n_l, ADA_ROWS, n_cols), F32),
        compiler_params=pltpu.CompilerParams(
            dimension_semantics=("arbitrary", "arbitrary"), vmem_limit_bytes=VMEM_LIMIT),
        name="ada",
    )(c_all, w_ada, b_ada.reshape(n_l, 1, n_cols))


def _rope_table_kernel(inv_ref, sign_ref, cos_ref, sin_ref, cos_row_ref, sin_row_ref, *, pos0, tm):
    t = pl.program_id(0)

    @pl.when(t == 0)
    def _():
        row = lax.broadcasted_iota(jnp.int32, (tm, LANES), 0).astype(F32)
        ang = row * inv_ref[...]
        cos_row_ref[...] = jnp.cos(ang)
        sin_row_ref[...] = jnp.sin(ang)

    start = (pos0 + t * tm).astype(F32) * inv_ref[...]
    cos_s, sin_s = jnp.cos(start), jnp.sin(start)
    cos_r, sin_r = cos_row_ref[...], sin_row_ref[...]
    cos_ref[...] = cos_s * cos_r - sin_s * sin_r
    sin_ref[...] = (sin_s * cos_r + cos_s * sin_r) * sign_ref[...]


def _rope_tables(n_rows, pos0, tm):
    half = ROPE_DIM // 2
    inv = ROPE_BASE ** (-jnp.arange(half, dtype=F32) / half)
    inv = jnp.tile(inv, LANES // half).reshape(1, LANES)
    sign = jnp.tile(jnp.concatenate([-jnp.ones((half,), F32), jnp.ones((half,), F32)]),
                    LANES // ROPE_DIM).reshape(1, LANES)
    return pl.pallas_call(
        functools.partial(_rope_table_kernel, pos0=pos0, tm=tm),
        grid=(n_rows // tm,),
        in_specs=[_const_spec((1, LANES)), _const_spec((1, LANES))],
        out_specs=[pl.BlockSpec((tm, LANES), lambda i: (i, 0))] * 2,
        out_shape=[jax.ShapeDtypeStruct((n_rows, LANES), F32)] * 2,
        scratch_shapes=[pltpu.VMEM((tm, LANES), F32)] * 2,
        compiler_params=pltpu.CompilerParams(dimension_semantics=("arbitrary",)),
        name="rope_tables",
    )(inv, sign)


def _ffn_kernel(*refs, has_mix, final_norm):
    refs = list(refs)
    h_ref = refs.pop(0)
    if has_mix:
        a_ref, sp_ref, wo_ref = refs.pop(0), refs.pop(0), refs.pop(0)
    mod_ref, ng_ref, wgu_ref, wd_ref = refs.pop(0), refs.pop(0), refs.pop(0), refs.pop(0)
    if final_norm:
        nf_ref = refs.pop(0)
    (o_ref,) = refs

    bt, tt, d = h_ref.shape
    m = bt * tt
    x = h_ref[...]
    mod = mod_ref[...]
    row0 = 0
    if has_mix:
        mix_w = a_ref.shape[-1]
        mix = _dot(a_ref[...].reshape(m, mix_w), wo_ref[0:mix_w, :])
        mix = mix + _dot(sp_ref[...].reshape(m, sp_ref.shape[-1]), wo_ref[mix_w:, :])
        x = x + mod[:, 5:6, :] * mix.reshape(bt, tt, d)
        row0 = 6
    shift, scale, gate = mod[:, row0:row0 + 1, :], mod[:, row0 + 1:row0 + 2, :], mod[:, row0 + 2:row0 + 3, :]
    n = _rms(x, ng_ref[...]) * (1.0 + scale) + shift
    nb = n.reshape(m, d).astype(BF16)
    acc = jnp.zeros((m, d), F32)
    for c in range(D_FF // FF_CHUNK):
        g = _dot(nb, wgu_ref[:, c * FF_CHUNK:(c + 1) * FF_CHUNK])
        u = _dot(nb, wgu_ref[:, D_FF + c * FF_CHUNK:D_FF + (c + 1) * FF_CHUNK])
        act = (_silu(g) * u).astype(BF16)
        acc = acc + _dot(act, wd_ref[c * FF_CHUNK:(c + 1) * FF_CHUNK, :])
    y = x + 0.5 * gate * acc.reshape(bt, tt, d)
    if final_norm:
        y = _rms(y, nf_ref[...])
    o_ref[...] = y


def _ffn(h, mod, layer, norm_g, w_gu, w_down, *, bt, tt, mix=None, norm_final=None):
    n_b, n_t, d = h.shape
    grid = (n_b // bt, n_t // tt)
    tile = lambda w: pl.BlockSpec((bt, tt, w), lambda b, t: (b, t, 0))
    args, specs = [h], [tile(d)]
    if mix is not None:
        a, sp, w_out = mix
        args += [a, sp, w_out]
        specs += [tile(a.shape[-1]), tile(sp.shape[-1]), _layer_spec(w_out, layer)]
    args += [mod, norm_g, w_gu, w_down]
    specs += [pl.BlockSpec((bt, N_ADA, d), lambda b, t: (b, 0, 0)), _layer_spec(norm_g, layer),
              _layer_spec(w_gu, layer), _layer_spec(w_down, layer)]
    if norm_final is not None:
        args.append(norm_final.reshape(1, d))
        specs.append(_const_spec((1, d)))
    return pl.pallas_call(
        functools.partial(_ffn_kernel, has_mix=mix is not None, final_norm=norm_final is not None),
        grid=grid,
        in_specs=specs,
        out_specs=tile(d),
        out_shape=jax.ShapeDtypeStruct(h.shape, F32),
        compiler_params=pltpu.CompilerParams(
            dimension_semantics=("arbitrary", "arbitrary"), vmem_limit_bytes=VMEM_LIMIT),
        name="ffn_mix" if mix is not None else "ffn",
    )(*args)


def _mix_in_kernel(h_ref, mod_ref, ng_ref, win_ref, qn_ref, wuq_ref, kvn_ref, wukt_ref, convw_ref,
                   poolw_ref, pools_ref, cos_ref, sin_ref, cstate_ref, pstate_ref,
                   q_ref, k_ref, ckv_ref, kr_ref, sp_ref, cnew_ref, pnew_ref,
                   eu_ref, ep_ref, s2_ref, s4_ref, s8_ref, *, tm, pos0):
    t = pl.program_id(1)
    x = h_ref[0]
    mod = mod_ref[0]
    n = _rms(x, ng_ref[...]) * (1.0 + mod[4:5, :]) + mod[3:4, :]
    proj = _dot(n.astype(BF16), win_ref[...])
    cos = cos_ref[...]
    sin = sin_ref[...]
    lane = lax.broadcasted_iota(jnp.int32, (tm, LANES), 1)

    def rope(packed):
        return packed * cos + pltpu.roll(packed, ROPE_DIM, axis=1) * sin

    qn = _rms(proj[:, IN_CQ:IN_CKV], qn_ref[...]).astype(BF16)
    qall = _dot(qn, wuq_ref[...])
    for hd in range(MLA_HEADS):
        q_nope = qall[:, UQ_NOPE + hd * NOPE_DIM:UQ_NOPE + (hd + 1) * NOPE_DIM].astype(BF16)
        q_lat = _dot(q_nope, wukt_ref[hd])
        q_rope = jnp.where(lane < ROPE_DIM, rope(qall[:, UQ_ROPE + hd * LANES:UQ_ROPE + (hd + 1) * LANES]), 0.0)
        q_ref[0, hd] = (jnp.concatenate([q_lat, q_rope], axis=-1) * (ATTN_SCALE * LOG2_E)).astype(BF16)

    ckv = _rms(proj[:, IN_CKV:IN_KR], kvn_ref[...])
    kr = rope(proj[:, IN_KR:IN_CB])
    ckv_ref[0] = ckv
    kr_ref[0] = kr[:, :ROPE_DIM]
    k_ref[0] = jnp.concatenate([ckv, jnp.where(lane < ROPE_DIM, kr, 1.0)], axis=-1).astype(BF16)

    @pl.when(t == 0)
    def _():
        eu_ref[0:HALO - (CONV_W - 1), :] = jnp.zeros((HALO - (CONV_W - 1), CONV_DIM), F32)
        eu_ref[HALO - (CONV_W - 1):HALO, :] = cstate_ref[0]
        ep_ref[0:1, :] = jnp.zeros((1, POOL_DIM), F32)
        ep_ref[1:HALO, :] = pstate_ref[0]

    u = proj[:, IN_CC:IN_CV] * proj[:, IN_CV:IN_PU]
    eu_ref[HALO:HALO + tm, :] = u
    cw = convw_ref[...]
    y = u * cw[CONV_W - 1:CONV_W, :]
    for j in range(CONV_W - 1):
        back = CONV_W - 1 - j
        y = y + eu_ref[HALO - back:HALO - back + tm, :] * cw[j:j + 1, :]
    sp_ref[0, :, 0:CONV_DIM] = (proj[:, IN_CB:IN_CC] * y).astype(BF16)
    cnew_ref[0] = eu_ref[HALO + tm - (CONV_W - 1):HALO + tm, :]
    eu_ref[0:HALO, :] = eu_ref[tm:tm + HALO, :]

    pu = proj[:, IN_PU:IN_END]
    ep_ref[HALO:HALO + tm, :] = pu
    n_ext = tm + HALO
    s2_ref[1:n_ext, :] = ep_ref[1:n_ext, :] + ep_ref[0:n_ext - 1, :]
    s4_ref[3:n_ext, :] = s2_ref[3:n_ext, :] + s2_ref[1:n_ext - 2, :]
    s8_ref[7:n_ext, :] = s4_ref[7:n_ext, :] + s4_ref[3:n_ext - 4, :]
    s16 = s8_ref[HALO:n_ext, :] + s8_ref[HALO - 8:n_ext - 8, :]
    lane_grp = lax.broadcasted_iota(jnp.int32, (tm, POOL_DIM), 1) >> (POOL_GROUP_DIM.bit_length() - 1)
    wsum = jnp.where(lane_grp == 0, s2_ref[HALO:n_ext, :],
                     jnp.where(lane_grp == 1, s4_ref[HALO:n_ext, :],
                               jnp.where(lane_grp == 2, s8_ref[HALO:n_ext, :], s16)))
    win = jnp.left_shift(2, lane_grp)
    pos = pos0 + t * tm + lax.broadcasted_iota(jnp.int32, (tm, POOL_DIM), 0)
    cnt = jnp.minimum(win, pos + 1).astype(F32)
    dlt = (wsum / cnt - pu).astype(BF16)
    sp_ref[0, :, CONV_DIM:] = (_dot(dlt, poolw_ref[...]) * pools_ref[...]).astype(BF16)
    pnew_ref[0] = ep_ref[tm + 1:tm + HALO, :]
    ep_ref[0:HALO, :] = ep_ref[tm:tm + HALO, :]


def _mix_in(h, mod, layer, lw, cos_t, sin_t, conv_state, pool_state, *, tm, pos0):
    n_b, n_t, d = h.shape
    grid = (n_b, n_t // tm)
    rows = lambda w: pl.BlockSpec((1, tm, w), lambda b, t: (b, t, 0))
    per_b = lambda r, w: pl.BlockSpec((1, r, w), lambda b, t: (b, 0, 0))
    weights = [lw[k] for k in ("norm_mix", "w_in", "q_norm", "w_uq", "kv_norm", "w_ukt", "conv_w",
                               "pool_bd", "pool_scale")]
    in_specs = [rows(d), per_b(N_ADA, d)] + [_layer_spec(w, layer) for w in weights] + [
        pl.BlockSpec((tm, LANES), lambda b, t: (t, 0)), pl.BlockSpec((tm, LANES), lambda b, t: (t, 0)),
        per_b(CONV_W - 1, CONV_DIM), per_b(POOL_MAX - 1, POOL_DIM),
    ]
    out_specs = [
        pl.BlockSpec((1, MLA_HEADS, tm, KEY_DIM), lambda b, t: (b, 0, t, 0)),
        rows(KEY_DIM), rows(KV_RANK), rows(ROPE_DIM), rows(CONV_DIM + POOL_DIM),
        per_b(CONV_W - 1, CONV_DIM), per_b(POOL_MAX - 1, POOL_DIM),
    ]
    out_shape = [
        jax.ShapeDtypeStruct((n_b, MLA_HEADS, n_t, KEY_DIM), BF16),
        jax.ShapeDtypeStruct((n_b, n_t, KEY_DIM), BF16),
        jax.ShapeDtypeStruct((n_b, n_t, KV_RANK), F32),
        jax.ShapeDtypeStruct((n_b, n_t, ROPE_DIM), F32),
        jax.ShapeDtypeStruct((n_b, n_t, CONV_DIM + POOL_DIM), BF16),
        jax.ShapeDtypeStruct((n_b, CONV_W - 1, CONV_DIM), F32),
        jax.ShapeDtypeStruct((n_b, POOL_MAX - 1, POOL_DIM), F32),
    ]
    ext = pltpu.VMEM((tm + HALO, POOL_DIM), F32)
    return pl.pallas_call(
        functools.partial(_mix_in_kernel, tm=tm, pos0=pos0),
        grid=grid,
        in_specs=in_specs,
        out_specs=out_specs,
        out_shape=out_shape,
        scratch_shapes=[ext, ext, ext, ext, ext],
        compiler_params=pltpu.CompilerParams(
            dimension_semantics=("arbitrary", "arbitrary"), vmem_limit_bytes=VMEM_LIMIT),
        name="mix_in",
    )(h, mod, *weights, cos_t, sin_t, conv_state, pool_state)


def _attn_kernel(*refs, tq, tk, tk_past, past_len, group_rows):
    refs = list(refs)
    q_ref, knew_ref = refs.pop(0), refs.pop(0)
    if past_len:
        ckv_past_ref, kr_past_ref = refs.pop(0), refs.pop(0)
    wuv_ref, o_ref, m_ref, acc_ref = refs[:4]
    s_ref, kpast_ref = (None, refs[4]) if past_len else (refs[4], None)

    i = pl.program_id(1)
    m_rows = MLA_HEADS * tq
    n_groups = m_rows // group_rows
    m_ref[...] = jnp.full(m_ref.shape, NEG_BIG, F32)
    acc_ref[...] = jnp.zeros(acc_ref.shape, F32)

    def lane_tile(x, width):
        return x[:, :width] if width < LANES else jnp.concatenate([x] * (width // LANES), axis=1)

    def group(g):
        return slice(g * group_rows, (g + 1) * group_rows)

    def scores(g, kblk):
        if group_rows >= tq:
            heads = group_rows // tq
            q = q_ref[0, g * heads:(g + 1) * heads].reshape(group_rows, KEY_DIM)
        else:
            per_head = tq // group_rows
            r0 = (g % per_head) * group_rows
            q = q_ref[0, g // per_head, r0:r0 + group_rows, :]
        return lax.dot_general(q, kblk, (((1,), (1,)), ((), ())), preferred_element_type=F32)

    def absorb(g, s, kblk, mask):
        width = kblk.shape[0]
        rows = group(g)
        if mask is not None:
            s = jnp.where(mask, s, NEG_BIG)
        m_prev = m_ref[rows, :]
        m_new = jnp.maximum(m_prev, jnp.max(s, axis=1, keepdims=True))
        alpha = jnp.exp2(m_prev - m_new)
        p = jnp.exp2(s - lane_tile(m_new, width)).astype(BF16)
        acc_ref[rows, :] = lane_tile(alpha, KEY_DIM) * acc_ref[rows, :] + _dot(p, kblk)
        m_ref[rows, :] = m_new

    chunk_shift = CHUNK.bit_length() - 1

    def diag_mask(g, col0, width):
        r0, n_r = ((g * group_rows) % tq, group_rows) if group_rows < tq else (0, tq)
        if (col0 + width - 1) >> chunk_shift <= r0 >> chunk_shift:
            return "all"
        if col0 >> chunk_shift > (r0 + n_r - 1) >> chunk_shift:
            return "none"
        row = (lax.broadcasted_iota(jnp.int32, (group_rows, width), 0) + r0) & (tq - 1)
        col = lax.broadcasted_iota(jnp.int32, (group_rows, width), 1) + col0
        return (col >> chunk_shift) <= (row >> chunk_shift)

    def mask_arg(vis):
        return None if isinstance(vis, str) else vis

    def any_visible(vis):
        return not (isinstance(vis, str) and vis == "none")

    if past_len:
        kpast_ref[:, KV_RANK + ROPE_DIM:] = jnp.ones((tk_past, KEY_DIM - KV_RANK - ROPE_DIM), BF16)

        def past_body(j, carry):
            rows = pl.ds(pl.multiple_of(j * tk_past, tk_past), tk_past)
            kpast_ref[:, :KV_RANK] = ckv_past_ref[0, rows, :].astype(BF16)
            kpast_ref[:, KV_RANK:KV_RANK + ROPE_DIM] = kr_past_ref[0, rows, :].astype(BF16)
            kblk = kpast_ref[...]
            for g in range(n_groups):
                absorb(g, scores(g, kblk), kblk, None)
            return carry
        lax.fori_loop(0, past_len // tk_past, past_body, 0)
        kblk = knew_ref[0]
        for g in range(n_groups):
            absorb(g, scores(g, kblk), kblk, mask_arg(diag_mask(g, 0, tq)))
    else:
        def new_block(j):
            return knew_ref[0, pl.ds(pl.multiple_of(j * tk, tk), tk), :]

        n_diag = tq // tk
        n_full = i * n_diag
        first = new_block(0)
        for g in range(n_groups):
            s_ref[group(g), :] = scores(g, first)

        def body(j, carry):
            kblk, knext = new_block(j), new_block(j + 1)
            for g in range(n_groups):
                s = s_ref[group(g), :]
                s_ref[group(g), :] = scores(g, knext)
                absorb(g, s, kblk, None)
            return carry
        lax.fori_loop(0, n_full, body, 0)

        for d in range(n_diag):
            kblk = new_block(n_full + d)
            knext = new_block(n_full + d + 1) if d + 1 < n_diag else None
            for g in range(n_groups):
                vis = diag_mask(g, d * tk, tk)
                if any_visible(vis):
                    s = s_ref[group(g), :]
                if knext is not None and any_visible(diag_mask(g, (d + 1) * tk, tk)):
                    s_ref[group(g), :] = scores(g, knext)
                if any_visible(vis):
                    absorb(g, s, kblk, mask_arg(vis))

    heads = []
    for hd in range(MLA_HEADS):
        acc = acc_ref[hd * tq:(hd + 1) * tq, :]
        denom = acc[:, KV_RANK + ROPE_DIM:KV_RANK + ROPE_DIM + 1]
        heads.append(_dot((acc[:, :KV_RANK] / denom).astype(BF16), wuv_ref[hd]))
    o_ref[0] = jnp.concatenate(heads, axis=-1).astype(BF16)


def _attn(q, k_new, past, layer, w_uv, *, tq, tk, tk_past, group_rows):
    n_b, _, n_t, _ = q.shape
    past_len = 0 if past is None else past[0].shape[2]
    assert (past_len % CHUNK == 0) and (tq % CHUNK == 0 or n_t == tq <= CHUNK)
    assert past_len == 0 or (n_t == tq and past_len % tk_past == 0)
    m_rows = MLA_HEADS * tq
    assert m_rows % group_rows == 0 and (group_rows % tq == 0 or tq % group_rows == 0)
    assert past_len or (tq % tk == 0 and tk % CHUNK == 0)
    scratch = [pltpu.VMEM((m_rows, LANES), F32), pltpu.VMEM((m_rows, KEY_DIM), F32)]
    scratch.append(pltpu.VMEM((tk_past, KEY_DIM), BF16) if past_len else pltpu.VMEM((m_rows, tk), F32))
    args = [q, k_new]
    specs = [pl.BlockSpec((1, MLA_HEADS, tq, KEY_DIM), lambda b, i: (b, 0, i, 0)),
             pl.BlockSpec((1, n_t, KEY_DIM), lambda b, i: (b, 0, 0), pipeline_mode=pl.Buffered(1))]
    if past_len:
        args += list(past)
        specs += [pl.BlockSpec((None, 1, past_len, c.shape[-1]), lambda b, i: (layer, b, 0, 0))
                  for c in past]
    args.append(w_uv)
    specs.append(_layer_spec(w_uv, layer))
    return pl.pallas_call(
        functools.partial(_attn_kernel, tq=tq, tk=tk, tk_past=tk_past, past_len=past_len,
                          group_rows=group_rows),
        grid=(n_b, n_t // tq),
        in_specs=specs,
        out_specs=pl.BlockSpec((1, tq, MLA_HEADS * V_DIM), lambda b, i: (b, i, 0)),
        out_shape=jax.ShapeDtypeStruct((n_b, n_t, MLA_HEADS * V_DIM), BF16),
        scratch_shapes=scratch,
        compiler_params=pltpu.CompilerParams(
            dimension_semantics=("arbitrary", "arbitrary"), vmem_limit_bytes=VMEM_LIMIT),
        name="attn",
    )(*args)


def _prep_weights(w_ffn1_gu, w_ffn1_down, norm_ffn1, norm_mix, w_in, q_norm, w_uq, kv_norm, w_ukv,
                  conv_w, pool_w, pool_scale, w_out, norm_ffn2, w_ffn2_gu, w_ffn2_down):
    n_l = w_in.shape[0]
    half = ROPE_DIM // 2
    packed = lambda w: jnp.concatenate([w, w[..., half:], w[..., :half]], axis=-1)
    vec = lambda v: v.reshape(n_l, 1, v.shape[-1])
    cq, ckv, kr, rest = (w_in[..., :Q_RANK], w_in[..., Q_RANK:Q_RANK + KV_RANK],
                         w_in[..., Q_RANK + KV_RANK:Q_RANK + KV_RANK + ROPE_DIM],
                         w_in[..., Q_RANK + KV_RANK + ROPE_DIM:])
    w_in_p = jnp.concatenate([cq, ckv, packed(kr), rest], axis=-1).astype(BF16)

    wq = w_uq.reshape(n_l, Q_RANK, MLA_HEADS, NOPE_DIM + ROPE_DIM)
    nope = wq[..., :NOPE_DIM].reshape(n_l, Q_RANK, MLA_HEADS * NOPE_DIM)
    rope_cols = [wq[:, :, hd, NOPE_DIM:] for hd in range(MLA_HEADS)]
    w_uq_p = jnp.concatenate([nope] + [packed(r) for r in rope_cols], axis=-1).astype(BF16)

    wkv = w_ukv.reshape(n_l, KV_RANK, MLA_HEADS, NOPE_DIM + V_DIM)
    w_ukt = jnp.transpose(wkv[..., :NOPE_DIM], (0, 2, 3, 1)).astype(BF16)
    w_uv = jnp.transpose(wkv[..., NOPE_DIM:], (0, 2, 1, 3)).astype(BF16)

    eye = jnp.eye(POOL_GROUPS, dtype=F32)
    pool_bd = (pool_w[:, :, :, None, :] * eye[None, :, None, :, None]).reshape(
        n_l, POOL_DIM, POOL_DIM).astype(BF16)

    return {
        "norm_ffn1": vec(norm_ffn1), "w_ffn1_gu": w_ffn1_gu.astype(BF16),
        "w_ffn1_down": w_ffn1_down.astype(BF16),
        "norm_mix": vec(norm_mix), "w_in": w_in_p, "q_norm": vec(q_norm), "w_uq": w_uq_p,
        "kv_norm": vec(kv_norm), "w_ukt": w_ukt, "w_uv": w_uv,
        "conv_w": conv_w, "pool_bd": pool_bd, "pool_scale": vec(pool_scale),
        "w_out": w_out.astype(BF16),
        "norm_ffn2": vec(norm_ffn2), "w_ffn2_gu": w_ffn2_gu.astype(BF16),
        "w_ffn2_down": w_ffn2_down.astype(BF16),
    }


def _trunk_layer(h, mod, layer, lw, tables, conv_state, pool_state, past, norm_final, *, bt, tt, tm,
                 tq, group_rows, pos0):
    h1 = _ffn(h, mod, layer, lw["norm_ffn1"], lw["w_ffn1_gu"], lw["w_ffn1_down"], bt=bt, tt=tt)
    q, k_new, ckv, kr, sp, conv_new, pool_new = _mix_in(
        h1, mod, layer, lw, tables[0], tables[1], conv_state, pool_state, tm=tm, pos0=pos0)
    a = _attn(q, k_new, past, layer, lw["w_uv"], tq=tq, tk=min(tq, 512), tk_past=2048,
              group_rows=group_rows)
    h3 = _ffn(h1, mod, layer, lw["norm_ffn2"], lw["w_ffn2_gu"], lw["w_ffn2_down"], bt=bt, tt=tt,
              mix=(a, sp, lw["w_out"]), norm_final=norm_final)
    return h3, ckv, kr, conv_new, pool_new


def kernel(x_prompt, x_sample, c_prompt, c_sample, cache_ckv, cache_krope, state_conv, state_pool, w_ada, b_ada, norm_ffn1, w_ffn1_gu, w_ffn1_down, norm_mix, w_in, q_norm, w_uq, kv_norm, w_ukv, conv_w, pool_w, pool_scale, w_out, norm_ffn2, w_ffn2_gu, w_ffn2_down, norm_final):
    bp, seq, _ = x_prompt.shape
    bs, dec_seq, _ = x_sample.shape
    past_len = cache_ckv.shape[2]
    assert bp + bs <= ADA_ROWS

    c_all = jnp.concatenate(
        [c_prompt, c_sample, jnp.zeros((ADA_ROWS - bp - bs, D_MODEL), F32)], axis=0)
    mods = _ada(c_all, w_ada, b_ada).reshape(DEPTH, ADA_ROWS, N_ADA, D_MODEL)

    tm_p = 512
    tables_p = _rope_tables(seq, 0, tm_p)
    tables_s = _rope_tables(dec_seq, past_len, dec_seq)
    lw = _prep_weights(w_ffn1_gu, w_ffn1_down, norm_ffn1, norm_mix, w_in, q_norm, w_uq, kv_norm,
                       w_ukv, conv_w, pool_w, pool_scale, w_out, norm_ffn2, w_ffn2_gu, w_ffn2_down)

    hp, hs = x_prompt, x_sample
    outs_p, outs_s = [], []
    zero_conv = jnp.zeros((bp, CONV_W - 1, CONV_DIM), F32)
    zero_pool = jnp.zeros((bp, POOL_MAX - 1, POOL_DIM), F32)
    for l in range(DEPTH):
        nf = norm_final if l == DEPTH - 1 else None
        hp, *op = _trunk_layer(hp, mods[l, :bp], l, lw, tables_p, zero_conv, zero_pool, None, nf,
                               bt=1, tt=tm_p, tm=tm_p, tq=1024, group_rows=512, pos0=0)
        hs, *os_ = _trunk_layer(hs, mods[l, bp:bp + bs], l, lw, tables_s, state_conv[l], state_pool[l],
                                (cache_ckv, cache_krope), nf, bt=bs, tt=dec_seq, tm=dec_seq, tq=dec_seq,
                                group_rows=MLA_HEADS * dec_seq, pos0=past_len)
        outs_p.append(op)
        outs_s.append(os_)

    stack = lambda outs, k: jnp.stack([o[k] for o in outs])
    return (hp, hs,
            stack(outs_p, 0), stack(outs_p, 1), stack(outs_p, 2), stack(outs_p, 3),
            stack(outs_s, 0), stack(outs_s, 1), stack(outs_s, 2), stack(outs_s, 3))
```

```python
import functools

import jax
import jax.numpy as jnp
import numpy as np
from jax import lax
from jax.experimental import pallas as pl
from jax.experimental.pallas import tpu as pltpu

F32 = jnp.float32
BF16 = jnp.bfloat16

D_MODEL = 1024
DEPTH = 2
CHUNK = 64
EPS = 1e-6
N_ADA = 9
D_FF = 2816
MLA_HEADS = 4
Q_RANK = 256
KV_RANK = 128
NOPE_DIM = 128
ROPE_DIM = 64
V_DIM = 128
ROPE_BASE = 10000.0
ATTN_SCALE = (NOPE_DIM + ROPE_DIM) ** -0.5
LOG2_E = 1.4426950408889634
CONV_DIM = 256
CONV_W = 3
POOL_WINDOWS = (2, 4, 8, 16)
POOL_GROUPS = 4
POOL_DIM = 256
POOL_GROUP_DIM = POOL_DIM // POOL_GROUPS
POOL_MAX = 16

LANES = 128
KEY_DIM = 2 * LANES
FF_CHUNK = 256
ADA_ROWS = 16
ADA_COLS = 1024
HALO = 16
NEG_BIG = -1e30
VMEM_LIMIT = 56 * 1024 * 1024

IN_CQ, IN_CKV, IN_KR, IN_CB, IN_CC, IN_CV, IN_PU, IN_END = 0, 256, 384, 512, 768, 1024, 1280, 1536
UQ_NOPE, UQ_ROPE, UQ_END = 0, 512, 1024


def _rms(x, g):
    return x * lax.rsqrt(jnp.mean(x * x, axis=-1, keepdims=True) + EPS) * g


def _silu(x):
    return x * jax.nn.sigmoid(x)


def _dot(a, b):
    return jnp.dot(a, b, preferred_element_type=F32)


def _const_spec(shape):
    nd = len(shape)
    return pl.BlockSpec(shape, lambda *_: (0,) * nd, pipeline_mode=pl.Buffered(1))


def _layer_spec(stacked, layer):
    rest = stacked.shape[1:]
    return pl.BlockSpec((None,) + rest, lambda *_: (layer,) + (0,) * len(rest),
                        pipeline_mode=pl.Buffered(1))


def _ada_kernel(c_ref, w_ref, b_ref, o_ref):
    a = _silu(c_ref[...]).astype(BF16)
    o_ref[0] = _dot(a, w_ref[0].astype(BF16)) + b_ref[0]


def _ada(c_all, w_ada, b_ada):
    n_l, _, n_cols = w_ada.shape
    return pl.pallas_call(
        _ada_kernel,
        grid=(n_l, n_cols // ADA_COLS),
        in_specs=[
            pl.BlockSpec((ADA_ROWS, D_MODEL), lambda l, j: (0, 0)),
            pl.BlockSpec((1, D_MODEL, ADA_COLS), lambda l, j: (l, 0, j)),
            pl.BlockSpec((1, 1, ADA_COLS), lambda l, j: (l, 0, j)),
        ],
        out_specs=pl.BlockSpec((1, ADA_ROWS, ADA_COLS), lambda l, j: (l, 0, j)),
        out_shape=jax.ShapeDtypeStruct((n_l, ADA_ROWS, n_cols), F32),
        compiler_params=pltpu.CompilerParams(
            dimension_semantics=("arbitrary", "arbitrary"), vmem_limit_bytes=VMEM_LIMIT),
        name="ada",
    )(c_all, w_ada, b_ada.reshape(n_l, 1, n_cols))


def _rope_table_kernel(inv_ref, sign_ref, cos_ref, sin_ref, cos_row_ref, sin_row_ref, *, pos0, tm):
    t = pl.program_id(0)

    @pl.when(t == 0)
    def _():
        row = lax.broadcasted_iota(jnp.int32, (tm, LANES), 0).astype(F32)
        ang = row * inv_ref[...]
        cos_row_ref[...] = jnp.cos(ang)
        sin_row_ref[...] = jnp.sin(ang)

    start = (pos0 + t * tm).astype(F32) * inv_ref[...]
    cos_s, sin_s = jnp.cos(start), jnp.sin(start)
    cos_r, sin_r = cos_row_ref[...], sin_row_ref[...]
    cos_ref[...] = cos_s * cos_r - sin_s * sin_r
    sin_ref[...] = (sin_s * cos_r + cos_s * sin_r) * sign_ref[...]


def _rope_tables(n_rows, pos0, tm):
    half = ROPE_DIM // 2
    inv = ROPE_BASE ** (-jnp.arange(half, dtype=F32) / half)
    inv = jnp.tile(inv, LANES // half).reshape(1, LANES)
    sign = jnp.tile(jnp.concatenate([-jnp.ones((half,), F32), jnp.ones((half,), F32)]),
                    LANES // ROPE_DIM).reshape(1, LANES)
    return pl.pallas_call(
        functools.partial(_rope_table_kernel, pos0=pos0, tm=tm),
        grid=(n_rows // tm,),
        in_specs=[_const_spec((1, LANES)), _const_spec((1, LANES))],
        out_specs=[pl.BlockSpec((tm, LANES), lambda i: (i, 0))] * 2,
        out_shape=[jax.ShapeDtypeStruct((n_rows, LANES), F32)] * 2,
        scratch_shapes=[pltpu.VMEM((tm, LANES), F32)] * 2,
        compiler_params=pltpu.CompilerParams(dimension_semantics=("arbitrary",)),
        name="rope_tables",
    )(inv, sign)


def _ffn_kernel(*refs, has_mix, final_norm):
    refs = list(refs)
    h_ref = refs.pop(0)
    if has_mix:
        a_ref, sp_ref, wo_ref = refs.pop(0), refs.pop(0), refs.pop(0)
    mod_ref, ng_ref, wgu_ref, wd_ref = refs.pop(0), refs.pop(0), refs.pop(0), refs.pop(0)
    if final_norm:
        nf_ref = refs.pop(0)
    (o_ref,) = refs

    bt, tt, d = h_ref.shape
    m = bt * tt
    x = h_ref[...]
    mod = mod_ref[...]
    row0 = 0
    if has_mix:
        mix_w = a_ref.shape[-1]
        mix = _dot(a_ref[...].reshape(m, mix_w), wo_ref[0:mix_w, :])
        mix = mix + _dot(sp_ref[...].reshape(m, sp_ref.shape[-1]), wo_ref[mix_w:, :])
        x = x + mod[:, 5:6, :] * mix.reshape(bt, tt, d)
        row0 = 6
    shift, scale, gate = mod[:, row0:row0 + 1, :], mod[:, row0 + 1:row0 + 2, :], mod[:, row0 + 2:row0 + 3, :]
    n = _rms(x, ng_ref[...]) * (1.0 + scale) + shift
    nb = n.reshape(m, d).astype(BF16)
    acc = jnp.zeros((m, d), F32)
    for c in range(D_FF // FF_CHUNK):
        g = _dot(nb, wgu_ref[:, c * FF_CHUNK:(c + 1) * FF_CHUNK])
        u = _dot(nb, wgu_ref[:, D_FF + c * FF_CHUNK:D_FF + (c + 1) * FF_CHUNK])
        act = (_silu(g) * u).astype(BF16)
        acc = acc + _dot(act, wd_ref[c * FF_CHUNK:(c + 1) * FF_CHUNK, :])
    y = x + 0.5 * gate * acc.reshape(bt, tt, d)
    if final_norm:
        y = _rms(y, nf_ref[...])
    o_ref[...] = y


def _ffn(h, mod, layer, norm_g, w_gu, w_down, *, bt, tt, mix=None, norm_final=None):
    n_b, n_t, d = h.shape
    grid = (n_b // bt, n_t // tt)
    tile = lambda w: pl.BlockSpec((bt, tt, w), lambda b, t: (b, t, 0))
    args, specs = [h], [tile(d)]
    if mix is not None:
        a, sp, w_out = mix
        args += [a, sp, w_out]
        specs += [tile(a.shape[-1]), tile(sp.shape[-1]), _layer_spec(w_out, layer)]
    args += [mod, norm_g, w_gu, w_down]
    specs += [pl.BlockSpec((bt, N_ADA, d), lambda b, t: (b, 0, 0)), _layer_spec(norm_g, layer),
              _layer_spec(w_gu, layer), _layer_spec(w_down, layer)]
    if norm_final is not None:
        args.append(norm_final.reshape(1, d))
        specs.append(_const_spec((1, d)))
    return pl.pallas_call(
        functools.partial(_ffn_kernel, has_mix=mix is not None, final_norm=norm_final is not None),
        grid=grid,
        in_specs=specs,
        out_specs=tile(d),
        out_shape=jax.ShapeDtypeStruct(h.shape, F32),
        compiler_params=pltpu.CompilerParams(
            dimension_semantics=("arbitrary", "arbitrary"), vmem_limit_bytes=VMEM_LIMIT),
        name="ffn_mix" if mix is not None else "ffn",
    )(*args)


def _mix_in_kernel(h_ref, mod_ref, ng_ref, wina_ref, winb_ref, qn_ref, wuq_ref, kvn_ref, wukt_ref,
                   convw_ref, poolw_ref, pools_ref, cos_ref, sin_ref, cstate_ref, pstate_ref,
                   q_ref, k_ref, ckv_ref, kr_ref, sp_ref, cnew_ref, pnew_ref,
                   eu_ref, ep_ref, s2_ref, s4_ref, s8_ref, *, tm, pos0, kr_transposed):
    t = pl.program_id(1)
    x = h_ref[0]
    mod = mod_ref[0]
    n = _rms(x, ng_ref[...]) * (1.0 + mod[4:5, :]) + mod[3:4, :]
    nb = n.astype(BF16)
    proj = jnp.concatenate([_dot(nb, wina_ref[...]), _dot(nb, winb_ref[...])], axis=-1)
    cos = cos_ref[...]
    sin = sin_ref[...]
    lane = lax.broadcasted_iota(jnp.int32, (tm, LANES), 1)

    def rope(packed):
        return packed * cos + pltpu.roll(packed, ROPE_DIM, axis=1) * sin

    qn = _rms(proj[:, IN_CQ:IN_CKV], qn_ref[...]).astype(BF16)
    qall = _dot(qn, wuq_ref[...])
    for hd in range(MLA_HEADS):
        q_nope = qall[:, UQ_NOPE + hd * NOPE_DIM:UQ_NOPE + (hd + 1) * NOPE_DIM].astype(BF16)
        q_lat = _dot(q_nope, wukt_ref[hd])
        q_rope = jnp.where(lane < ROPE_DIM, rope(qall[:, UQ_ROPE + hd * LANES:UQ_ROPE + (hd + 1) * LANES]), 0.0)
        q_ref[0, hd] = (jnp.concatenate([q_lat, q_rope], axis=-1) * (ATTN_SCALE * LOG2_E)).astype(BF16)

    ckv = _rms(proj[:, IN_CKV:IN_KR], kvn_ref[...])
    kr = rope(proj[:, IN_KR:IN_CB])
    ckv_ref[0] = ckv
    kr_ref[0] = jnp.transpose(kr)[:ROPE_DIM, :] if kr_transposed else kr[:, :ROPE_DIM]
    k_ref[0] = jnp.concatenate([ckv, jnp.where(lane < ROPE_DIM, kr, 1.0)], axis=-1).astype(BF16)

    @pl.when(t == 0)
    def _():
        eu_ref[0:HALO - (CONV_W - 1), :] = jnp.zeros((HALO - (CONV_W - 1), CONV_DIM), F32)
        eu_ref[HALO - (CONV_W - 1):HALO, :] = cstate_ref[0]
        ep_ref[0:1, :] = jnp.zeros((1, POOL_DIM), F32)
        ep_ref[1:HALO, :] = pstate_ref[0]

    u = proj[:, IN_CC:IN_CV] * proj[:, IN_CV:IN_PU]
    eu_ref[HALO:HALO + tm, :] = u
    cw = convw_ref[...]
    y = u * cw[CONV_W - 1:CONV_W, :]
    for j in range(CONV_W - 1):
        back = CONV_W - 1 - j
        y = y + eu_ref[HALO - back:HALO - back + tm, :] * cw[j:j + 1, :]
    sp_ref[0, :, 0:CONV_DIM] = (proj[:, IN_CB:IN_CC] * y).astype(BF16)
    cnew_ref[0] = eu_ref[HALO + tm - (CONV_W - 1):HALO + tm, :]
    eu_ref[0:HALO, :] = eu_ref[tm:tm + HALO, :]

    pu = proj[:, IN_PU:IN_END]
    ep_ref[HALO:HALO + tm, :] = pu
    n_ext = tm + HALO
    s2_ref[1:n_ext, :] = ep_ref[1:n_ext, :] + ep_ref[0:n_ext - 1, :]
    s4_ref[3:n_ext, :] = s2_ref[3:n_ext, :] + s2_ref[1:n_ext - 2, :]
    s8_ref[7:n_ext, :] = s4_ref[7:n_ext, :] + s4_ref[3:n_ext - 4, :]
    s16 = s8_ref[HALO:n_ext, :] + s8_ref[HALO - 8:n_ext - 8, :]
    lane_grp = lax.broadcasted_iota(jnp.int32, (tm, POOL_DIM), 1) >> (POOL_GROUP_DIM.bit_length() - 1)
    wsum = jnp.where(lane_grp == 0, s2_ref[HALO:n_ext, :],
                     jnp.where(lane_grp == 1, s4_ref[HALO:n_ext, :],
                               jnp.where(lane_grp == 2, s8_ref[HALO:n_ext, :], s16)))
    win = jnp.left_shift(2, lane_grp)
    pos = pos0 + t * tm + lax.broadcasted_iota(jnp.int32, (tm, POOL_DIM), 0)
    cnt = jnp.minimum(win, pos + 1).astype(F32)
    dlt = (wsum / cnt - pu).astype(BF16)
    sp_ref[0, :, CONV_DIM:] = (_dot(dlt, poolw_ref[...]) * pools_ref[...]).astype(BF16)
    pnew_ref[0] = ep_ref[tm + 1:tm + HALO, :]
    ep_ref[0:HALO, :] = ep_ref[tm:tm + HALO, :]


def _mix_in(h, mod, layer, lw, cos_t, sin_t, conv_state, pool_state, *, tm, pos0, kr_transposed):
    n_b, n_t, d = h.shape
    grid = (n_b, n_t // tm)
    rows = lambda w: pl.BlockSpec((1, tm, w), lambda b, t: (b, t, 0))
    per_b = lambda r, w: pl.BlockSpec((1, r, w), lambda b, t: (b, 0, 0))
    weights = [lw[k] for k in ("norm_mix", "w_in_a", "w_in_b", "q_norm", "w_uq", "kv_norm", "w_ukt",
                               "conv_w", "pool_bd", "pool_scale")]
    kr_spec = pl.BlockSpec((1, ROPE_DIM, tm), lambda b, t: (b, 0, t)) if kr_transposed else rows(ROPE_DIM)
    kr_shape = (n_b, ROPE_DIM, n_t) if kr_transposed else (n_b, n_t, ROPE_DIM)
    in_specs = [rows(d), per_b(N_ADA, d)] + [_layer_spec(w, layer) for w in weights] + [
        pl.BlockSpec((tm, LANES), lambda b, t: (t, 0)), pl.BlockSpec((tm, LANES), lambda b, t: (t, 0)),
        per_b(CONV_W - 1, CONV_DIM), per_b(POOL_MAX - 1, POOL_DIM),
    ]
    out_specs = [
        pl.BlockSpec((1, MLA_HEADS, tm, KEY_DIM), lambda b, t: (b, 0, t, 0)),
        rows(KEY_DIM), rows(KV_RANK), kr_spec, rows(CONV_DIM + POOL_DIM),
        per_b(CONV_W - 1, CONV_DIM), per_b(POOL_MAX - 1, POOL_DIM),
    ]
    out_shape = [
        jax.ShapeDtypeStruct((n_b, MLA_HEADS, n_t, KEY_DIM), BF16),
        jax.ShapeDtypeStruct((n_b, n_t, KEY_DIM), BF16),
        jax.ShapeDtypeStruct((n_b, n_t, KV_RANK), F32),
        jax.ShapeDtypeStruct(kr_shape, F32),
        jax.ShapeDtypeStruct((n_b, n_t, CONV_DIM + POOL_DIM), BF16),
        jax.ShapeDtypeStruct((n_b, CONV_W - 1, CONV_DIM), F32),
        jax.ShapeDtypeStruct((n_b, POOL_MAX - 1, POOL_DIM), F32),
    ]
    ext = pltpu.VMEM((tm + HALO, POOL_DIM), F32)
    return pl.pallas_call(
        functools.partial(_mix_in_kernel, tm=tm, pos0=pos0, kr_transposed=kr_transposed),
        grid=grid,
        in_specs=in_specs,
        out_specs=out_specs,
        out_shape=out_shape,
        scratch_shapes=[ext, ext, ext, ext, ext],
        compiler_params=pltpu.CompilerParams(
            dimension_semantics=("arbitrary", "arbitrary"), vmem_limit_bytes=VMEM_LIMIT),
        name="mix_in",
    )(h, mod, *weights, cos_t, sin_t, conv_state, pool_state)


def _attn_kernel(*refs, tq, tk, tk_past, past_len, group_rows):
    refs = list(refs)
    q_ref, knew_ref = refs.pop(0), refs.pop(0)
    if past_len:
        ckv_past_ref, krt_past_ref = refs.pop(0), refs.pop(0)
    wuv_ref, o_ref, m_ref, acc_ref = refs[:4]
    s_ref = None if past_len else refs[4]

    i = pl.program_id(1)
    m_rows = MLA_HEADS * tq
    n_groups = m_rows // group_rows
    m_ref[...] = jnp.full(m_ref.shape, NEG_BIG, F32)
    acc_ref[...] = jnp.zeros(acc_ref.shape, F32)

    def lane_tile(x, width):
        return x[:, :width] if width < LANES else jnp.concatenate([x] * (width // LANES), axis=1)

    def group(g):
        return slice(g * group_rows, (g + 1) * group_rows)

    def scores(g, kblk):
        if group_rows >= tq:
            heads = group_rows // tq
            q = q_ref[0, g * heads:(g + 1) * heads].reshape(group_rows, KEY_DIM)
        else:
            per_head = tq // group_rows
            r0 = (g % per_head) * group_rows
            q = q_ref[0, g // per_head, r0:r0 + group_rows, :]
        return lax.dot_general(q, kblk, (((1,), (1,)), ((), ())), preferred_element_type=F32)

    def absorb(g, s, kblk, mask):
        width = kblk.shape[0]
        rows = group(g)
        if mask is not None:
            s = jnp.where(mask, s, NEG_BIG)
        m_prev = m_ref[rows, :]
        m_new = jnp.maximum(m_prev, jnp.max(s, axis=1, keepdims=True))
        alpha = jnp.exp2(m_prev - m_new)
        p = jnp.exp2(s - lane_tile(m_new, width)).astype(BF16)
        acc_ref[rows, :] = lane_tile(alpha, KEY_DIM) * acc_ref[rows, :] + _dot(p, kblk)
        m_ref[rows, :] = m_new

    chunk_shift = CHUNK.bit_length() - 1

    def diag_mask(g, col0, width):
        r0, n_r = ((g * group_rows) % tq, group_rows) if group_rows < tq else (0, tq)
        if (col0 + width - 1) >> chunk_shift <= r0 >> chunk_shift:
            return "all"
        if col0 >> chunk_shift > (r0 + n_r - 1) >> chunk_shift:
            return "none"
        row = (lax.broadcasted_iota(jnp.int32, (group_rows, width), 0) + r0) & (tq - 1)
        col = lax.broadcasted_iota(jnp.int32, (group_rows, width), 1) + col0
        return (col >> chunk_shift) <= (row >> chunk_shift)

    def mask_arg(vis):
        return None if isinstance(vis, str) else vis

    def any_visible(vis):
        return not (isinstance(vis, str) and vis == "none")

    if past_len:
        def past_body(j, carry):
            start = pl.multiple_of(j * tk_past, tk_past)
            lat = ckv_past_ref[0, pl.ds(start, tk_past), :].astype(BF16)
            rot_t = krt_past_ref[0, :, pl.ds(start, tk_past)].astype(BF16)
            for g in range(n_groups):
                rows = group(g)
                q = q_ref[0].reshape(m_rows, KEY_DIM)[rows]
                s = lax.dot_general(q[:, :KV_RANK], lat, (((1,), (1,)), ((), ())), preferred_element_type=F32)
                s = s + _dot(q[:, KV_RANK:KV_RANK + ROPE_DIM], rot_t)
                m_prev = m_ref[rows, :]
                m_new = jnp.maximum(m_prev, jnp.max(s, axis=1, keepdims=True))
                alpha = jnp.exp2(m_prev - m_new)
                p = jnp.exp2(s - lane_tile(m_new, tk_past))
                denom = jnp.broadcast_to(jnp.sum(p, axis=1, keepdims=True), (group_rows, KEY_DIM - KV_RANK))
                acc_ref[rows, :] = lane_tile(alpha, KEY_DIM) * acc_ref[rows, :] + jnp.concatenate(
                    [_dot(p.astype(BF16), lat), denom], axis=-1)
                m_ref[rows, :] = m_new
            return carry
        lax.fori_loop(0, past_len // tk_past, past_body, 0)
        kblk = knew_ref[0]
        for g in range(n_groups):
            absorb(g, scores(g, kblk), kblk, mask_arg(diag_mask(g, 0, tq)))
    else:
        def new_block(j):
            return knew_ref[0, pl.ds(pl.multiple_of(j * tk, tk), tk), :]

        n_diag = tq // tk
        n_full = i * n_diag
        first = new_block(0)
        for g in range(n_groups):
            s_ref[group(g), :] = scores(g, first)

        def body(j, carry):
            kblk, knext = new_block(j), new_block(j + 1)
            for g in range(n_groups):
                s = s_ref[group(g), :]
                s_ref[group(g), :] = scores(g, knext)
                absorb(g, s, kblk, None)
            return carry
        lax.fori_loop(0, n_full, body, 0)

        for d in range(n_diag):
            kblk = new_block(n_full + d)
            knext = new_block(n_full + d + 1) if d + 1 < n_diag else None
            for g in range(n_groups):
                vis = diag_mask(g, d * tk, tk)
                if any_visible(vis):
                    s = s_ref[group(g), :]
                if knext is not None and any_visible(diag_mask(g, (d + 1) * tk, tk)):
                    s_ref[group(g), :] = scores(g, knext)
                if any_visible(vis):
                    absorb(g, s, kblk, mask_arg(vis))

    heads = []
    for hd in range(MLA_HEADS):
        acc = acc_ref[hd * tq:(hd + 1) * tq, :]
        denom = acc[:, KV_RANK + ROPE_DIM:KV_RANK + ROPE_DIM + 1]
        heads.append(_dot((acc[:, :KV_RANK] / denom).astype(BF16), wuv_ref[hd]))
    o_ref[0] = jnp.concatenate(heads, axis=-1).astype(BF16)


def _attn(q, k_new, past, layer, w_uv, *, tq, tk, tk_past, group_rows):
    n_b, _, n_t, _ = q.shape
    past_len = 0 if past is None else past[0].shape[2]
    assert (past_len % CHUNK == 0) and (tq % CHUNK == 0 or n_t == tq <= CHUNK)
    assert past_len == 0 or (n_t == tq and past_len % tk_past == 0)
    m_rows = MLA_HEADS * tq
    assert m_rows % group_rows == 0 and (group_rows % tq == 0 or tq % group_rows == 0)
    assert past_len or (tq % tk == 0 and tk % CHUNK == 0)
    scratch = [pltpu.VMEM((m_rows, LANES), F32), pltpu.VMEM((m_rows, KEY_DIM), F32)]
    if not past_len:
        scratch.append(pltpu.VMEM((m_rows, tk), F32))
    args = [q, k_new]
    specs = [pl.BlockSpec((1, MLA_HEADS, tq, KEY_DIM), lambda b, i: (b, 0, i, 0)),
             pl.BlockSpec((1, n_t, KEY_DIM), lambda b, i: (b, 0, 0), pipeline_mode=pl.Buffered(1))]
    if past_len:
        args += list(past)
        specs += [pl.BlockSpec((None, 1) + c.shape[2:], lambda b, i: (layer, b, 0, 0)) for c in past]
    args.append(w_uv)
    specs.append(_layer_spec(w_uv, layer))
    return pl.pallas_call(
        functools.partial(_attn_kernel, tq=tq, tk=tk, tk_past=tk_past, past_len=past_len,
                          group_rows=group_rows),
        grid=(n_b, n_t // tq),
        in_specs=specs,
        out_specs=pl.BlockSpec((1, tq, MLA_HEADS * V_DIM), lambda b, i: (b, i, 0)),
        out_shape=jax.ShapeDtypeStruct((n_b, n_t, MLA_HEADS * V_DIM), BF16),
        scratch_shapes=scratch,
        compiler_params=pltpu.CompilerParams(
            dimension_semantics=("arbitrary", "arbitrary"), vmem_limit_bytes=VMEM_LIMIT),
        name="attn",
    )(*args)


def _prep_weights(w_ffn1_gu, w_ffn1_down, norm_ffn1, norm_mix, w_in, q_norm, w_uq, kv_norm, w_ukv,
                  conv_w, pool_w, pool_scale, w_out, norm_ffn2, w_ffn2_gu, w_ffn2_down):
    n_l = w_in.shape[0]
    half = ROPE_DIM // 2
    packed = lambda w: jnp.concatenate([w, w[..., half:], w[..., :half]], axis=-1)
    vec = lambda v: v.reshape(n_l, 1, v.shape[-1])
    cq, ckv, kr, rest = (w_in[..., :Q_RANK], w_in[..., Q_RANK:Q_RANK + KV_RANK],
                         w_in[..., Q_RANK + KV_RANK:Q_RANK + KV_RANK + ROPE_DIM],
                         w_in[..., Q_RANK + KV_RANK + ROPE_DIM:])
    w_in_a = jnp.concatenate([cq, ckv, packed(kr)], axis=-1).astype(BF16)
    w_in_b = rest.astype(BF16)

    wq = w_uq.reshape(n_l, Q_RANK, MLA_HEADS, NOPE_DIM + ROPE_DIM)
    nope = wq[..., :NOPE_DIM].reshape(n_l, Q_RANK, MLA_HEADS * NOPE_DIM)
    rope_cols = [wq[:, :, hd, NOPE_DIM:] for hd in range(MLA_HEADS)]
    w_uq_p = jnp.concatenate([nope] + [packed(r) for r in rope_cols], axis=-1).astype(BF16)

    wkv = w_ukv.reshape(n_l, KV_RANK, MLA_HEADS, NOPE_DIM + V_DIM)
    w_ukt = jnp.transpose(wkv[..., :NOPE_DIM], (0, 2, 3, 1)).astype(BF16)
    w_uv = jnp.transpose(wkv[..., NOPE_DIM:], (0, 2, 1, 3)).astype(BF16)

    eye = jnp.eye(POOL_GROUPS, dtype=F32)
    pool_bd = (pool_w[:, :, :, None, :] * eye[None, :, None, :, None]).reshape(
        n_l, POOL_DIM, POOL_DIM).astype(BF16)

    return {
        "norm_ffn1": vec(norm_ffn1), "w_ffn1_gu": w_ffn1_gu.astype(BF16),
        "w_ffn1_down": w_ffn1_down.astype(BF16),
        "norm_mix": vec(norm_mix), "w_in_a": w_in_a, "w_in_b": w_in_b, "q_norm": vec(q_norm),
        "w_uq": w_uq_p,
        "kv_norm": vec(kv_norm), "w_ukt": w_ukt, "w_uv": w_uv,
        "conv_w": conv_w, "pool_bd": pool_bd, "pool_scale": vec(pool_scale),
        "w_out": w_out.astype(BF16),
        "norm_ffn2": vec(norm_ffn2), "w_ffn2_gu": w_ffn2_gu.astype(BF16),
        "w_ffn2_down": w_ffn2_down.astype(BF16),
    }


def _trunk_layer(h, mod, layer, lw, tables, conv_state, pool_state, past, norm_final, *, bt, tt, tm,
                 tq, group_rows, pos0):
    h1 = _ffn(h, mod, layer, lw["norm_ffn1"], lw["w_ffn1_gu"], lw["w_ffn1_down"], bt=bt, tt=tt)
    q, k_new, ckv, kr, sp, conv_new, pool_new = _mix_in(
        h1, mod, layer, lw, tables[0], tables[1], conv_state, pool_state, tm=tm, pos0=pos0,
        kr_transposed=past is None)
    a = _attn(q, k_new, past, layer, lw["w_uv"], tq=tq, tk=min(tq, 512), tk_past=2048,
              group_rows=group_rows)
    h3 = _ffn(h1, mod, layer, lw["norm_ffn2"], lw["w_ffn2_gu"], lw["w_ffn2_down"], bt=bt, tt=tt,
              mix=(a, sp, lw["w_out"]), norm_final=norm_final)
    return h3, ckv, kr, conv_new, pool_new


def kernel(x_prompt, x_sample, c_prompt, c_sample, cache_ckv, cache_krope, state_conv, state_pool, w_ada, b_ada, norm_ffn1, w_ffn1_gu, w_ffn1_down, norm_mix, w_in, q_norm, w_uq, kv_norm, w_ukv, conv_w, pool_w, pool_scale, w_out, norm_ffn2, w_ffn2_gu, w_ffn2_down, norm_final):
    bp, seq, _ = x_prompt.shape
    bs, dec_seq, _ = x_sample.shape
    past_len = cache_ckv.shape[2]
    assert bp + bs <= ADA_ROWS

    c_all = jnp.concatenate(
        [c_prompt, c_sample, jnp.zeros((ADA_ROWS - bp - bs, D_MODEL), F32)], axis=0)
    mods = _ada(c_all, w_ada, b_ada).reshape(DEPTH, ADA_ROWS, N_ADA, D_MODEL)

    tm_p = 512
    tables_p = _rope_tables(seq, 0, tm_p)
    tables_s = _rope_tables(dec_seq, past_len, dec_seq)
    lw = _prep_weights(w_ffn1_gu, w_ffn1_down, norm_ffn1, norm_mix, w_in, q_norm, w_uq, kv_norm,
                       w_ukv, conv_w, pool_w, pool_scale, w_out, norm_ffn2, w_ffn2_gu, w_ffn2_down)

    hp, hs = x_prompt, x_sample
    outs_p, outs_s = [], []
    zero_conv = jnp.zeros((bp, CONV_W - 1, CONV_DIM), F32)
    zero_pool = jnp.zeros((bp, POOL_MAX - 1, POOL_DIM), F32)
    for l in range(DEPTH):
        nf = norm_final if l == DEPTH - 1 else None
        hp, *op = _trunk_layer(hp, mods[l, :bp], l, lw, tables_p, zero_conv, zero_pool, None, nf,
                               bt=1, tt=tm_p, tm=tm_p, tq=1024, group_rows=512, pos0=0)
        hs, *os_ = _trunk_layer(hs, mods[l, bp:bp + bs], l, lw, tables_s, state_conv[l], state_pool[l],
                                (cache_ckv, jnp.swapaxes(cache_krope, 2, 3)), nf, bt=bs, tt=dec_seq, tm=dec_seq, tq=dec_seq,
                                group_rows=MLA_HEADS * dec_seq, pos0=past_len)
        outs_p.append(op)
        outs_s.append(os_)

    stack = lambda outs, k: jnp.stack([o[k] for o in outs])
    return (hp, hs,
            stack(outs_p, 0), jnp.swapaxes(stack(outs_p, 1), 2, 3), stack(outs_p, 2), stack(outs_p, 3),
            stack(outs_s, 0), stack(outs_s, 1), stack(outs_s, 2), stack(outs_s, 3))
```

```python
import functools

import jax
import jax.numpy as jnp
import numpy as np
from jax import lax
from jax.experimental import pallas as pl
from jax.experimental.pallas import tpu as pltpu

F32 = jnp.float32
BF16 = jnp.bfloat16

D_MODEL = 1024
DEPTH = 2
CHUNK = 64
EPS = 1e-6
N_ADA = 9
D_FF = 2816
MLA_HEADS = 4
Q_RANK = 256
KV_RANK = 128
NOPE_DIM = 128
ROPE_DIM = 64
V_DIM = 128
ROPE_BASE = 10000.0
ATTN_SCALE = (NOPE_DIM + ROPE_DIM) ** -0.5
LOG2_E = 1.4426950408889634
CONV_DIM = 256
CONV_W = 3
POOL_WINDOWS = (2, 4, 8, 16)
POOL_GROUPS = 4
POOL_DIM = 256
POOL_GROUP_DIM = POOL_DIM // POOL_GROUPS
POOL_MAX = 16

LANES = 128
BF16_SUBLANES = 16
KEY_DIM = 2 * LANES
FF_CHUNK = 256
ADA_ROWS = 16
ADA_COLS = 1024
HALO = 16
NEG_BIG = -1e30
VMEM_LIMIT = 56 * 1024 * 1024

IN_CQ, IN_CKV, IN_KR, IN_CB, IN_CC, IN_CV, IN_PU, IN_END = 0, 256, 384, 512, 768, 1024, 1280, 1536
UQ_NOPE, UQ_ROPE, UQ_END = 0, 512, 1024


def _rms(x, g):
    return x * lax.rsqrt(jnp.mean(x * x, axis=-1, keepdims=True) + EPS) * g


def _silu(x):
    return x * jax.nn.sigmoid(x)


def _dot(a, b):
    return jnp.dot(a, b, preferred_element_type=F32)


def _const_spec(shape):
    nd = len(shape)
    return pl.BlockSpec(shape, lambda *_: (0,) * nd, pipeline_mode=pl.Buffered(1))


def _layer_spec(stacked, layer):
    rest = stacked.shape[1:]
    return pl.BlockSpec((None,) + rest, lambda *_: (layer,) + (0,) * len(rest),
                        pipeline_mode=pl.Buffered(1))


def _ada_kernel(c_ref, w_ref, b_ref, o_ref):
    a = _silu(c_ref[...]).astype(BF16)
    o_ref[0] = _dot(a, w_ref[0].astype(BF16)) + b_ref[0]


def _ada(c_all, w_ada, b_ada):
    n_l, _, n_cols = w_ada.shape
    return pl.pallas_call(
        _ada_kernel,
        grid=(n_l, n_cols // ADA_COLS),
        in_specs=[
            pl.BlockSpec((ADA_ROWS, D_MODEL), lambda l, j: (0, 0)),
            pl.BlockSpec((1, D_MODEL, ADA_COLS), lambda l, j: (l, 0, j)),
            pl.BlockSpec((1, 1, ADA_COLS), lambda l, j: (l, 0, j)),
        ],
        out_specs=pl.BlockSpec((1, ADA_ROWS, ADA_COLS), lambda l, j: (l, 0, j)),
        out_shape=jax.ShapeDtypeStruct((n_l, ADA_ROWS, n_cols), F32),
        compiler_params=pltpu.CompilerParams(
            dimension_semantics=("arbitrary", "arbitrary"), vmem_limit_bytes=VMEM_LIMIT),
        name="ada",
    )(c_all, w_ada, b_ada.reshape(n_l, 1, n_cols))


def _rope_table_kernel(inv_ref, sign_ref, cos_ref, sin_ref, cos_row_ref, sin_row_ref, *, pos0, tm):
    t = pl.program_id(0)

    @pl.when(t == 0)
    def _():
        row = lax.broadcasted_iota(jnp.int32, (tm, LANES), 0).astype(F32)
        ang = row * inv_ref[...]
        cos_row_ref[...] = jnp.cos(ang)
        sin_row_ref[...] = jnp.sin(ang)

    start = (pos0 + t * tm).astype(F32) * inv_ref[...]
    cos_s, sin_s = jnp.cos(start), jnp.sin(start)
    cos_r, sin_r = cos_row_ref[...], sin_row_ref[...]
    cos_ref[...] = cos_s * cos_r - sin_s * sin_r
    sin_ref[...] = (sin_s * cos_r + cos_s * sin_r) * sign_ref[...]


def _rope_tables(n_rows, pos0, tm):
    half = ROPE_DIM // 2
    inv = ROPE_BASE ** (-jnp.arange(half, dtype=F32) / half)
    inv = jnp.tile(inv, LANES // half).reshape(1, LANES)
    sign = jnp.tile(jnp.concatenate([-jnp.ones((half,), F32), jnp.ones((half,), F32)]),
                    LANES // ROPE_DIM).reshape(1, LANES)
    return pl.pallas_call(
        functools.partial(_rope_table_kernel, pos0=pos0, tm=tm),
        grid=(n_rows // tm,),
        in_specs=[_const_spec((1, LANES)), _const_spec((1, LANES))],
        out_specs=[pl.BlockSpec((tm, LANES), lambda i: (i, 0))] * 2,
        out_shape=[jax.ShapeDtypeStruct((n_rows, LANES), F32)] * 2,
        scratch_shapes=[pltpu.VMEM((tm, LANES), F32)] * 2,
        compiler_params=pltpu.CompilerParams(dimension_semantics=("arbitrary",)),
        name="rope_tables",
    )(inv, sign)


def _ffn_kernel(*refs, has_mix, final_norm, n_cast):
    refs = list(refs)
    h_ref = refs.pop(0)
    if has_mix:
        a_ref, sp_ref, wo_ref = refs.pop(0), refs.pop(0), refs.pop(0)
    mod_ref, ng_ref, wgu_ref, wd_ref = refs.pop(0), refs.pop(0), refs.pop(0), refs.pop(0)
    if final_norm:
        nf_ref = refs.pop(0)
    cast_in = [refs.pop(0) for _ in range(n_cast)]
    o_ref = refs.pop(0)
    for src, dst in zip(cast_in, refs):
        dst[...] = src[...].astype(BF16)

    bt, tt, d = h_ref.shape
    m = bt * tt
    x = h_ref[...]
    mod = mod_ref[...]
    row0 = 0
    if has_mix:
        mix_w = a_ref.shape[-1]
        mix = _dot(a_ref[...].reshape(m, mix_w), wo_ref[0:mix_w, :])
        mix = mix + _dot(sp_ref[...].reshape(m, sp_ref.shape[-1]), wo_ref[mix_w:, :])
        x = x + mod[:, 5:6, :] * mix.reshape(bt, tt, d)
        row0 = 6
    shift, scale, gate = mod[:, row0:row0 + 1, :], mod[:, row0 + 1:row0 + 2, :], mod[:, row0 + 2:row0 + 3, :]
    n = _rms(x, ng_ref[...]) * (1.0 + scale) + shift
    nb = n.reshape(m, d).astype(BF16)
    acc = jnp.zeros((m, d), F32)
    for c in range(D_FF // FF_CHUNK):
        g = _dot(nb, wgu_ref[:, c * FF_CHUNK:(c + 1) * FF_CHUNK])
        u = _dot(nb, wgu_ref[:, D_FF + c * FF_CHUNK:D_FF + (c + 1) * FF_CHUNK])
        act = (_silu(g) * u).astype(BF16)
        acc = acc + _dot(act, wd_ref[c * FF_CHUNK:(c + 1) * FF_CHUNK, :])
    y = x + 0.5 * gate * acc.reshape(bt, tt, d)
    if final_norm:
        y = _rms(y, nf_ref[...])
    o_ref[...] = y


def _ffn(h, mod, layer, norm_g, w_gu, w_down, *, bt, tt, mix=None, norm_final=None, cast=()):
    n_b, n_t, d = h.shape
    grid = (n_b // bt, n_t // tt)
    tile = lambda w: pl.BlockSpec((bt, tt, w), lambda b, t: (b, t, 0))
    args, specs = [h], [tile(d)]
    if mix is not None:
        a, sp, w_out = mix
        args += [a, sp, w_out]
        specs += [tile(a.shape[-1]), tile(sp.shape[-1]), _layer_spec(w_out, layer)]
    args += [mod, norm_g, w_gu, w_down]
    specs += [pl.BlockSpec((bt, N_ADA, d), lambda b, t: (b, 0, 0)), _layer_spec(norm_g, layer),
              _layer_spec(w_gu, 0), _layer_spec(w_down, 0)]
    if norm_final is not None:
        args.append(norm_final.reshape(1, d))
        specs.append(_const_spec((1, d)))
    out_specs, out_shape = [tile(d)], [jax.ShapeDtypeStruct(h.shape, F32)]
    assert not cast or grid[0] == 1
    for stacked, src_layer in cast:
        _, n_rows, n_cols = stacked.shape
        share = next(k for k in (1, 2, 4, 8) if grid[1] % k == 0 and n_rows % (grid[1] // k) == 0
                     and (n_rows // (grid[1] // k)) % BF16_SUBLANES == 0)
        slab = n_rows // (grid[1] // share)
        args.append(stacked)
        specs.append(pl.BlockSpec((None, slab, n_cols),
                                  lambda b, t, _l=src_layer, _s=share: (_l, t // _s, 0)))
        out_specs.append(pl.BlockSpec((None, slab, n_cols), lambda b, t, _s=share: (0, t // _s, 0)))
        out_shape.append(jax.ShapeDtypeStruct((1, n_rows, n_cols), BF16))
    outs = pl.pallas_call(
        functools.partial(_ffn_kernel, has_mix=mix is not None, final_norm=norm_final is not None,
                          n_cast=len(cast)),
        grid=grid,
        in_specs=specs,
        out_specs=out_specs,
        out_shape=out_shape,
        compiler_params=pltpu.CompilerParams(
            dimension_semantics=("arbitrary", "arbitrary"), vmem_limit_bytes=VMEM_LIMIT),
        name="ffn_mix" if mix is not None else "ffn",
    )(*args)
    return outs[0], tuple(outs[1:])


def _mix_in_kernel(h_ref, mod_ref, ng_ref, wina_ref, winb_ref, qn_ref, wuq_ref, kvn_ref, wukt_ref,
                   convw_ref, poolw_ref, pools_ref, cos_ref, sin_ref, cstate_ref, pstate_ref,
                   q_ref, k_ref, ckv_ref, kr_ref, sp_ref, cnew_ref, pnew_ref,
                   eu_ref, ep_ref, s2_ref, s4_ref, s8_ref, *, tm, pos0, kr_transposed):
    t = pl.program_id(1)
    x = h_ref[0]
    mod = mod_ref[0]
    n = _rms(x, ng_ref[...]) * (1.0 + mod[4:5, :]) + mod[3:4, :]
    nb = n.astype(BF16)
    proj = jnp.concatenate([_dot(nb, wina_ref[...]), _dot(nb, winb_ref[...])], axis=-1)
    cos = cos_ref[...]
    sin = sin_ref[...]
    lane = lax.broadcasted_iota(jnp.int32, (tm, LANES), 1)

    def rope(packed):
        return packed * cos + pltpu.roll(packed, ROPE_DIM, axis=1) * sin

    qn = _rms(proj[:, IN_CQ:IN_CKV], qn_ref[...]).astype(BF16)
    qall = _dot(qn, wuq_ref[...])
    for hd in range(MLA_HEADS):
        q_nope = qall[:, UQ_NOPE + hd * NOPE_DIM:UQ_NOPE + (hd + 1) * NOPE_DIM].astype(BF16)
        q_lat = _dot(q_nope, wukt_ref[hd])
        q_rope = jnp.where(lane < ROPE_DIM, rope(qall[:, UQ_ROPE + hd * LANES:UQ_ROPE + (hd + 1) * LANES]), 0.0)
        q_ref[0, hd] = (jnp.concatenate([q_lat, q_rope], axis=-1) * (ATTN_SCALE * LOG2_E)).astype(BF16)

    ckv = _rms(proj[:, IN_CKV:IN_KR], kvn_ref[...])
    kr = rope(proj[:, IN_KR:IN_CB])
    ckv_ref[0] = ckv
    kr_ref[0] = jnp.transpose(kr)[:ROPE_DIM, :] if kr_transposed else kr[:, :ROPE_DIM]
    k_ref[0] = jnp.concatenate([ckv, jnp.where(lane < ROPE_DIM, kr, 1.0)], axis=-1).astype(BF16)

    @pl.when(t == 0)
    def _():
        eu_ref[0:HALO - (CONV_W - 1), :] = jnp.zeros((HALO - (CONV_W - 1), CONV_DIM), F32)
        eu_ref[HALO - (CONV_W - 1):HALO, :] = cstate_ref[0]
        ep_ref[0:1, :] = jnp.zeros((1, POOL_DIM), F32)
        ep_ref[1:HALO, :] = pstate_ref[0]

    u = proj[:, IN_CC:IN_CV] * proj[:, IN_CV:IN_PU]
    eu_ref[HALO:HALO + tm, :] = u
    cw = convw_ref[...]
    y = u * cw[CONV_W - 1:CONV_W, :]
    for j in range(CONV_W - 1):
        back = CONV_W - 1 - j
        y = y + eu_ref[HALO - back:HALO - back + tm, :] * cw[j:j + 1, :]
    sp_ref[0, :, 0:CONV_DIM] = (proj[:, IN_CB:IN_CC] * y).astype(BF16)
    cnew_ref[0] = eu_ref[HALO + tm - (CONV_W - 1):HALO + tm, :]
    eu_ref[0:HALO, :] = eu_ref[tm:tm + HALO, :]

    pu = proj[:, IN_PU:IN_END]
    ep_ref[HALO:HALO + tm, :] = pu
    n_ext = tm + HALO
    s2_ref[1:n_ext, :] = ep_ref[1:n_ext, :] + ep_ref[0:n_ext - 1, :]
    s4_ref[3:n_ext, :] = s2_ref[3:n_ext, :] + s2_ref[1:n_ext - 2, :]
    s8_ref[7:n_ext, :] = s4_ref[7:n_ext, :] + s4_ref[3:n_ext - 4, :]
    s16 = s8_ref[HALO:n_ext, :] + s8_ref[HALO - 8:n_ext - 8, :]
    lane_grp = lax.broadcasted_iota(jnp.int32, (tm, POOL_DIM), 1) >> (POOL_GROUP_DIM.bit_length() - 1)
    wsum = jnp.where(lane_grp == 0, s2_ref[HALO:n_ext, :],
                     jnp.where(lane_grp == 1, s4_ref[HALO:n_ext, :],
                               jnp.where(lane_grp == 2, s8_ref[HALO:n_ext, :], s16)))
    win = jnp.left_shift(2, lane_grp)
    pos = pos0 + t * tm + lax.broadcasted_iota(jnp.int32, (tm, POOL_DIM), 0)
    cnt = jnp.minimum(win, pos + 1).astype(F32)
    dlt = (wsum / cnt - pu).astype(BF16)
    sp_ref[0, :, CONV_DIM:] = (_dot(dlt, poolw_ref[...]) * pools_ref[...]).astype(BF16)
    pnew_ref[0] = ep_ref[tm + 1:tm + HALO, :]
    ep_ref[0:HALO, :] = ep_ref[tm:tm + HALO, :]


def _mix_in(h, mod, layer, lw, cos_t, sin_t, conv_state, pool_state, *, tm, pos0, kr_transposed):
    n_b, n_t, d = h.shape
    grid = (n_b, n_t // tm)
    rows = lambda w: pl.BlockSpec((1, tm, w), lambda b, t: (b, t, 0))
    per_b = lambda r, w: pl.BlockSpec((1, r, w), lambda b, t: (b, 0, 0))
    weights = [lw[k] for k in ("norm_mix", "w_in_a", "w_in_b", "q_norm", "w_uq", "kv_norm", "w_ukt",
                               "conv_w", "pool_bd", "pool_scale")]
    kr_spec = pl.BlockSpec((1, ROPE_DIM, tm), lambda b, t: (b, 0, t)) if kr_transposed else rows(ROPE_DIM)
    kr_shape = (n_b, ROPE_DIM, n_t) if kr_transposed else (n_b, n_t, ROPE_DIM)
    in_specs = [rows(d), per_b(N_ADA, d)] + [_layer_spec(w, layer) for w in weights] + [
        pl.BlockSpec((tm, LANES), lambda b, t: (t, 0)), pl.BlockSpec((tm, LANES), lambda b, t: (t, 0)),
        per_b(CONV_W - 1, CONV_DIM), per_b(POOL_MAX - 1, POOL_DIM),
    ]
    out_specs = [
        pl.BlockSpec((1, MLA_HEADS, tm, KEY_DIM), lambda b, t: (b, 0, t, 0)),
        rows(KEY_DIM), rows(KV_RANK), kr_spec, rows(CONV_DIM + POOL_DIM),
        per_b(CONV_W - 1, CONV_DIM), per_b(POOL_MAX - 1, POOL_DIM),
    ]
    out_shape = [
        jax.ShapeDtypeStruct((n_b, MLA_HEADS, n_t, KEY_DIM), BF16),
        jax.ShapeDtypeStruct((n_b, n_t, KEY_DIM), BF16),
        jax.ShapeDtypeStruct((n_b, n_t, KV_RANK), F32),
        jax.ShapeDtypeStruct(kr_shape, F32),
        jax.ShapeDtypeStruct((n_b, n_t, CONV_DIM + POOL_DIM), BF16),
        jax.ShapeDtypeStruct((n_b, CONV_W - 1, CONV_DIM), F32),
        jax.ShapeDtypeStruct((n_b, POOL_MAX - 1, POOL_DIM), F32),
    ]
    ext = pltpu.VMEM((tm + HALO, POOL_DIM), F32)
    return pl.pallas_call(
        functools.partial(_mix_in_kernel, tm=tm, pos0=pos0, kr_transposed=kr_transposed),
        grid=grid,
        in_specs=in_specs,
        out_specs=out_specs,
        out_shape=out_shape,
        scratch_shapes=[ext, ext, ext, ext, ext],
        compiler_params=pltpu.CompilerParams(
            dimension_semantics=("arbitrary", "arbitrary"), vmem_limit_bytes=VMEM_LIMIT),
        name="mix_in",
    )(h, mod, *weights, cos_t, sin_t, conv_state, pool_state)


def _attn_kernel(*refs, tq, tk, tk_past, past_len, group_rows):
    refs = list(refs)
    q_ref, knew_ref = refs.pop(0), refs.pop(0)
    if past_len:
        ckv_past_ref, krt_past_ref = refs.pop(0), refs.pop(0)
    wuv_ref, o_ref, m_ref, acc_ref = refs[:4]
    s_ref = None if past_len else refs[4]

    i = pl.program_id(1)
    m_rows = MLA_HEADS * tq
    n_groups = m_rows // group_rows
    m_ref[...] = jnp.full(m_ref.shape, NEG_BIG, F32)
    acc_ref[...] = jnp.zeros(acc_ref.shape, F32)

    def lane_tile(x, width):
        return x[:, :width] if width < LANES else jnp.concatenate([x] * (width // LANES), axis=1)

    def group(g):
        return slice(g * group_rows, (g + 1) * group_rows)

    def scores(g, kblk):
        if group_rows >= tq:
            heads = group_rows // tq
            q = q_ref[0, g * heads:(g + 1) * heads].reshape(group_rows, KEY_DIM)
        else:
            per_head = tq // group_rows
            r0 = (g % per_head) * group_rows
            q = q_ref[0, g // per_head, r0:r0 + group_rows, :]
        return lax.dot_general(q, kblk, (((1,), (1,)), ((), ())), preferred_element_type=F32)

    def absorb(g, s, kblk, mask):
        width = kblk.shape[0]
        rows = group(g)
        if mask is not None:
            s = jnp.where(mask, s, NEG_BIG)
        m_prev = m_ref[rows, :]
        m_new = jnp.maximum(m_prev, jnp.max(s, axis=1, keepdims=True))
        alpha = jnp.exp2(m_prev - m_new)
        p = jnp.exp2(s - lane_tile(m_new, width)).astype(BF16)
        acc_ref[rows, :] = lane_tile(alpha, KEY_DIM) * acc_ref[rows, :] + _dot(p, kblk)
        m_ref[rows, :] = m_new

    chunk_shift = CHUNK.bit_length() - 1

    def diag_mask(g, col0, width):
        r0, n_r = ((g * group_rows) % tq, group_rows) if group_rows < tq else (0, tq)
        if (col0 + width - 1) >> chunk_shift <= r0 >> chunk_shift:
            return "all"
        if col0 >> chunk_shift > (r0 + n_r - 1) >> chunk_shift:
            return "none"
        row = (lax.broadcasted_iota(jnp.int32, (group_rows, width), 0) + r0) & (tq - 1)
        col = lax.broadcasted_iota(jnp.int32, (group_rows, width), 1) + col0
        return (col >> chunk_shift) <= (row >> chunk_shift)

    def mask_arg(vis):
        return None if isinstance(vis, str) else vis

    def any_visible(vis):
        return not (isinstance(vis, str) and vis == "none")

    if past_len:
        def past_body(j, carry):
            start = pl.multiple_of(j * tk_past, tk_past)
            lat = ckv_past_ref[0, pl.ds(start, tk_past), :].astype(BF16)
            rot_t = krt_past_ref[0, :, pl.ds(start, tk_past)].astype(BF16)
            for g in range(n_groups):
                rows = group(g)
                q = q_ref[0].reshape(m_rows, KEY_DIM)[rows]
                s = lax.dot_general(q[:, :KV_RANK], lat, (((1,), (1,)), ((), ())), preferred_element_type=F32)
                s = s + _dot(q[:, KV_RANK:KV_RANK + ROPE_DIM], rot_t)
                m_prev = m_ref[rows, :]
                m_new = jnp.maximum(m_prev, jnp.max(s, axis=1, keepdims=True))
                alpha = jnp.exp2(m_prev - m_new)
                p = jnp.exp2(s - lane_tile(m_new, tk_past))
                denom = jnp.broadcast_to(jnp.sum(p, axis=1, keepdims=True), (group_rows, KEY_DIM - KV_RANK))
                acc_ref[rows, :] = lane_tile(alpha, KEY_DIM) * acc_ref[rows, :] + jnp.concatenate(
                    [_dot(p.astype(BF16), lat), denom], axis=-1)
                m_ref[rows, :] = m_new
            return carry
        lax.fori_loop(0, past_len // tk_past, past_body, 0)
        kblk = knew_ref[0]
        for g in range(n_groups):
            absorb(g, scores(g, kblk), kblk, mask_arg(diag_mask(g, 0, tq)))
    else:
        def new_block(j):
            return knew_ref[0, pl.ds(pl.multiple_of(j * tk, tk), tk), :]

        n_diag = tq // tk
        n_full = i * n_diag
        first = new_block(0)
        for g in range(n_groups):
            s_ref[group(g), :] = scores(g, first)

        def body(j, carry):
            kblk, knext = new_block(j), new_block(j + 1)
            for g in range(n_groups):
                s = s_ref[group(g), :]
                s_ref[group(g), :] = scores(g, knext)
                absorb(g, s, kblk, None)
            return carry
        lax.fori_loop(0, n_full, body, 0)

        for d in range(n_diag):
            kblk = new_block(n_full + d)
            knext = new_block(n_full + d + 1) if d + 1 < n_diag else None
            for g in range(n_groups):
                vis = diag_mask(g, d * tk, tk)
                if any_visible(vis):
                    s = s_ref[group(g), :]
                if knext is not None and any_visible(diag_mask(g, (d + 1) * tk, tk)):
                    s_ref[group(g), :] = scores(g, knext)
                if any_visible(vis):
                    absorb(g, s, kblk, mask_arg(vis))

    heads = []
    for hd in range(MLA_HEADS):
        acc = acc_ref[hd * tq:(hd + 1) * tq, :]
        denom = acc[:, KV_RANK + ROPE_DIM:KV_RANK + ROPE_DIM + 1]
        heads.append(_dot((acc[:, :KV_RANK] / denom).astype(BF16), wuv_ref[hd]))
    o_ref[0] = jnp.concatenate(heads, axis=-1).astype(BF16)


def _attn(q, k_new, past, layer, w_uv, *, tq, tk, tk_past, group_rows):
    n_b, _, n_t, _ = q.shape
    past_len = 0 if past is None else past[0].shape[2]
    assert (past_len % CHUNK == 0) and (tq % CHUNK == 0 or n_t == tq <= CHUNK)
    assert past_len == 0 or (n_t == tq and past_len % tk_past == 0)
    m_rows = MLA_HEADS * tq
    assert m_rows % group_rows == 0 and (group_rows % tq == 0 or tq % group_rows == 0)
    assert past_len or (tq % tk == 0 and tk % CHUNK == 0)
    scratch = [pltpu.VMEM((m_rows, LANES), F32), pltpu.VMEM((m_rows, KEY_DIM), F32)]
    if not past_len:
        scratch.append(pltpu.VMEM((m_rows, tk), F32))
    args = [q, k_new]
    specs = [pl.BlockSpec((1, MLA_HEADS, tq, KEY_DIM), lambda b, i: (b, 0, i, 0)),
             pl.BlockSpec((1, n_t, KEY_DIM), lambda b, i: (b, 0, 0), pipeline_mode=pl.Buffered(1))]
    if past_len:
        args += list(past)
        specs += [pl.BlockSpec((None, 1) + c.shape[2:], lambda b, i: (layer, b, 0, 0)) for c in past]
    args.append(w_uv)
    specs.append(_layer_spec(w_uv, layer))
    return pl.pallas_call(
        functools.partial(_attn_kernel, tq=tq, tk=tk, tk_past=tk_past, past_len=past_len,
                          group_rows=group_rows),
        grid=(n_b, n_t // tq),
        in_specs=specs,
        out_specs=pl.BlockSpec((1, tq, MLA_HEADS * V_DIM), lambda b, i: (b, i, 0)),
        out_shape=jax.ShapeDtypeStruct((n_b, n_t, MLA_HEADS * V_DIM), BF16),
        scratch_shapes=scratch,
        compiler_params=pltpu.CompilerParams(
            dimension_semantics=("arbitrary", "arbitrary"), vmem_limit_bytes=VMEM_LIMIT),
        name="attn",
    )(*args)


def _prep_weights(norm_ffn1, norm_mix, w_in, q_norm, w_uq, kv_norm, w_ukv, conv_w, pool_w, pool_scale,
                  w_out, norm_ffn2):
    n_l = w_in.shape[0]
    half = ROPE_DIM // 2
    packed = lambda w: jnp.concatenate([w, w[..., half:], w[..., :half]], axis=-1)
    vec = lambda v: v.reshape(n_l, 1, v.shape[-1])
    cq, ckv, kr, rest = (w_in[..., :Q_RANK], w_in[..., Q_RANK:Q_RANK + KV_RANK],
                         w_in[..., Q_RANK + KV_RANK:Q_RANK + KV_RANK + ROPE_DIM],
                         w_in[..., Q_RANK + KV_RANK + ROPE_DIM:])
    w_in_a = jnp.concatenate([cq, ckv, packed(kr)], axis=-1).astype(BF16)
    w_in_b = rest.astype(BF16)

    wq = w_uq.reshape(n_l, Q_RANK, MLA_HEADS, NOPE_DIM + ROPE_DIM)
    nope = wq[..., :NOPE_DIM].reshape(n_l, Q_RANK, MLA_HEADS * NOPE_DIM)
    rope_cols = [wq[:, :, hd, NOPE_DIM:] for hd in range(MLA_HEADS)]
    w_uq_p = jnp.concatenate([nope] + [packed(r) for r in rope_cols], axis=-1).astype(BF16)

    wkv = w_ukv.reshape(n_l, KV_RANK, MLA_HEADS, NOPE_DIM + V_DIM)
    w_ukt = jnp.transpose(wkv[..., :NOPE_DIM], (0, 2, 3, 1)).astype(BF16)
    w_uv = jnp.transpose(wkv[..., NOPE_DIM:], (0, 2, 1, 3)).astype(BF16)

    eye = jnp.eye(POOL_GROUPS, dtype=F32)
    pool_bd = (pool_w[:, :, :, None, :] * eye[None, :, None, :, None]).reshape(
        n_l, POOL_DIM, POOL_DIM).astype(BF16)

    return {
        "norm_ffn1": vec(norm_ffn1),
        "norm_mix": vec(norm_mix), "w_in_a": w_in_a, "w_in_b": w_in_b, "q_norm": vec(q_norm),
        "w_uq": w_uq_p,
        "kv_norm": vec(kv_norm), "w_ukt": w_ukt, "w_uv": w_uv,
        "conv_w": conv_w, "pool_bd": pool_bd, "pool_scale": vec(pool_scale),
        "w_out": w_out.astype(BF16),
        "norm_ffn2": vec(norm_ffn2),
    }


def _trunk_layer(h, mod, layer, lw, ffn1_w, ffn2_w, cast1, cast2, tables, conv_state, pool_state, past,
                 norm_final, *, bt, tt, tm, tq, group_rows, pos0):
    h1, made1 = _ffn(h, mod, layer, lw["norm_ffn1"], *ffn1_w, bt=bt, tt=tt, cast=cast1)
    if ffn2_w is None:
        ffn2_w = made1
    q, k_new, ckv, kr, sp, conv_new, pool_new = _mix_in(
        h1, mod, layer, lw, tables[0], tables[1], conv_state, pool_state, tm=tm, pos0=pos0,
        kr_transposed=past is None)
    a = _attn(q, k_new, past, layer, lw["w_uv"], tq=tq, tk=min(tq, 512), tk_past=2048,
              group_rows=group_rows)
    h3, made2 = _ffn(h1, mod, layer, lw["norm_ffn2"], *ffn2_w, bt=bt, tt=tt,
                     mix=(a, sp, lw["w_out"]), norm_final=norm_final, cast=cast2)
    return (h3, ckv, kr, conv_new, pool_new), ffn2_w, made2


def kernel(x_prompt, x_sample, c_prompt, c_sample, cache_ckv, cache_krope, state_conv, state_pool, w_ada, b_ada, norm_ffn1, w_ffn1_gu, w_ffn1_down, norm_mix, w_in, q_norm, w_uq, kv_norm, w_ukv, conv_w, pool_w, pool_scale, w_out, norm_ffn2, w_ffn2_gu, w_ffn2_down, norm_final):
    bp, seq, _ = x_prompt.shape
    bs, dec_seq, _ = x_sample.shape
    past_len = cache_ckv.shape[2]
    assert bp + bs <= ADA_ROWS

    c_all = jnp.concatenate(
        [c_prompt, c_sample, jnp.zeros((ADA_ROWS - bp - bs, D_MODEL), F32)], axis=0)
    mods = _ada(c_all, w_ada, b_ada).reshape(DEPTH, ADA_ROWS, N_ADA, D_MODEL)

    tm_p = 512
    tables_p = _rope_tables(seq, 0, tm_p)
    tables_s = _rope_tables(dec_seq, past_len, dec_seq)
    lw = _prep_weights(norm_ffn1, norm_mix, w_in, q_norm, w_uq, kv_norm, w_ukv, conv_w, pool_w,
                       pool_scale, w_out, norm_ffn2)
    past = (cache_ckv, jnp.swapaxes(cache_krope, 2, 3))

    hp, hs = x_prompt, x_sample
    outs_p, outs_s = [], []
    zero_conv = jnp.zeros((bp, CONV_W - 1, CONV_DIM), F32)
    zero_pool = jnp.zeros((bp, POOL_MAX - 1, POOL_DIM), F32)
    ffn1_w = (w_ffn1_gu[:1].astype(BF16), w_ffn1_down[:1].astype(BF16))
    for l in range(DEPTH):
        nf = norm_final if l == DEPTH - 1 else None
        cast2 = ((w_ffn1_gu, l + 1), (w_ffn1_down, l + 1)) if l + 1 < DEPTH else ()
        op, ffn2_w, next_ffn1_w = _trunk_layer(
            hp, mods[l, :bp], l, lw, ffn1_w, None, ((w_ffn2_gu, l), (w_ffn2_down, l)), cast2,
            tables_p, zero_conv, zero_pool, None, nf,
            bt=1, tt=tm_p, tm=tm_p, tq=1024, group_rows=512, pos0=0)
        os_, _, _ = _trunk_layer(
            hs, mods[l, bp:bp + bs], l, lw, ffn1_w, ffn2_w, (), (),
            tables_s, state_conv[l], state_pool[l], past, nf,
            bt=bs, tt=dec_seq, tm=dec_seq, tq=dec_seq, group_rows=MLA_HEADS * dec_seq, pos0=past_len)
        hp, hs, ffn1_w = op[0], os_[0], next_ffn1_w
        outs_p.append(op[1:])
        outs_s.append(os_[1:])

    stack = lambda outs, k: jnp.stack([o[k] for o in outs])
    return (hp, hs,
            stack(outs_p, 0), jnp.swapaxes(stack(outs_p, 1), 2, 3), stack(outs_p, 2), stack(outs_p, 3),
            stack(outs_s, 0), stack(outs_s, 1), stack(outs_s, 2), stack(outs_s, 3))
```

```python
import functools

import jax
import jax.numpy as jnp
import numpy as np
from jax import lax
from jax.experimental import pallas as pl
from jax.experimental.pallas import tpu as pltpu

F32 = jnp.float32
BF16 = jnp.bfloat16

D_MODEL = 1024
DEPTH = 2
CHUNK = 64
EPS = 1e-6
N_ADA = 9
D_FF = 2816
MLA_HEADS = 4
Q_RANK = 256
KV_RANK = 128
NOPE_DIM = 128
ROPE_DIM = 64
V_DIM = 128
ROPE_BASE = 10000.0
ATTN_SCALE = (NOPE_DIM + ROPE_DIM) ** -0.5
LOG2_E = 1.4426950408889634
CONV_DIM = 256
CONV_W = 3
POOL_WINDOWS = (2, 4, 8, 16)
POOL_GROUPS = 4
POOL_DIM = 256
POOL_GROUP_DIM = POOL_DIM // POOL_GROUPS
POOL_MAX = 16

LANES = 128
BF16_SUBLANES = 16
KEY_DIM = 2 * LANES
FF_CHUNK = 256
ADA_ROWS = 16
ADA_COLS = 1024
HALO = 16
NEG_BIG = -1e30
VMEM_LIMIT = 56 * 1024 * 1024

IN_CQ, IN_CKV, IN_KR, IN_CB, IN_CC, IN_CV, IN_PU, IN_END = 0, 256, 384, 512, 768, 1024, 1280, 1536
UQ_NOPE, UQ_ROPE, UQ_END = 0, 512, 1024


def _rms(x, g):
    return x * lax.rsqrt(jnp.mean(x * x, axis=-1, keepdims=True) + EPS) * g


def _silu(x):
    return x * jax.nn.sigmoid(x)


def _dot(a, b):
    return jnp.dot(a, b, preferred_element_type=F32)


def _const_spec(shape):
    nd = len(shape)
    return pl.BlockSpec(shape, lambda *_: (0,) * nd, pipeline_mode=pl.Buffered(1))


def _layer_spec(stacked, layer):
    rest = stacked.shape[1:]
    return pl.BlockSpec((None,) + rest, lambda *_: (layer,) + (0,) * len(rest),
                        pipeline_mode=pl.Buffered(1))


def _ada_kernel(c_ref, w_ref, b_ref, o_ref):
    a = _silu(c_ref[...]).astype(BF16)
    o_ref[0] = _dot(a, w_ref[0].astype(BF16)) + b_ref[0]


def _ada(c_all, w_ada, b_ada):
    n_l, _, n_cols = w_ada.shape
    return pl.pallas_call(
        _ada_kernel,
        grid=(n_l, n_cols // ADA_COLS),
        in_specs=[
            pl.BlockSpec((ADA_ROWS, D_MODEL), lambda l, j: (0, 0)),
            pl.BlockSpec((1, D_MODEL, ADA_COLS), lambda l, j: (l, 0, j)),
            pl.BlockSpec((1, 1, ADA_COLS), lambda l, j: (l, 0, j)),
        ],
        out_specs=pl.BlockSpec((1, ADA_ROWS, ADA_COLS), lambda l, j: (l, 0, j)),
        out_shape=jax.ShapeDtypeStruct((n_l, ADA_ROWS, n_cols), F32),
        compiler_params=pltpu.CompilerParams(
            dimension_semantics=("arbitrary", "arbitrary"), vmem_limit_bytes=VMEM_LIMIT),
        name="ada",
    )(c_all, w_ada, b_ada.reshape(n_l, 1, n_cols))


def _cast_kernel(src_ref, dst_ref):
    dst_ref[...] = src_ref[...].astype(BF16)


def _cast_layer(stacked, layer, n_steps=8):
    _, n_rows, n_cols = stacked.shape
    slab = n_rows // n_steps
    assert n_rows % n_steps == 0 and slab % BF16_SUBLANES == 0
    return pl.pallas_call(
        _cast_kernel,
        grid=(n_steps,),
        in_specs=[pl.BlockSpec((None, slab, n_cols), lambda t: (layer, t, 0))],
        out_specs=pl.BlockSpec((None, slab, n_cols), lambda t: (0, t, 0)),
        out_shape=jax.ShapeDtypeStruct((1, n_rows, n_cols), BF16),
        compiler_params=pltpu.CompilerParams(
            dimension_semantics=("arbitrary",), vmem_limit_bytes=VMEM_LIMIT),
        name="cast_layer",
    )(stacked)


def _rope_table_kernel(inv_ref, sign_ref, cos_ref, sin_ref, cos_row_ref, sin_row_ref, *, pos0, tm):
    t = pl.program_id(0)

    @pl.when(t == 0)
    def _():
        row = lax.broadcasted_iota(jnp.int32, (tm, LANES), 0).astype(F32)
        ang = row * inv_ref[...]
        cos_row_ref[...] = jnp.cos(ang)
        sin_row_ref[...] = jnp.sin(ang)

    start = (pos0 + t * tm).astype(F32) * inv_ref[...]
    cos_s, sin_s = jnp.cos(start), jnp.sin(start)
    cos_r, sin_r = cos_row_ref[...], sin_row_ref[...]
    cos_ref[...] = cos_s * cos_r - sin_s * sin_r
    sin_ref[...] = (sin_s * cos_r + cos_s * sin_r) * sign_ref[...]


def _rope_tables(n_rows, pos0, tm):
    half = ROPE_DIM // 2
    inv = ROPE_BASE ** (-jnp.arange(half, dtype=F32) / half)
    inv = jnp.tile(inv, LANES // half).reshape(1, LANES)
    sign = jnp.tile(jnp.concatenate([-jnp.ones((half,), F32), jnp.ones((half,), F32)]),
                    LANES // ROPE_DIM).reshape(1, LANES)
    return pl.pallas_call(
        functools.partial(_rope_table_kernel, pos0=pos0, tm=tm),
        grid=(n_rows // tm,),
        in_specs=[_const_spec((1, LANES)), _const_spec((1, LANES))],
        out_specs=[pl.BlockSpec((tm, LANES), lambda i: (i, 0))] * 2,
        out_shape=[jax.ShapeDtypeStruct((n_rows, LANES), F32)] * 2,
        scratch_shapes=[pltpu.VMEM((tm, LANES), F32)] * 2,
        compiler_params=pltpu.CompilerParams(dimension_semantics=("arbitrary",)),
        name="rope_tables",
    )(inv, sign)


def _ffn_kernel(*refs, has_mix, final_norm, n_cast):
    refs = list(refs)
    h_ref = refs.pop(0)
    if has_mix:
        a_ref, sp_ref, wo_ref = refs.pop(0), refs.pop(0), refs.pop(0)
    mod_ref, ng_ref, wgu_ref, wd_ref = refs.pop(0), refs.pop(0), refs.pop(0), refs.pop(0)
    if final_norm:
        nf_ref = refs.pop(0)
    cast_in = [refs.pop(0) for _ in range(n_cast)]
    o_ref = refs.pop(0)
    for src, dst in zip(cast_in, refs):
        dst[...] = src[...].astype(BF16)

    bt, tt, d = h_ref.shape
    m = bt * tt
    x = h_ref[...]
    mod = mod_ref[...]
    row0 = 0
    if has_mix:
        mix_w = a_ref.shape[-1]
        mix = _dot(a_ref[...].reshape(m, mix_w), wo_ref[0:mix_w, :])
        mix = mix + _dot(sp_ref[...].reshape(m, sp_ref.shape[-1]), wo_ref[mix_w:, :])
        x = x + mod[:, 5:6, :] * mix.reshape(bt, tt, d)
        row0 = 6
    shift, scale, gate = mod[:, row0:row0 + 1, :], mod[:, row0 + 1:row0 + 2, :], mod[:, row0 + 2:row0 + 3, :]
    n = _rms(x, ng_ref[...]) * (1.0 + scale) + shift
    nb = n.reshape(m, d).astype(BF16)
    acc = jnp.zeros((m, d), F32)
    for c in range(D_FF // FF_CHUNK):
        g = _dot(nb, wgu_ref[:, c * FF_CHUNK:(c + 1) * FF_CHUNK])
        u = _dot(nb, wgu_ref[:, D_FF + c * FF_CHUNK:D_FF + (c + 1) * FF_CHUNK])
        act = (_silu(g) * u).astype(BF16)
        acc = acc + _dot(act, wd_ref[c * FF_CHUNK:(c + 1) * FF_CHUNK, :])
    y = x + 0.5 * gate * acc.reshape(bt, tt, d)
    if final_norm:
        y = _rms(y, nf_ref[...])
    o_ref[...] = y


def _ffn(h, mod, layer, norm_g, w_gu, w_down, *, bt, tt, mix=None, norm_final=None, cast=()):
    n_b, n_t, d = h.shape
    grid = (n_b // bt, n_t // tt)
    tile = lambda w: pl.BlockSpec((bt, tt, w), lambda b, t: (b, t, 0))
    args, specs = [h], [tile(d)]
    if mix is not None:
        a, sp, w_out = mix
        args += [a, sp, w_out]
        specs += [tile(a.shape[-1]), tile(sp.shape[-1]), _layer_spec(w_out, layer)]
    args += [mod, norm_g, w_gu, w_down]
    specs += [pl.BlockSpec((bt, N_ADA, d), lambda b, t: (b, 0, 0)), _layer_spec(norm_g, layer),
              _layer_spec(w_gu, 0), _layer_spec(w_down, 0)]
    if norm_final is not None:
        args.append(norm_final.reshape(1, d))
        specs.append(_const_spec((1, d)))
    out_specs, out_shape = [tile(d)], [jax.ShapeDtypeStruct(h.shape, F32)]
    assert not cast or grid[0] == 1
    for stacked, src_layer in cast:
        _, n_rows, n_cols = stacked.shape
        share = next(k for k in (1, 2, 4, 8) if grid[1] % k == 0 and n_rows % (grid[1] // k) == 0
                     and (n_rows // (grid[1] // k)) % BF16_SUBLANES == 0)
        slab = n_rows // (grid[1] // share)
        args.append(stacked)
        specs.append(pl.BlockSpec((None, slab, n_cols),
                                  lambda b, t, _l=src_layer, _s=share: (_l, t // _s, 0)))
        out_specs.append(pl.BlockSpec((None, slab, n_cols), lambda b, t, _s=share: (0, t // _s, 0)))
        out_shape.append(jax.ShapeDtypeStruct((1, n_rows, n_cols), BF16))
    outs = pl.pallas_call(
        functools.partial(_ffn_kernel, has_mix=mix is not None, final_norm=norm_final is not None,
                          n_cast=len(cast)),
        grid=grid,
        in_specs=specs,
        out_specs=out_specs,
        out_shape=out_shape,
        compiler_params=pltpu.CompilerParams(
            dimension_semantics=("arbitrary", "arbitrary"), vmem_limit_bytes=VMEM_LIMIT),
        name="ffn_mix" if mix is not None else "ffn",
    )(*args)
    return outs[0], tuple(outs[1:])


def _mix_in_kernel(h_ref, mod_ref, ng_ref, wint_ref, qn_ref, wuq_ref, kvn_ref, wukt_ref,
                   convw_ref, poolw_ref, pools_ref, cos_ref, sin_ref, cstate_ref, pstate_ref,
                   q_ref, k_ref, ckv_ref, kr_ref, sp_ref, cnew_ref, pnew_ref,
                   eu_ref, ep_ref, s2_ref, s4_ref, s8_ref, *, tm, pos0, kr_transposed):
    t = pl.program_id(1)
    x = h_ref[0]
    mod = mod_ref[0]
    n = _rms(x, ng_ref[...]) * (1.0 + mod[4:5, :]) + mod[3:4, :]
    nb = n.astype(BF16)
    half = ROPE_DIM // 2
    kr0 = Q_RANK + KV_RANK
    nt = lambda w: lax.dot_general(nb, w, (((1,), (1,)), ((), ())), preferred_element_type=F32)
    head = jnp.concatenate([wint_ref[0:kr0 + ROPE_DIM, :], wint_ref[kr0 + half:kr0 + ROPE_DIM, :],
                            wint_ref[kr0:kr0 + half, :]], axis=0)
    proj = jnp.concatenate([nt(head), nt(wint_ref[kr0 + ROPE_DIM:, :])], axis=-1)
    cos = cos_ref[...]
    sin = sin_ref[...]
    lane = lax.broadcasted_iota(jnp.int32, (tm, LANES), 1)

    def rope(packed):
        return packed * cos + pltpu.roll(packed, ROPE_DIM, axis=1) * sin

    qn = _rms(proj[:, IN_CQ:IN_CKV], qn_ref[...]).astype(BF16)
    qall = _dot(qn, wuq_ref[...])
    for hd in range(MLA_HEADS):
        q_nope = qall[:, UQ_NOPE + hd * NOPE_DIM:UQ_NOPE + (hd + 1) * NOPE_DIM].astype(BF16)
        q_lat = _dot(q_nope, wukt_ref[hd])
        q_rope = jnp.where(lane < ROPE_DIM, rope(qall[:, UQ_ROPE + hd * LANES:UQ_ROPE + (hd + 1) * LANES]), 0.0)
        q_ref[0, hd] = (jnp.concatenate([q_lat, q_rope], axis=-1) * (ATTN_SCALE * LOG2_E)).astype(BF16)

    ckv = _rms(proj[:, IN_CKV:IN_KR], kvn_ref[...])
    kr = rope(proj[:, IN_KR:IN_CB])
    ckv_ref[0] = ckv
    kr_ref[0] = jnp.transpose(kr)[:ROPE_DIM, :] if kr_transposed else kr[:, :ROPE_DIM]
    k_ref[0] = jnp.concatenate([ckv, jnp.where(lane < ROPE_DIM, kr, 1.0)], axis=-1).astype(BF16)

    @pl.when(t == 0)
    def _():
        eu_ref[0:HALO - (CONV_W - 1), :] = jnp.zeros((HALO - (CONV_W - 1), CONV_DIM), F32)
        eu_ref[HALO - (CONV_W - 1):HALO, :] = cstate_ref[0]
        ep_ref[0:1, :] = jnp.zeros((1, POOL_DIM), F32)
        ep_ref[1:HALO, :] = pstate_ref[0]

    u = proj[:, IN_CC:IN_CV] * proj[:, IN_CV:IN_PU]
    eu_ref[HALO:HALO + tm, :] = u
    cw = convw_ref[...]
    y = u * cw[CONV_W - 1:CONV_W, :]
    for j in range(CONV_W - 1):
        back = CONV_W - 1 - j
        y = y + eu_ref[HALO - back:HALO - back + tm, :] * cw[j:j + 1, :]
    sp_ref[0, :, 0:CONV_DIM] = (proj[:, IN_CB:IN_CC] * y).astype(BF16)
    cnew_ref[0] = eu_ref[HALO + tm - (CONV_W - 1):HALO + tm, :]
    eu_ref[0:HALO, :] = eu_ref[tm:tm + HALO, :]

    pu = proj[:, IN_PU:IN_END]
    ep_ref[HALO:HALO + tm, :] = pu
    n_ext = tm + HALO
    s2_ref[1:n_ext, :] = ep_ref[1:n_ext, :] + ep_ref[0:n_ext - 1, :]
    s4_ref[3:n_ext, :] = s2_ref[3:n_ext, :] + s2_ref[1:n_ext - 2, :]
    s8_ref[7:n_ext, :] = s4_ref[7:n_ext, :] + s4_ref[3:n_ext - 4, :]
    s16 = s8_ref[HALO:n_ext, :] + s8_ref[HALO - 8:n_ext - 8, :]
    lane_grp = lax.broadcasted_iota(jnp.int32, (tm, POOL_DIM), 1) >> (POOL_GROUP_DIM.bit_length() - 1)
    wsum = jnp.where(lane_grp == 0, s2_ref[HALO:n_ext, :],
                     jnp.where(lane_grp == 1, s4_ref[HALO:n_ext, :],
                               jnp.where(lane_grp == 2, s8_ref[HALO:n_ext, :], s16)))
    win = jnp.left_shift(2, lane_grp)
    pos = pos0 + t * tm + lax.broadcasted_iota(jnp.int32, (tm, POOL_DIM), 0)
    cnt = jnp.minimum(win, pos + 1).astype(F32)
    dlt = (wsum / cnt - pu).astype(BF16)
    sp_ref[0, :, CONV_DIM:] = (_dot(dlt, poolw_ref[...]) * pools_ref[...]).astype(BF16)
    pnew_ref[0] = ep_ref[tm + 1:tm + HALO, :]
    ep_ref[0:HALO, :] = ep_ref[tm:tm + HALO, :]


def _mix_in(h, mod, layer, lw, cos_t, sin_t, conv_state, pool_state, *, tm, pos0, kr_transposed):
    n_b, n_t, d = h.shape
    grid = (n_b, n_t // tm)
    rows = lambda w: pl.BlockSpec((1, tm, w), lambda b, t: (b, t, 0))
    per_b = lambda r, w: pl.BlockSpec((1, r, w), lambda b, t: (b, 0, 0))
    weights = [lw[k] for k in ("norm_mix", "w_in_t", "q_norm", "w_uq", "kv_norm", "w_ukt",
                               "conv_w", "pool_bd", "pool_scale")]
    kr_spec = pl.BlockSpec((1, ROPE_DIM, tm), lambda b, t: (b, 0, t)) if kr_transposed else rows(ROPE_DIM)
    kr_shape = (n_b, ROPE_DIM, n_t) if kr_transposed else (n_b, n_t, ROPE_DIM)
    in_specs = [rows(d), per_b(N_ADA, d)] + [_layer_spec(w, layer) for w in weights] + [
        pl.BlockSpec((tm, LANES), lambda b, t: (t, 0)), pl.BlockSpec((tm, LANES), lambda b, t: (t, 0)),
        per_b(CONV_W - 1, CONV_DIM), per_b(POOL_MAX - 1, POOL_DIM),
    ]
    out_specs = [
        pl.BlockSpec((1, MLA_HEADS, tm, KEY_DIM), lambda b, t: (b, 0, t, 0)),
        rows(KEY_DIM), rows(KV_RANK), kr_spec, rows(CONV_DIM + POOL_DIM),
        per_b(CONV_W - 1, CONV_DIM), per_b(POOL_MAX - 1, POOL_DIM),
    ]
    out_shape = [
        jax.ShapeDtypeStruct((n_b, MLA_HEADS, n_t, KEY_DIM), BF16),
        jax.ShapeDtypeStruct((n_b, n_t, KEY_DIM), BF16),
        jax.ShapeDtypeStruct((n_b, n_t, KV_RANK), F32),
        jax.ShapeDtypeStruct(kr_shape, F32),
        jax.ShapeDtypeStruct((n_b, n_t, CONV_DIM + POOL_DIM), BF16),
        jax.ShapeDtypeStruct((n_b, CONV_W - 1, CONV_DIM), F32),
        jax.ShapeDtypeStruct((n_b, POOL_MAX - 1, POOL_DIM), F32),
    ]
    ext = pltpu.VMEM((tm + HALO, POOL_DIM), F32)
    return pl.pallas_call(
        functools.partial(_mix_in_kernel, tm=tm, pos0=pos0, kr_transposed=kr_transposed),
        grid=grid,
        in_specs=in_specs,
        out_specs=out_specs,
        out_shape=out_shape,
        scratch_shapes=[ext, ext, ext, ext, ext],
        compiler_params=pltpu.CompilerParams(
            dimension_semantics=("arbitrary", "arbitrary"), vmem_limit_bytes=VMEM_LIMIT),
        name="mix_in",
    )(h, mod, *weights, cos_t, sin_t, conv_state, pool_state)


def _attn_kernel(*refs, tq, tk, tk_past, past_len, group_rows):
    refs = list(refs)
    q_ref, knew_ref = refs.pop(0), refs.pop(0)
    if past_len:
        ckv_past_ref, krt_past_ref = refs.pop(0), refs.pop(0)
    wuv_ref, o_ref, m_ref, acc_ref = refs[:4]
    s_ref = None if past_len else refs[4]

    i = pl.program_id(1)
    m_rows = MLA_HEADS * tq
    n_groups = m_rows // group_rows
    m_ref[...] = jnp.full(m_ref.shape, NEG_BIG, F32)
    acc_ref[...] = jnp.zeros(acc_ref.shape, F32)

    def lane_tile(x, width):
        return x[:, :width] if width < LANES else jnp.concatenate([x] * (width // LANES), axis=1)

    def group(g):
        return slice(g * group_rows, (g + 1) * group_rows)

    def scores(g, kblk):
        if group_rows >= tq:
            heads = group_rows // tq
            q = q_ref[0, g * heads:(g + 1) * heads].reshape(group_rows, KEY_DIM)
        else:
            per_head = tq // group_rows
            r0 = (g % per_head) * group_rows
            q = q_ref[0, g // per_head, r0:r0 + group_rows, :]
        return lax.dot_general(q, kblk, (((1,), (1,)), ((), ())), preferred_element_type=F32)

    def absorb(g, s, kblk, mask):
        width = kblk.shape[0]
        rows = group(g)
        if mask is not None:
            s = jnp.where(mask, s, NEG_BIG)
        m_prev = m_ref[rows, :]
        m_new = jnp.maximum(m_prev, jnp.max(s, axis=1, keepdims=True))
        alpha = jnp.exp2(m_prev - m_new)
        p = jnp.exp2(s - lane_tile(m_new, width)).astype(BF16)
        acc_ref[rows, :] = lane_tile(alpha, KEY_DIM) * acc_ref[rows, :] + _dot(p, kblk)
        m_ref[rows, :] = m_new

    chunk_shift = CHUNK.bit_length() - 1

    def diag_mask(g, col0, width):
        r0, n_r = ((g * group_rows) % tq, group_rows) if group_rows < tq else (0, tq)
        if (col0 + width - 1) >> chunk_shift <= r0 >> chunk_shift:
            return "all"
        if col0 >> chunk_shift > (r0 + n_r - 1) >> chunk_shift:
            return "none"
        row = (lax.broadcasted_iota(jnp.int32, (group_rows, width), 0) + r0) & (tq - 1)
        col = lax.broadcasted_iota(jnp.int32, (group_rows, width), 1) + col0
        return (col >> chunk_shift) <= (row >> chunk_shift)

    def mask_arg(vis):
        return None if isinstance(vis, str) else vis

    def any_visible(vis):
        return not (isinstance(vis, str) and vis == "none")

    if past_len:
        def past_body(j, carry):
            start = pl.multiple_of(j * tk_past, tk_past)
            lat = ckv_past_ref[0, pl.ds(start, tk_past), :].astype(BF16)
            rot_t = krt_past_ref[0, :, pl.ds(start, tk_past)].astype(BF16)
            for g in range(n_groups):
                rows = group(g)
                q = q_ref[0].reshape(m_rows, KEY_DIM)[rows]
                s = lax.dot_general(q[:, :KV_RANK], lat, (((1,), (1,)), ((), ())), preferred_element_type=F32)
                s = s + _dot(q[:, KV_RANK:KV_RANK + ROPE_DIM], rot_t)
                m_prev = m_ref[rows, :]
                m_new = jnp.maximum(m_prev, jnp.max(s, axis=1, keepdims=True))
                alpha = jnp.exp2(m_prev - m_new)
                p = jnp.exp2(s - lane_tile(m_new, tk_past))
                denom = jnp.broadcast_to(jnp.sum(p, axis=1, keepdims=True), (group_rows, KEY_DIM - KV_RANK))
                acc_ref[rows, :] = lane_tile(alpha, KEY_DIM) * acc_ref[rows, :] + jnp.concatenate(
                    [_dot(p.astype(BF16), lat), denom], axis=-1)
                m_ref[rows, :] = m_new
            return carry
        lax.fori_loop(0, past_len // tk_past, past_body, 0)
        kblk = knew_ref[0]
        for g in range(n_groups):
            absorb(g, scores(g, kblk), kblk, mask_arg(diag_mask(g, 0, tq)))
    else:
        def new_block(j):
            return knew_ref[0, pl.ds(pl.multiple_of(j * tk, tk), tk), :]

        n_diag = tq // tk
        n_full = i * n_diag
        first = new_block(0)
        for g in range(n_groups):
            s_ref[group(g), :] = scores(g, first)

        def body(j, carry):
            kblk, knext = new_block(j), new_block(j + 1)
            for g in range(n_groups):
                s = s_ref[group(g), :]
                s_ref[group(g), :] = scores(g, knext)
                absorb(g, s, kblk, None)
            return carry
        lax.fori_loop(0, n_full, body, 0)

        for d in range(n_diag):
            kblk = new_block(n_full + d)
            knext = new_block(n_full + d + 1) if d + 1 < n_diag else None
            for g in range(n_groups):
                vis = diag_mask(g, d * tk, tk)
                if any_visible(vis):
                    s = s_ref[group(g), :]
                if knext is not None and any_visible(diag_mask(g, (d + 1) * tk, tk)):
                    s_ref[group(g), :] = scores(g, knext)
                if any_visible(vis):
                    absorb(g, s, kblk, mask_arg(vis))

    heads = []
    for hd in range(MLA_HEADS):
        acc = acc_ref[hd * tq:(hd + 1) * tq, :]
        denom = acc[:, KV_RANK + ROPE_DIM:KV_RANK + ROPE_DIM + 1]
        heads.append(_dot((acc[:, :KV_RANK] / denom).astype(BF16), wuv_ref[hd]))
    o_ref[0] = jnp.concatenate(heads, axis=-1).astype(BF16)


def _attn(q, k_new, past, layer, w_uv, *, tq, tk, tk_past, group_rows):
    n_b, _, n_t, _ = q.shape
    past_len = 0 if past is None else past[0].shape[2]
    assert (past_len % CHUNK == 0) and (tq % CHUNK == 0 or n_t == tq <= CHUNK)
    assert past_len == 0 or (n_t == tq and past_len % tk_past == 0)
    m_rows = MLA_HEADS * tq
    assert m_rows % group_rows == 0 and (group_rows % tq == 0 or tq % group_rows == 0)
    assert past_len or (tq % tk == 0 and tk % CHUNK == 0)
    scratch = [pltpu.VMEM((m_rows, LANES), F32), pltpu.VMEM((m_rows, KEY_DIM), F32)]
    if not past_len:
        scratch.append(pltpu.VMEM((m_rows, tk), F32))
    args = [q, k_new]
    specs = [pl.BlockSpec((1, MLA_HEADS, tq, KEY_DIM), lambda b, i: (b, 0, i, 0)),
             pl.BlockSpec((1, n_t, KEY_DIM), lambda b, i: (b, 0, 0), pipeline_mode=pl.Buffered(1))]
    if past_len:
        args += list(past)
        specs += [pl.BlockSpec((None, 1) + c.shape[2:], lambda b, i: (layer, b, 0, 0)) for c in past]
    args.append(w_uv)
    specs.append(_layer_spec(w_uv, layer))
    return pl.pallas_call(
        functools.partial(_attn_kernel, tq=tq, tk=tk, tk_past=tk_past, past_len=past_len,
                          group_rows=group_rows),
        grid=(n_b, n_t // tq),
        in_specs=specs,
        out_specs=pl.BlockSpec((1, tq, MLA_HEADS * V_DIM), lambda b, i: (b, i, 0)),
        out_shape=jax.ShapeDtypeStruct((n_b, n_t, MLA_HEADS * V_DIM), BF16),
        scratch_shapes=scratch,
        compiler_params=pltpu.CompilerParams(
            dimension_semantics=("arbitrary", "arbitrary"), vmem_limit_bytes=VMEM_LIMIT),
        name="attn",
    )(*args)


def _prep_weights(norm_ffn1, norm_mix, w_in, q_norm, w_uq, kv_norm, w_ukv, conv_w, pool_w, pool_scale,
                  w_out, norm_ffn2):
    n_l = w_in.shape[0]
    half = ROPE_DIM // 2
    packed = lambda w: jnp.concatenate([w, w[..., half:], w[..., :half]], axis=-1)
    vec = lambda v: v.reshape(n_l, 1, v.shape[-1])
    wq = w_uq.reshape(n_l, Q_RANK, MLA_HEADS, NOPE_DIM + ROPE_DIM)
    nope = wq[..., :NOPE_DIM].reshape(n_l, Q_RANK, MLA_HEADS * NOPE_DIM)
    rope_cols = [wq[:, :, hd, NOPE_DIM:] for hd in range(MLA_HEADS)]
    w_uq_p = jnp.concatenate([nope] + [packed(r) for r in rope_cols], axis=-1).astype(BF16)

    wkv = w_ukv.reshape(n_l, KV_RANK, MLA_HEADS, NOPE_DIM + V_DIM)
    w_ukt = jnp.transpose(wkv[..., :NOPE_DIM], (0, 2, 3, 1)).astype(BF16)
    w_uv = jnp.transpose(wkv[..., NOPE_DIM:], (0, 2, 1, 3)).astype(BF16)

    eye = jnp.eye(POOL_GROUPS, dtype=F32)
    pool_bd = (pool_w[:, :, :, None, :] * eye[None, :, None, :, None]).reshape(
        n_l, POOL_DIM, POOL_DIM).astype(BF16)

    return {
        "norm_ffn1": vec(norm_ffn1),
        "norm_mix": vec(norm_mix), "w_in_t": jnp.swapaxes(w_in, 1, 2).astype(BF16), "q_norm": vec(q_norm),
        "w_uq": w_uq_p,
        "kv_norm": vec(kv_norm), "w_ukt": w_ukt, "w_uv": w_uv,
        "conv_w": conv_w, "pool_bd": pool_bd, "pool_scale": vec(pool_scale),
        "w_out": w_out.astype(BF16),
        "norm_ffn2": vec(norm_ffn2),
    }


def _trunk_layer(h, mod, layer, lw, ffn1_w, ffn2_w, cast1, cast2, tables, conv_state, pool_state, past,
                 norm_final, *, bt, tt, tm, tq, group_rows, pos0):
    h1, made1 = _ffn(h, mod, layer, lw["norm_ffn1"], *ffn1_w, bt=bt, tt=tt, cast=cast1)
    if ffn2_w is None:
        ffn2_w = made1
    q, k_new, ckv, kr, sp, conv_new, pool_new = _mix_in(
        h1, mod, layer, lw, tables[0], tables[1], conv_state, pool_state, tm=tm, pos0=pos0,
        kr_transposed=past is None)
    a = _attn(q, k_new, past, layer, lw["w_uv"], tq=tq, tk=min(tq, 512), tk_past=2048,
              group_rows=group_rows)
    h3, made2 = _ffn(h1, mod, layer, lw["norm_ffn2"], *ffn2_w, bt=bt, tt=tt,
                     mix=(a, sp, lw["w_out"]), norm_final=norm_final, cast=cast2)
    return (h3, ckv, kr, conv_new, pool_new), ffn2_w, made2


def kernel(x_prompt, x_sample, c_prompt, c_sample, cache_ckv, cache_krope, state_conv, state_pool, w_ada, b_ada, norm_ffn1, w_ffn1_gu, w_ffn1_down, norm_mix, w_in, q_norm, w_uq, kv_norm, w_ukv, conv_w, pool_w, pool_scale, w_out, norm_ffn2, w_ffn2_gu, w_ffn2_down, norm_final):
    bp, seq, _ = x_prompt.shape
    bs, dec_seq, _ = x_sample.shape
    past_len = cache_ckv.shape[2]
    assert bp + bs <= ADA_ROWS

    c_all = jnp.concatenate(
        [c_prompt, c_sample, jnp.zeros((ADA_ROWS - bp - bs, D_MODEL), F32)], axis=0)
    mods = _ada(c_all, w_ada, b_ada).reshape(DEPTH, ADA_ROWS, N_ADA, D_MODEL)

    tm_p = 512
    tables_p = _rope_tables(seq, 0, tm_p)
    tables_s = _rope_tables(dec_seq, past_len, dec_seq)
    lw = _prep_weights(norm_ffn1, norm_mix, w_in, q_norm, w_uq, kv_norm, w_ukv, conv_w, pool_w,
                       pool_scale, w_out, norm_ffn2)
    past = (cache_ckv, jnp.swapaxes(cache_krope, 2, 3))

    hp, hs = x_prompt, x_sample
    outs_p, outs_s = [], []
    zero_conv = jnp.zeros((bp, CONV_W - 1, CONV_DIM), F32)
    zero_pool = jnp.zeros((bp, POOL_MAX - 1, POOL_DIM), F32)
    ffn1_w = (_cast_layer(w_ffn1_gu, 0), _cast_layer(w_ffn1_down, 0))
    for l in range(DEPTH):
        nf = norm_final if l == DEPTH - 1 else None
        cast2 = ((w_ffn1_gu, l + 1), (w_ffn1_down, l + 1)) if l + 1 < DEPTH else ()
        op, ffn2_w, next_ffn1_w = _trunk_layer(
            hp, mods[l, :bp], l, lw, ffn1_w, None, ((w_ffn2_gu, l), (w_ffn2_down, l)), cast2,
            tables_p, zero_conv, zero_pool, None, nf,
            bt=1, tt=tm_p, tm=tm_p, tq=1024, group_rows=512, pos0=0)
        os_, _, _ = _trunk_layer(
            hs, mods[l, bp:bp + bs], l, lw, ffn1_w, ffn2_w, (), (),
            tables_s, state_conv[l], state_pool[l], past, nf,
            bt=bs, tt=dec_seq, tm=dec_seq, tq=dec_seq, group_rows=MLA_HEADS * dec_seq, pos0=past_len)
        hp, hs, ffn1_w = op[0], os_[0], next_ffn1_w
        outs_p.append(op[1:])
        outs_s.append(os_[1:])

    stack = lambda outs, k: jnp.stack([o[k] for o in outs])
    return (hp, hs,
            stack(outs_p, 0), jnp.swapaxes(stack(outs_p, 1), 2, 3), stack(outs_p, 2), stack(outs_p, 3),
            stack(outs_s, 0), stack(outs_s, 1), stack(outs_s, 2), stack(outs_s, 3))
```

```python
import functools

import jax
import jax.numpy as jnp
from jax import lax
from jax.experimental import pallas as pl
from jax.experimental.pallas import tpu as pltpu

F32 = jnp.float32
BF16 = jnp.bfloat16

D_MODEL = 1024
DEPTH = 2
CHUNK = 64
EPS = 1e-6
N_ADA = 9
D_FF = 2816
MLA_HEADS = 4
Q_RANK = 256
KV_RANK = 128
NOPE_DIM = 128
ROPE_DIM = 64
V_DIM = 128
ROPE_BASE = 10000.0
ATTN_SCALE = (NOPE_DIM + ROPE_DIM) ** -0.5
LOG2_E = 1.4426950408889634
CONV_DIM = 256
CONV_W = 3
POOL_WINDOWS = (2, 4, 8, 16)
POOL_GROUPS = 4
POOL_DIM = 256
POOL_GROUP_DIM = POOL_DIM // POOL_GROUPS
POOL_MAX = 16

LANES = 128
BF16_SUBLANES = 16
KEY_DIM = 2 * LANES
FF_CHUNK = 256
ADA_ROWS = 16
ADA_COLS = 1024
HALO = 16
NEG_BIG = -1e30
VMEM_LIMIT = 56 * 1024 * 1024

IN_CQ, IN_CKV, IN_KR, IN_CB, IN_CC, IN_CV, IN_PU, IN_END = 0, 256, 384, 512, 768, 1024, 1280, 1536
UQ_NOPE, UQ_ROPE, UQ_END = 0, 512, 1024


def _rms(x, g):
    return x * lax.rsqrt(jnp.mean(x * x, axis=-1, keepdims=True) + EPS) * g


def _silu(x):
    return x * jax.nn.sigmoid(x)


def _dot(a, b):
    return jnp.dot(a, b, preferred_element_type=F32)


def _const_spec(shape):
    nd = len(shape)
    return pl.BlockSpec(shape, lambda *_: (0,) * nd, pipeline_mode=pl.Buffered(1))


def _layer_spec(stacked, layer):
    rest = stacked.shape[1:]
    return pl.BlockSpec((None,) + rest, lambda *_: (layer,) + (0,) * len(rest),
                        pipeline_mode=pl.Buffered(1))


def _ada_kernel(c_ref, w_ref, b_ref, o_ref):
    a = _silu(c_ref[...]).astype(BF16)
    o_ref[0] = _dot(a, w_ref[0].astype(BF16)) + b_ref[0]


def _ada(c_all, w_ada, b_ada):
    n_l, _, n_cols = w_ada.shape
    return pl.pallas_call(
        _ada_kernel,
        grid=(n_l, n_cols // ADA_COLS),
        in_specs=[
            pl.BlockSpec((ADA_ROWS, D_MODEL), lambda l, j: (0, 0)),
            pl.BlockSpec((1, D_MODEL, ADA_COLS), lambda l, j: (l, 0, j)),
            pl.BlockSpec((1, 1, ADA_COLS), lambda l, j: (l, 0, j)),
        ],
        out_specs=pl.BlockSpec((1, ADA_ROWS, ADA_COLS), lambda l, j: (l, 0, j)),
        out_shape=jax.ShapeDtypeStruct((n_l, ADA_ROWS, n_cols), F32),
        compiler_params=pltpu.CompilerParams(
            dimension_semantics=("arbitrary", "arbitrary"), vmem_limit_bytes=VMEM_LIMIT),
        name="ada",
    )(c_all, w_ada, b_ada.reshape(n_l, 1, n_cols))


def _cast_kernel(src_ref, dst_ref):
    dst_ref[...] = src_ref[...].astype(BF16)


def _cast_layer(stacked, layer, n_steps=8):
    _, n_rows, n_cols = stacked.shape
    slab = n_rows // n_steps
    assert n_rows % n_steps == 0 and slab % BF16_SUBLANES == 0
    return pl.pallas_call(
        _cast_kernel,
        grid=(n_steps,),
        in_specs=[pl.BlockSpec((None, slab, n_cols), lambda t: (layer, t, 0))],
        out_specs=pl.BlockSpec((None, slab, n_cols), lambda t: (0, t, 0)),
        out_shape=jax.ShapeDtypeStruct((1, n_rows, n_cols), BF16),
        compiler_params=pltpu.CompilerParams(
            dimension_semantics=("arbitrary",), vmem_limit_bytes=VMEM_LIMIT),
        name="cast_layer",
    )(stacked)


def _rope_table_kernel(inv_ref, sign_ref, cos_ref, sin_ref, cos_row_ref, sin_row_ref, *, pos0, tm):
    t = pl.program_id(0)

    @pl.when(t == 0)
    def _():
        row = lax.broadcasted_iota(jnp.int32, (tm, LANES), 0).astype(F32)
        ang = row * inv_ref[...]
        cos_row_ref[...] = jnp.cos(ang)
        sin_row_ref[...] = jnp.sin(ang)

    start = (pos0 + t * tm).astype(F32) * inv_ref[...]
    cos_s, sin_s = jnp.cos(start), jnp.sin(start)
    cos_r, sin_r = cos_row_ref[...], sin_row_ref[...]
    cos_ref[...] = cos_s * cos_r - sin_s * sin_r
    sin_ref[...] = (sin_s * cos_r + cos_s * sin_r) * sign_ref[...]


def _rope_tables(n_rows, pos0, tm):
    half = ROPE_DIM // 2
    inv = ROPE_BASE ** (-jnp.arange(half, dtype=F32) / half)
    inv = jnp.tile(inv, LANES // half).reshape(1, LANES)
    sign = jnp.tile(jnp.concatenate([-jnp.ones((half,), F32), jnp.ones((half,), F32)]),
                    LANES // ROPE_DIM).reshape(1, LANES)
    return pl.pallas_call(
        functools.partial(_rope_table_kernel, pos0=pos0, tm=tm),
        grid=(n_rows // tm,),
        in_specs=[_const_spec((1, LANES)), _const_spec((1, LANES))],
        out_specs=[pl.BlockSpec((tm, LANES), lambda i: (i, 0))] * 2,
        out_shape=[jax.ShapeDtypeStruct((n_rows, LANES), F32)] * 2,
        scratch_shapes=[pltpu.VMEM((tm, LANES), F32)] * 2,
        compiler_params=pltpu.CompilerParams(dimension_semantics=("arbitrary",)),
        name="rope_tables",
    )(inv, sign)


def _ffn_kernel(*refs, has_mix, final_norm, n_cast):
    refs = list(refs)
    h_ref = refs.pop(0)
    if has_mix:
        a_ref, sp_ref, wo_ref = refs.pop(0), refs.pop(0), refs.pop(0)
    mod_ref, ng_ref, wgu_ref, wd_ref = refs.pop(0), refs.pop(0), refs.pop(0), refs.pop(0)
    if final_norm:
        nf_ref = refs.pop(0)
    cast_in = [refs.pop(0) for _ in range(n_cast)]
    o_ref = refs.pop(0)
    for src, dst in zip(cast_in, refs):
        dst[...] = src[...].astype(BF16)

    bt, tt, d = h_ref.shape
    m = bt * tt
    x = h_ref[...]
    mod = mod_ref[...]
    row0 = 0
    if has_mix:
        mix_w = a_ref.shape[-1]
        mix = _dot(a_ref[...].reshape(m, mix_w), wo_ref[0:mix_w, :])
        mix = mix + _dot(sp_ref[...].reshape(m, sp_ref.shape[-1]), wo_ref[mix_w:, :])
        x = x + mod[:, 5:6, :] * mix.reshape(bt, tt, d)
        row0 = 6
    shift, scale, gate = mod[:, row0:row0 + 1, :], mod[:, row0 + 1:row0 + 2, :], mod[:, row0 + 2:row0 + 3, :]
    n = _rms(x, ng_ref[...] * (1.0 + scale)) + shift
    nb = n.reshape(m, d).astype(BF16)
    acc = jnp.zeros((m, d), F32)
    for c in range(D_FF // FF_CHUNK):
        g = _dot(nb, wgu_ref[:, c * FF_CHUNK:(c + 1) * FF_CHUNK])
        u = _dot(nb, wgu_ref[:, D_FF + c * FF_CHUNK:D_FF + (c + 1) * FF_CHUNK])
        act = (_silu(g) * u).astype(BF16)
        acc = acc + _dot(act, wd_ref[c * FF_CHUNK:(c + 1) * FF_CHUNK, :])
    y = x + 0.5 * gate * acc.reshape(bt, tt, d)
    if final_norm:
        y = _rms(y, nf_ref[...])
    o_ref[...] = y


def _ffn(h, mod, layer, norm_g, w_gu, w_down, *, bt, tt, mix=None, norm_final=None, cast=()):
    n_b, n_t, d = h.shape
    grid = (n_b // bt, n_t // tt)
    tile = lambda w: pl.BlockSpec((bt, tt, w), lambda b, t: (b, t, 0))
    args, specs = [h], [tile(d)]
    if mix is not None:
        a, sp, w_out = mix
        args += [a, sp, w_out]
        specs += [tile(a.shape[-1]), tile(sp.shape[-1]), _layer_spec(w_out, layer)]
    args += [mod, norm_g, w_gu, w_down]
    specs += [pl.BlockSpec((bt, N_ADA, d), lambda b, t: (b, 0, 0)), _layer_spec(norm_g, layer),
              _layer_spec(w_gu, 0), _layer_spec(w_down, 0)]
    if norm_final is not None:
        args.append(norm_final.reshape(1, d))
        specs.append(_const_spec((1, d)))
    out_specs, out_shape = [tile(d)], [jax.ShapeDtypeStruct(h.shape, F32)]
    assert not cast or grid[0] == 1
    for stacked, src_layer in cast:
        _, n_rows, n_cols = stacked.shape
        share = next(k for k in (1, 2, 4, 8) if grid[1] % k == 0 and n_rows % (grid[1] // k) == 0
                     and (n_rows // (grid[1] // k)) % BF16_SUBLANES == 0)
        slab = n_rows // (grid[1] // share)
        args.append(stacked)
        specs.append(pl.BlockSpec((None, slab, n_cols),
                                  lambda b, t, _l=src_layer, _s=share: (_l, t // _s, 0)))
        out_specs.append(pl.BlockSpec((None, slab, n_cols), lambda b, t, _s=share: (0, t // _s, 0)))
        out_shape.append(jax.ShapeDtypeStruct((1, n_rows, n_cols), BF16))
    outs = pl.pallas_call(
        functools.partial(_ffn_kernel, has_mix=mix is not None, final_norm=norm_final is not None,
                          n_cast=len(cast)),
        grid=grid,
        in_specs=specs,
        out_specs=out_specs,
        out_shape=out_shape,
        compiler_params=pltpu.CompilerParams(
            dimension_semantics=("arbitrary", "arbitrary"), vmem_limit_bytes=VMEM_LIMIT),
        name="ffn_mix" if mix is not None else "ffn",
    )(*args)
    return outs[0], tuple(outs[1:])


def _mix_in_kernel(h_ref, mod_ref, ng_ref, wint_ref, qn_ref, wuq_ref, kvn_ref, wukt_ref,
                   convw_ref, poolw_ref, pools_ref, cos_ref, sin_ref, cstate_ref, pstate_ref,
                   q_ref, k_ref, ckv_ref, kr_ref, sp_ref, cnew_ref, pnew_ref,
                   eu_ref, ep_ref, s2_ref, s4_ref, s8_ref, *, tm, pos0, kr_transposed):
    t = pl.program_id(1)
    x = h_ref[0]
    mod = mod_ref[0]
    n = _rms(x, ng_ref[...] * (1.0 + mod[4:5, :])) + mod[3:4, :]
    nb = n.astype(BF16)
    half = ROPE_DIM // 2
    kr0 = Q_RANK + KV_RANK
    nt = lambda w: lax.dot_general(nb, w, (((1,), (1,)), ((), ())), preferred_element_type=F32)
    head = jnp.concatenate([wint_ref[0:kr0 + ROPE_DIM, :], wint_ref[kr0 + half:kr0 + ROPE_DIM, :],
                            wint_ref[kr0:kr0 + half, :]], axis=0)
    proj = jnp.concatenate([nt(head), nt(wint_ref[kr0 + ROPE_DIM:, :])], axis=-1)
    cos = cos_ref[...]
    sin = sin_ref[...]
    lane = lax.broadcasted_iota(jnp.int32, (tm, LANES), 1)

    def rope(packed):
        return packed * cos + pltpu.roll(packed, ROPE_DIM, axis=1) * sin

    qn = _rms(proj[:, IN_CQ:IN_CKV], qn_ref[...]).astype(BF16)
    qall = _dot(qn, wuq_ref[...])
    for hd in range(MLA_HEADS):
        q_nope = qall[:, UQ_NOPE + hd * NOPE_DIM:UQ_NOPE + (hd + 1) * NOPE_DIM].astype(BF16)
        q_lat = _dot(q_nope, wukt_ref[hd])
        q_rope = jnp.where(lane < ROPE_DIM, rope(qall[:, UQ_ROPE + hd * LANES:UQ_ROPE + (hd + 1) * LANES]), 0.0)
        q_ref[0, hd] = (jnp.concatenate([q_lat, q_rope], axis=-1) * (ATTN_SCALE * LOG2_E)).astype(BF16)

    ckv = _rms(proj[:, IN_CKV:IN_KR], kvn_ref[...])
    kr = rope(proj[:, IN_KR:IN_CB])
    ckv_ref[0] = ckv
    kr_ref[0] = jnp.transpose(kr)[:ROPE_DIM, :] if kr_transposed else kr[:, :ROPE_DIM]
    k_ref[0] = jnp.concatenate([ckv, jnp.where(lane < ROPE_DIM, kr, 1.0)], axis=-1).astype(BF16)

    @pl.when(t == 0)
    def _():
        eu_ref[0:HALO - (CONV_W - 1), :] = jnp.zeros((HALO - (CONV_W - 1), CONV_DIM), F32)
        eu_ref[HALO - (CONV_W - 1):HALO, :] = cstate_ref[0]
        ep_ref[0:1, :] = jnp.zeros((1, POOL_DIM), F32)
        ep_ref[1:HALO, :] = pstate_ref[0]

    u = proj[:, IN_CC:IN_CV] * proj[:, IN_CV:IN_PU]
    eu_ref[HALO:HALO + tm, :] = u
    cw = convw_ref[...]
    y = u * cw[CONV_W - 1:CONV_W, :]
    for j in range(CONV_W - 1):
        back = CONV_W - 1 - j
        y = y + eu_ref[HALO - back:HALO - back + tm, :] * cw[j:j + 1, :]
    sp_ref[0, :, 0:CONV_DIM] = (proj[:, IN_CB:IN_CC] * y).astype(BF16)
    cnew_ref[0] = eu_ref[HALO + tm - (CONV_W - 1):HALO + tm, :]
    eu_ref[0:HALO, :] = eu_ref[tm:tm + HALO, :]

    pu = proj[:, IN_PU:IN_END]
    ep_ref[HALO:HALO + tm, :] = pu
    n_ext = tm + HALO
    s2_ref[1:n_ext, :] = ep_ref[1:n_ext, :] + ep_ref[0:n_ext - 1, :]
    s4_ref[3:n_ext, :] = s2_ref[3:n_ext, :] + s2_ref[1:n_ext - 2, :]
    s8_ref[7:n_ext, :] = s4_ref[7:n_ext, :] + s4_ref[3:n_ext - 4, :]
    s16 = s8_ref[HALO:n_ext, :] + s8_ref[HALO - 8:n_ext - 8, :]
    lane_grp = lax.broadcasted_iota(jnp.int32, (tm, POOL_DIM), 1) >> (POOL_GROUP_DIM.bit_length() - 1)
    wsum = jnp.where(lane_grp == 0, s2_ref[HALO:n_ext, :],
                     jnp.where(lane_grp == 1, s4_ref[HALO:n_ext, :],
                               jnp.where(lane_grp == 2, s8_ref[HALO:n_ext, :], s16)))
    assert POOL_WINDOWS == tuple(2 << g for g in range(POOL_GROUPS))
    win = jnp.left_shift(2, lane_grp)
    pos = pos0 + t * tm + lax.broadcasted_iota(jnp.int32, (tm, POOL_DIM), 0)
    cnt = jnp.minimum(win, pos + 1).astype(F32)
    dlt = (wsum / cnt - pu).astype(BF16)
    sp_ref[0, :, CONV_DIM:] = (_dot(dlt, poolw_ref[...]) * pools_ref[...]).astype(BF16)
    pnew_ref[0] = ep_ref[tm + 1:tm + HALO, :]
    ep_ref[0:HALO, :] = ep_ref[tm:tm + HALO, :]


def _mix_in(h, mod, layer, lw, cos_t, sin_t, conv_state, pool_state, *, tm, pos0, kr_transposed):
    n_b, n_t, d = h.shape
    grid = (n_b, n_t // tm)
    rows = lambda w: pl.BlockSpec((1, tm, w), lambda b, t: (b, t, 0))
    per_b = lambda r, w: pl.BlockSpec((1, r, w), lambda b, t: (b, 0, 0))
    weights = [lw[k] for k in ("norm_mix", "w_in_t", "q_norm", "w_uq", "kv_norm", "w_ukt",
                               "conv_w", "pool_bd", "pool_scale")]
    kr_spec = pl.BlockSpec((1, ROPE_DIM, tm), lambda b, t: (b, 0, t)) if kr_transposed else rows(ROPE_DIM)
    kr_shape = (n_b, ROPE_DIM, n_t) if kr_transposed else (n_b, n_t, ROPE_DIM)
    in_specs = [rows(d), per_b(N_ADA, d)] + [_layer_spec(w, layer) for w in weights] + [
        pl.BlockSpec((tm, LANES), lambda b, t: (t, 0)), pl.BlockSpec((tm, LANES), lambda b, t: (t, 0)),
        per_b(CONV_W - 1, CONV_DIM), per_b(POOL_MAX - 1, POOL_DIM),
    ]
    out_specs = [
        pl.BlockSpec((1, MLA_HEADS, tm, KEY_DIM), lambda b, t: (b, 0, t, 0)),
        rows(KEY_DIM), rows(KV_RANK), kr_spec, rows(CONV_DIM + POOL_DIM),
        per_b(CONV_W - 1, CONV_DIM), per_b(POOL_MAX - 1, POOL_DIM),
    ]
    out_shape = [
        jax.ShapeDtypeStruct((n_b, MLA_HEADS, n_t, KEY_DIM), BF16),
        jax.ShapeDtypeStruct((n_b, n_t, KEY_DIM), BF16),
        jax.ShapeDtypeStruct((n_b, n_t, KV_RANK), F32),
        jax.ShapeDtypeStruct(kr_shape, F32),
        jax.ShapeDtypeStruct((n_b, n_t, CONV_DIM + POOL_DIM), BF16),
        jax.ShapeDtypeStruct((n_b, CONV_W - 1, CONV_DIM), F32),
        jax.ShapeDtypeStruct((n_b, POOL_MAX - 1, POOL_DIM), F32),
    ]
    ext = pltpu.VMEM((tm + HALO, POOL_DIM), F32)
    return pl.pallas_call(
        functools.partial(_mix_in_kernel, tm=tm, pos0=pos0, kr_transposed=kr_transposed),
        grid=grid,
        in_specs=in_specs,
        out_specs=out_specs,
        out_shape=out_shape,
        scratch_shapes=[ext, ext, ext, ext, ext],
        compiler_params=pltpu.CompilerParams(
            dimension_semantics=("arbitrary", "arbitrary"), vmem_limit_bytes=VMEM_LIMIT),
        name="mix_in",
    )(h, mod, *weights, cos_t, sin_t, conv_state, pool_state)


def _attn_kernel(*refs, tq, tk, tk_past, past_len, group_rows):
    refs = list(refs)
    q_ref, knew_ref = refs.pop(0), refs.pop(0)
    if past_len:
        ckv_past_ref, krt_past_ref = refs.pop(0), refs.pop(0)
    wuv_ref, o_ref, m_ref, acc_ref = refs[:4]
    s_ref = None if past_len else refs[4]

    i = pl.program_id(1)
    m_rows = MLA_HEADS * tq
    n_groups = m_rows // group_rows
    m_ref[...] = jnp.full(m_ref.shape, NEG_BIG, F32)
    acc_ref[...] = jnp.zeros(acc_ref.shape, F32)

    def lane_tile(x, width):
        return x[:, :width] if width < LANES else jnp.concatenate([x] * (width // LANES), axis=1)

    def group(g):
        return slice(g * group_rows, (g + 1) * group_rows)

    def scores(g, kblk):
        if group_rows >= tq:
            heads = group_rows // tq
            q = q_ref[0, g * heads:(g + 1) * heads].reshape(group_rows, KEY_DIM)
        else:
            per_head = tq // group_rows
            r0 = (g % per_head) * group_rows
            q = q_ref[0, g // per_head, r0:r0 + group_rows, :]
        return lax.dot_general(q, kblk, (((1,), (1,)), ((), ())), preferred_element_type=F32)

    def absorb(g, s, kblk, mask):
        width = kblk.shape[0]
        rows = group(g)
        if mask is not None:
            s = jnp.where(mask, s, NEG_BIG)
        m_prev = m_ref[rows, :]
        m_new = jnp.maximum(m_prev, jnp.max(s, axis=1, keepdims=True))
        alpha = jnp.exp2(m_prev - m_new)
        p = jnp.exp2(s - lane_tile(m_new, width)).astype(BF16)
        acc_ref[rows, :] = lane_tile(alpha, KEY_DIM) * acc_ref[rows, :] + _dot(p, kblk)
        m_ref[rows, :] = m_new

    chunk_shift = CHUNK.bit_length() - 1

    def diag_mask(g, col0, width):
        r0, n_r = ((g * group_rows) % tq, group_rows) if group_rows < tq else (0, tq)
        if (col0 + width - 1) >> chunk_shift <= r0 >> chunk_shift:
            return "all"
        if col0 >> chunk_shift > (r0 + n_r - 1) >> chunk_shift:
            return "none"
        row = (lax.broadcasted_iota(jnp.int32, (group_rows, width), 0) + r0) & (tq - 1)
        col = lax.broadcasted_iota(jnp.int32, (group_rows, width), 1) + col0
        return (col >> chunk_shift) <= (row >> chunk_shift)

    def mask_arg(vis):
        return None if isinstance(vis, str) else vis

    def any_visible(vis):
        return not (isinstance(vis, str) and vis == "none")

    if past_len:
        def past_body(j, carry):
            start = pl.multiple_of(j * tk_past, tk_past)
            lat = ckv_past_ref[0, pl.ds(start, tk_past), :].astype(BF16)
            rot_t = krt_past_ref[0, :, pl.ds(start, tk_past)].astype(BF16)
            for g in range(n_groups):
                rows = group(g)
                q = q_ref[0].reshape(m_rows, KEY_DIM)[rows]
                s = lax.dot_general(q[:, :KV_RANK], lat, (((1,), (1,)), ((), ())), preferred_element_type=F32)
                s = s + _dot(q[:, KV_RANK:KV_RANK + ROPE_DIM], rot_t)
                m_prev = m_ref[rows, :]
                m_new = jnp.maximum(m_prev, jnp.max(s, axis=1, keepdims=True))
                alpha = jnp.exp2(m_prev - m_new)
                p = jnp.exp2(s - lane_tile(m_new, tk_past))
                denom = jnp.broadcast_to(jnp.sum(p, axis=1, keepdims=True), (group_rows, KEY_DIM - KV_RANK))
                acc_ref[rows, :] = lane_tile(alpha, KEY_DIM) * acc_ref[rows, :] + jnp.concatenate(
                    [_dot(p.astype(BF16), lat), denom], axis=-1)
                m_ref[rows, :] = m_new
            return carry
        lax.fori_loop(0, past_len // tk_past, past_body, 0)
        kblk = knew_ref[0]
        for g in range(n_groups):
            absorb(g, scores(g, kblk), kblk, mask_arg(diag_mask(g, 0, tq)))
    else:
        def new_block(j):
            return knew_ref[0, pl.ds(pl.multiple_of(j * tk, tk), tk), :]

        n_diag = tq // tk
        n_full = i * n_diag
        first = new_block(0)
        for g in range(n_groups):
            s_ref[group(g), :] = scores(g, first)

        def body(j, carry):
            kblk, knext = new_block(j), new_block(j + 1)
            for g in range(n_groups):
                s = s_ref[group(g), :]
                s_ref[group(g), :] = scores(g, knext)
                absorb(g, s, kblk, None)
            return carry
        lax.fori_loop(0, n_full, body, 0)

        for d in range(n_diag):
            kblk = new_block(n_full + d)
            knext = new_block(n_full + d + 1) if d + 1 < n_diag else None
            for g in range(n_groups):
                vis = diag_mask(g, d * tk, tk)
                if any_visible(vis):
                    s = s_ref[group(g), :]
                if knext is not None and any_visible(diag_mask(g, (d + 1) * tk, tk)):
                    s_ref[group(g), :] = scores(g, knext)
                if any_visible(vis):
                    absorb(g, s, kblk, mask_arg(vis))

    heads = []
    for hd in range(MLA_HEADS):
        acc = acc_ref[hd * tq:(hd + 1) * tq, :]
        denom = acc[:, KV_RANK + ROPE_DIM:KV_RANK + ROPE_DIM + 1]
        heads.append(_dot((acc[:, :KV_RANK] / denom).astype(BF16), wuv_ref[hd]))
    o_ref[0] = jnp.concatenate(heads, axis=-1).astype(BF16)


def _attn(q, k_new, past, layer, w_uv, *, tq, tk, tk_past, group_rows):
    n_b, _, n_t, _ = q.shape
    past_len = 0 if past is None else past[0].shape[2]
    assert (past_len % CHUNK == 0) and (tq % CHUNK == 0 or n_t == tq <= CHUNK)
    assert past_len == 0 or (n_t == tq and past_len % tk_past == 0)
    m_rows = MLA_HEADS * tq
    assert m_rows % group_rows == 0 and (group_rows % tq == 0 or tq % group_rows == 0)
    assert past_len or (tq % tk == 0 and tk % CHUNK == 0)
    scratch = [pltpu.VMEM((m_rows, LANES), F32), pltpu.VMEM((m_rows, KEY_DIM), F32)]
    if not past_len:
        scratch.append(pltpu.VMEM((m_rows, tk), F32))
    args = [q, k_new]
    specs = [pl.BlockSpec((1, MLA_HEADS, tq, KEY_DIM), lambda b, i: (b, 0, i, 0)),
             pl.BlockSpec((1, n_t, KEY_DIM), lambda b, i: (b, 0, 0), pipeline_mode=pl.Buffered(1))]
    if past_len:
        args += list(past)
        specs += [pl.BlockSpec((None, 1) + c.shape[2:], lambda b, i: (layer, b, 0, 0)) for c in past]
    args.append(w_uv)
    specs.append(_layer_spec(w_uv, layer))
    return pl.pallas_call(
        functools.partial(_attn_kernel, tq=tq, tk=tk, tk_past=tk_past, past_len=past_len,
                          group_rows=group_rows),
        grid=(n_b, n_t // tq),
        in_specs=specs,
        out_specs=pl.BlockSpec((1, tq, MLA_HEADS * V_DIM), lambda b, i: (b, i, 0)),
        out_shape=jax.ShapeDtypeStruct((n_b, n_t, MLA_HEADS * V_DIM), BF16),
        scratch_shapes=scratch,
        compiler_params=pltpu.CompilerParams(
            dimension_semantics=("arbitrary", "arbitrary"), vmem_limit_bytes=VMEM_LIMIT),
        name="attn",
    )(*args)


def _prep_weights(norm_ffn1, norm_mix, w_in, q_norm, w_uq, kv_norm, w_ukv, conv_w, pool_w, pool_scale,
                  w_out, norm_ffn2):
    n_l = w_in.shape[0]
    half = ROPE_DIM // 2
    packed = lambda w: jnp.concatenate([w, w[..., half:], w[..., :half]], axis=-1)
    vec = lambda v: v.reshape(n_l, 1, v.shape[-1])
    wq = w_uq.reshape(n_l, Q_RANK, MLA_HEADS, NOPE_DIM + ROPE_DIM)
    nope = wq[..., :NOPE_DIM].reshape(n_l, Q_RANK, MLA_HEADS * NOPE_DIM)
    rope_cols = [wq[:, :, hd, NOPE_DIM:] for hd in range(MLA_HEADS)]
    w_uq_p = jnp.concatenate([nope] + [packed(r) for r in rope_cols], axis=-1).astype(BF16)

    wkv = w_ukv.reshape(n_l, KV_RANK, MLA_HEADS, NOPE_DIM + V_DIM)
    w_ukt = jnp.transpose(wkv[..., :NOPE_DIM], (0, 2, 3, 1)).astype(BF16)
    w_uv = jnp.transpose(wkv[..., NOPE_DIM:], (0, 2, 1, 3)).astype(BF16)

    eye = jnp.eye(POOL_GROUPS, dtype=F32)
    pool_bd = (pool_w[:, :, :, None, :] * eye[None, :, None, :, None]).reshape(
        n_l, POOL_DIM, POOL_DIM).astype(BF16)

    return {
        "norm_ffn1": vec(norm_ffn1),
        "norm_mix": vec(norm_mix), "w_in_t": jnp.swapaxes(w_in, 1, 2).astype(BF16), "q_norm": vec(q_norm),
        "w_uq": w_uq_p,
        "kv_norm": vec(kv_norm), "w_ukt": w_ukt, "w_uv": w_uv,
        "conv_w": conv_w, "pool_bd": pool_bd, "pool_scale": vec(pool_scale),
        "w_out": w_out.astype(BF16),
        "norm_ffn2": vec(norm_ffn2),
    }


def _trunk_layer(h, mod, layer, lw, ffn1_w, ffn2_w, cast1, cast2, tables, conv_state, pool_state, past,
                 norm_final, *, bt, tt, tm, tq, group_rows, pos0):
    h1, made1 = _ffn(h, mod, layer, lw["norm_ffn1"], *ffn1_w, bt=bt, tt=tt, cast=cast1)
    if ffn2_w is None:
        ffn2_w = made1
    q, k_new, ckv, kr, sp, conv_new, pool_new = _mix_in(
        h1, mod, layer, lw, tables[0], tables[1], conv_state, pool_state, tm=tm, pos0=pos0,
        kr_transposed=past is None)
    a = _attn(q, k_new, past, layer, lw["w_uv"], tq=tq, tk=min(tq, 512), tk_past=past[0].shape[2] if past is not None else 0,
              group_rows=group_rows)
    h3, made2 = _ffn(h1, mod, layer, lw["norm_ffn2"], *ffn2_w, bt=bt, tt=tt,
                     mix=(a, sp, lw["w_out"]), norm_final=norm_final, cast=cast2)
    return (h3, ckv, kr, conv_new, pool_new), ffn2_w, made2


def kernel(x_prompt, x_sample, c_prompt, c_sample, cache_ckv, cache_krope, state_conv, state_pool, w_ada, b_ada, norm_ffn1, w_ffn1_gu, w_ffn1_down, norm_mix, w_in, q_norm, w_uq, kv_norm, w_ukv, conv_w, pool_w, pool_scale, w_out, norm_ffn2, w_ffn2_gu, w_ffn2_down, norm_final):
    bp, seq, _ = x_prompt.shape
    bs, dec_seq, _ = x_sample.shape
    past_len = cache_ckv.shape[2]
    assert bp + bs <= ADA_ROWS

    c_all = jnp.concatenate(
        [c_prompt, c_sample, jnp.zeros((ADA_ROWS - bp - bs, D_MODEL), F32)], axis=0)
    mods = _ada(c_all, w_ada, b_ada).reshape(DEPTH, ADA_ROWS, N_ADA, D_MODEL)

    tm_p = 512
    tables_p = _rope_tables(seq, 0, tm_p)
    tables_s = _rope_tables(dec_seq, past_len, dec_seq)
    lw = _prep_weights(norm_ffn1, norm_mix, w_in, q_norm, w_uq, kv_norm, w_ukv, conv_w, pool_w,
                       pool_scale, w_out, norm_ffn2)
    past = (cache_ckv, jnp.swapaxes(cache_krope, 2, 3))

    hp, hs = x_prompt, x_sample
    outs_p, outs_s = [], []
    zero_conv = jnp.zeros((bp, CONV_W - 1, CONV_DIM), F32)
    zero_pool = jnp.zeros((bp, POOL_MAX - 1, POOL_DIM), F32)
    ffn1_w = (_cast_layer(w_ffn1_gu, 0), _cast_layer(w_ffn1_down, 0))
    for l in range(DEPTH):
        nf = norm_final if l == DEPTH - 1 else None
        cast2 = ((w_ffn1_gu, l + 1), (w_ffn1_down, l + 1)) if l + 1 < DEPTH else ()
        op, ffn2_w, next_ffn1_w = _trunk_layer(
            hp, mods[l, :bp], l, lw, ffn1_w, None, ((w_ffn2_gu, l), (w_ffn2_down, l)), cast2,
            tables_p, zero_conv, zero_pool, None, nf,
            bt=1, tt=tm_p, tm=tm_p, tq=1024, group_rows=512, pos0=0)
        os_, _, _ = _trunk_layer(
            hs, mods[l, bp:bp + bs], l, lw, ffn1_w, ffn2_w, (), (),
            tables_s, state_conv[l], state_pool[l], past, nf,
            bt=bs, tt=dec_seq, tm=dec_seq, tq=dec_seq, group_rows=MLA_HEADS * dec_seq, pos0=past_len)
        hp, hs, ffn1_w = op[0], os_[0], next_ffn1_w
        outs_p.append(op[1:])
        outs_s.append(os_[1:])

    stack = lambda outs, k: jnp.stack([o[k] for o in outs])
    return (hp, hs,
            stack(outs_p, 0), jnp.swapaxes(stack(outs_p, 1), 2, 3), stack(outs_p, 2), stack(outs_p, 3),
            stack(outs_s, 0), stack(outs_s, 1), stack(outs_s, 2), stack(outs_s, 3))
```

```python
import functools

import jax
import jax.numpy as jnp
from jax import lax
from jax.experimental import pallas as pl
from jax.experimental.pallas import tpu as pltpu

F32 = jnp.float32
BF16 = jnp.bfloat16

D_MODEL = 1024
DEPTH = 2
CHUNK = 64
EPS = 1e-6
N_ADA = 9
D_FF = 2816
MLA_HEADS = 4
Q_RANK = 256
KV_RANK = 128
NOPE_DIM = 128
ROPE_DIM = 64
V_DIM = 128
ROPE_BASE = 10000.0
ATTN_SCALE = (NOPE_DIM + ROPE_DIM) ** -0.5
LOG2_E = 1.4426950408889634
CONV_DIM = 256
CONV_W = 3
POOL_WINDOWS = (2, 4, 8, 16)
POOL_GROUPS = 4
POOL_DIM = 256
POOL_GROUP_DIM = POOL_DIM // POOL_GROUPS
POOL_MAX = 16

LANES = 128
BF16_SUBLANES = 16
KEY_DIM = 2 * LANES
FF_CHUNK = 256
ADA_ROWS = 16
ADA_COLS = 1024
HALO = 16
NEG_BIG = -1e30
VMEM_LIMIT = 56 * 1024 * 1024

IN_CQ, IN_CKV, IN_KR, IN_CB, IN_CC, IN_CV, IN_PU, IN_END = 0, 256, 384, 512, 768, 1024, 1280, 1536
UQ_NOPE, UQ_ROPE, UQ_END = 0, 512, 1024


def _rms(x, g):
    return x * lax.rsqrt(jnp.mean(x * x, axis=-1, keepdims=True) + EPS) * g


def _silu(x):
    return x * jax.nn.sigmoid(x)


def _dot(a, b):
    return jnp.dot(a, b, preferred_element_type=F32)


def _const_spec(shape):
    nd = len(shape)
    return pl.BlockSpec(shape, lambda *_: (0,) * nd, pipeline_mode=pl.Buffered(1))


def _layer_spec(stacked, layer):
    rest = stacked.shape[1:]
    return pl.BlockSpec((None,) + rest, lambda *_: (layer,) + (0,) * len(rest),
                        pipeline_mode=pl.Buffered(1))


def _ada_kernel(c_ref, w_ref, b_ref, o_ref):
    a = _silu(c_ref[...]).astype(BF16)
    o_ref[0] = _dot(a, w_ref[0].astype(BF16)) + b_ref[0]


def _ada(c_all, w_ada, b_ada):
    n_l, _, n_cols = w_ada.shape
    return pl.pallas_call(
        _ada_kernel,
        grid=(n_l, n_cols // ADA_COLS),
        in_specs=[
            pl.BlockSpec((ADA_ROWS, D_MODEL), lambda l, j: (0, 0)),
            pl.BlockSpec((1, D_MODEL, ADA_COLS), lambda l, j: (l, 0, j)),
            pl.BlockSpec((1, 1, ADA_COLS), lambda l, j: (l, 0, j)),
        ],
        out_specs=pl.BlockSpec((1, ADA_ROWS, ADA_COLS), lambda l, j: (l, 0, j)),
        out_shape=jax.ShapeDtypeStruct((n_l, ADA_ROWS, n_cols), F32),
        compiler_params=pltpu.CompilerParams(
            dimension_semantics=("arbitrary", "arbitrary"), vmem_limit_bytes=VMEM_LIMIT),
        name="ada",
    )(c_all, w_ada, b_ada.reshape(n_l, 1, n_cols))


def _cast_kernel(src_ref, dst_ref):
    dst_ref[...] = src_ref[...].astype(BF16)


def _cast_layer(stacked, layer, n_steps=8):
    _, n_rows, n_cols = stacked.shape
    slab = n_rows // n_steps
    assert n_rows % n_steps == 0 and slab % BF16_SUBLANES == 0
    return pl.pallas_call(
        _cast_kernel,
        grid=(n_steps,),
        in_specs=[pl.BlockSpec((None, slab, n_cols), lambda t: (layer, t, 0))],
        out_specs=pl.BlockSpec((None, slab, n_cols), lambda t: (0, t, 0)),
        out_shape=jax.ShapeDtypeStruct((1, n_rows, n_cols), BF16),
        compiler_params=pltpu.CompilerParams(
            dimension_semantics=("arbitrary",), vmem_limit_bytes=VMEM_LIMIT),
        name="cast_layer",
    )(stacked)


def _rope_table_kernel(inv_ref, sign_ref, cos_ref, sin_ref, cos_row_ref, sin_row_ref, *, pos0, tm):
    t = pl.program_id(0)

    @pl.when(t == 0)
    def _():
        row = lax.broadcasted_iota(jnp.int32, (tm, LANES), 0).astype(F32)
        ang = row * inv_ref[...]
        cos_row_ref[...] = jnp.cos(ang)
        sin_row_ref[...] = jnp.sin(ang)

    start = (pos0 + t * tm).astype(F32) * inv_ref[...]
    cos_s, sin_s = jnp.cos(start), jnp.sin(start)
    cos_r, sin_r = cos_row_ref[...], sin_row_ref[...]
    cos_ref[...] = cos_s * cos_r - sin_s * sin_r
    sin_ref[...] = (sin_s * cos_r + cos_s * sin_r) * sign_ref[...]


def _rope_tables(n_rows, pos0, tm):
    half = ROPE_DIM // 2
    inv = ROPE_BASE ** (-jnp.arange(half, dtype=F32) / half)
    inv = jnp.tile(inv, LANES // half).reshape(1, LANES)
    sign = jnp.tile(jnp.concatenate([-jnp.ones((half,), F32), jnp.ones((half,), F32)]),
                    LANES // ROPE_DIM).reshape(1, LANES)
    return pl.pallas_call(
        functools.partial(_rope_table_kernel, pos0=pos0, tm=tm),
        grid=(n_rows // tm,),
        in_specs=[_const_spec((1, LANES)), _const_spec((1, LANES))],
        out_specs=[pl.BlockSpec((tm, LANES), lambda i: (i, 0))] * 2,
        out_shape=[jax.ShapeDtypeStruct((n_rows, LANES), F32)] * 2,
        scratch_shapes=[pltpu.VMEM((tm, LANES), F32)] * 2,
        compiler_params=pltpu.CompilerParams(dimension_semantics=("arbitrary",)),
        name="rope_tables",
    )(inv, sign)


def _ffn_kernel(*refs, has_mix, final_norm, n_cast):
    refs = list(refs)
    h_ref = refs.pop(0)
    if has_mix:
        a_ref, sp_ref, wo_ref = refs.pop(0), refs.pop(0), refs.pop(0)
    mod_ref, ng_ref, wgu_ref, wd_ref = refs.pop(0), refs.pop(0), refs.pop(0), refs.pop(0)
    if final_norm:
        nf_ref = refs.pop(0)
    cast_in = [refs.pop(0) for _ in range(n_cast)]
    o_ref = refs.pop(0)
    for src, dst in zip(cast_in, refs):
        dst[...] = src[...].astype(BF16)

    bt, tt, d = h_ref.shape
    m = bt * tt
    x = h_ref[...]
    mod = mod_ref[...]
    row0 = 0
    if has_mix:
        mix_w = a_ref.shape[-1]
        mix = _dot(a_ref[...].reshape(m, mix_w), wo_ref[0:mix_w, :])
        mix = mix + _dot(sp_ref[...].reshape(m, sp_ref.shape[-1]), wo_ref[mix_w:, :])
        x = x + mod[:, 5:6, :] * mix.reshape(bt, tt, d)
        row0 = 6
    shift, scale, gate = mod[:, row0:row0 + 1, :], mod[:, row0 + 1:row0 + 2, :], mod[:, row0 + 2:row0 + 3, :]
    n = _rms(x, ng_ref[...] * (1.0 + scale)) + shift
    nb = n.reshape(m, d).astype(BF16)
    acc = jnp.zeros((m, d), F32)
    for c in range(D_FF // FF_CHUNK):
        g = _dot(nb, wgu_ref[:, c * FF_CHUNK:(c + 1) * FF_CHUNK])
        u = _dot(nb, wgu_ref[:, D_FF + c * FF_CHUNK:D_FF + (c + 1) * FF_CHUNK])
        act = (_silu(g) * u).astype(BF16)
        acc = acc + _dot(act, wd_ref[c * FF_CHUNK:(c + 1) * FF_CHUNK, :])
    y = x + 0.5 * gate * acc.reshape(bt, tt, d)
    if final_norm:
        y = _rms(y, nf_ref[...])
    o_ref[...] = y


def _ffn(h, mod, layer, norm_g, w_gu, w_down, *, bt, tt, mix=None, norm_final=None, cast=()):
    n_b, n_t, d = h.shape
    grid = (n_b // bt, n_t // tt)
    tile = lambda w: pl.BlockSpec((bt, tt, w), lambda b, t: (b, t, 0))
    args, specs = [h], [tile(d)]
    if mix is not None:
        a, sp, w_out = mix
        args += [a, sp, w_out]
        specs += [tile(a.shape[-1]), tile(sp.shape[-1]), _layer_spec(w_out, layer)]
    args += [mod, norm_g, w_gu, w_down]
    specs += [pl.BlockSpec((bt, N_ADA, d), lambda b, t: (b, 0, 0)), _layer_spec(norm_g, layer),
              _layer_spec(w_gu, 0), _layer_spec(w_down, 0)]
    if norm_final is not None:
        args.append(norm_final.reshape(1, d))
        specs.append(_const_spec((1, d)))
    out_specs, out_shape = [tile(d)], [jax.ShapeDtypeStruct(h.shape, F32)]
    assert not cast or grid[0] == 1
    for stacked, src_layer in cast:
        _, n_rows, n_cols = stacked.shape
        share = next(k for k in (1, 2, 4, 8) if grid[1] % k == 0 and n_rows % (grid[1] // k) == 0
                     and (n_rows // (grid[1] // k)) % BF16_SUBLANES == 0)
        slab = n_rows // (grid[1] // share)
        args.append(stacked)
        specs.append(pl.BlockSpec((None, slab, n_cols),
                                  lambda b, t, _l=src_layer, _s=share: (_l, t // _s, 0)))
        out_specs.append(pl.BlockSpec((None, slab, n_cols), lambda b, t, _s=share: (0, t // _s, 0)))
        out_shape.append(jax.ShapeDtypeStruct((1, n_rows, n_cols), BF16))
    outs = pl.pallas_call(
        functools.partial(_ffn_kernel, has_mix=mix is not None, final_norm=norm_final is not None,
                          n_cast=len(cast)),
        grid=grid,
        in_specs=specs,
        out_specs=out_specs,
        out_shape=out_shape,
        compiler_params=pltpu.CompilerParams(
            dimension_semantics=("arbitrary", "arbitrary"), vmem_limit_bytes=VMEM_LIMIT),
        name="ffn_mix" if mix is not None else "ffn",
    )(*args)
    return outs[0], tuple(outs[1:])


def _ffn_stream_kernel(*refs, has_mix, final_norm):
    refs = list(refs)
    h_ref = refs.pop(0)
    if has_mix:
        a_ref, sp_ref, wo_ref = refs.pop(0), refs.pop(0), refs.pop(0)
    mod_ref, ng_ref, wg_ref, wu_ref, wd_ref = (refs.pop(0) for _ in range(5))
    if final_norm:
        nf_ref = refs.pop(0)
    o_ref, x_ref, nb_ref, acc_ref = refs

    c = pl.program_id(0)
    bt, tt, d = h_ref.shape
    m = bt * tt
    mod = mod_ref[...]
    row0 = 6 if has_mix else 0
    shift, scale, gate = mod[:, row0:row0 + 1, :], mod[:, row0 + 1:row0 + 2, :], mod[:, row0 + 2:row0 + 3, :]

    @pl.when(c == 0)
    def _():
        x = h_ref[...]
        if has_mix:
            mix_w = a_ref.shape[-1]
            mix = _dot(a_ref[...].reshape(m, mix_w), wo_ref[0:mix_w, :])
            mix = mix + _dot(sp_ref[...].reshape(m, sp_ref.shape[-1]), wo_ref[mix_w:, :])
            x = x + mod[:, 5:6, :] * mix.reshape(bt, tt, d)
        x_ref[...] = x
        nb_ref[...] = (_rms(x, ng_ref[...] * (1.0 + scale)) + shift).reshape(m, d).astype(BF16)
        acc_ref[...] = jnp.zeros((m, d), F32)

    nb = nb_ref[...]
    act = (_silu(_dot(nb, wg_ref[...])) * _dot(nb, wu_ref[...])).astype(BF16)
    acc_ref[...] += _dot(act, wd_ref[...])

    @pl.when(c == pl.num_programs(0) - 1)
    def _():
        y = x_ref[...] + 0.5 * gate * acc_ref[...].reshape(bt, tt, d)
        if final_norm:
            y = _rms(y, nf_ref[...])
        o_ref[...] = y


def _ffn_stream(h, mod, layer, norm_g, w_gu, w_down, *, mix=None, norm_final=None):
    bt, tt, d = h.shape
    n_chunks = D_FF // FF_CHUNK
    whole = lambda arr: pl.BlockSpec(arr.shape, lambda c: (0,) * arr.ndim)
    args, specs = [h], [whole(h)]
    if mix is not None:
        a, sp, w_out = mix
        args += [a, sp, w_out]
        specs += [whole(a), whole(sp), _layer_spec(w_out, layer)]
    args += [mod, norm_g, w_gu, w_gu, w_down]
    specs += [whole(mod), _layer_spec(norm_g, layer),
              pl.BlockSpec((None, d, FF_CHUNK), lambda c: (0, 0, c)),
              pl.BlockSpec((None, d, FF_CHUNK), lambda c: (0, 0, n_chunks + c)),
              pl.BlockSpec((None, FF_CHUNK, d), lambda c: (0, c, 0))]
    if norm_final is not None:
        args.append(norm_final.reshape(1, d))
        specs.append(_const_spec((1, d)))
    return pl.pallas_call(
        functools.partial(_ffn_stream_kernel, has_mix=mix is not None, final_norm=norm_final is not None),
        grid=(n_chunks,),
        in_specs=specs,
        out_specs=whole(h),
        out_shape=jax.ShapeDtypeStruct(h.shape, F32),
        scratch_shapes=[pltpu.VMEM(h.shape, F32), pltpu.VMEM((bt * tt, d), BF16),
                        pltpu.VMEM((bt * tt, d), F32)],
        compiler_params=pltpu.CompilerParams(
            dimension_semantics=("arbitrary",), vmem_limit_bytes=VMEM_LIMIT),
        name="ffn_mix_stream" if mix is not None else "ffn_stream",
    )(*args)


def _mix_in_kernel(h_ref, mod_ref, ng_ref, wint_ref, qn_ref, wuq_ref, kvn_ref, wukt_ref,
                   convw_ref, poolw_ref, pools_ref, cos_ref, sin_ref, cstate_ref, pstate_ref,
                   q_ref, k_ref, ckv_ref, kr_ref, sp_ref, cnew_ref, pnew_ref,
                   eu_ref, ep_ref, s2_ref, s4_ref, s8_ref, *, tm, pos0, kr_transposed):
    t = pl.program_id(1)
    x = h_ref[0]
    mod = mod_ref[0]
    n = _rms(x, ng_ref[...] * (1.0 + mod[4:5, :])) + mod[3:4, :]
    nb = n.astype(BF16)
    half = ROPE_DIM // 2
    kr0 = Q_RANK + KV_RANK
    nt = lambda w: lax.dot_general(nb, w, (((1,), (1,)), ((), ())), preferred_element_type=F32)
    head = jnp.concatenate([wint_ref[0:kr0 + ROPE_DIM, :], wint_ref[kr0 + half:kr0 + ROPE_DIM, :],
                            wint_ref[kr0:kr0 + half, :]], axis=0)
    proj = jnp.concatenate([nt(head), nt(wint_ref[kr0 + ROPE_DIM:, :])], axis=-1)
    cos = cos_ref[...]
    sin = sin_ref[...]
    lane = lax.broadcasted_iota(jnp.int32, (tm, LANES), 1)

    def rope(packed):
        return packed * cos + pltpu.roll(packed, ROPE_DIM, axis=1) * sin

    qn = _rms(proj[:, IN_CQ:IN_CKV], qn_ref[...]).astype(BF16)
    qall = _dot(qn, wuq_ref[...])
    for hd in range(MLA_HEADS):
        q_nope = qall[:, UQ_NOPE + hd * NOPE_DIM:UQ_NOPE + (hd + 1) * NOPE_DIM].astype(BF16)
        q_lat = _dot(q_nope, wukt_ref[hd])
        q_rope = jnp.where(lane < ROPE_DIM, rope(qall[:, UQ_ROPE + hd * LANES:UQ_ROPE + (hd + 1) * LANES]), 0.0)
        q_ref[0, hd] = (jnp.concatenate([q_lat, q_rope], axis=-1) * (ATTN_SCALE * LOG2_E)).astype(BF16)

    ckv = _rms(proj[:, IN_CKV:IN_KR], kvn_ref[...])
    kr = rope(proj[:, IN_KR:IN_CB])
    ckv_ref[0] = ckv
    kr_ref[0] = jnp.transpose(kr)[:ROPE_DIM, :] if kr_transposed else kr[:, :ROPE_DIM]
    k_ref[0] = jnp.concatenate([ckv, jnp.where(lane < ROPE_DIM, kr, 1.0)], axis=-1).astype(BF16)

    @pl.when(t == 0)
    def _():
        eu_ref[0:HALO - (CONV_W - 1), :] = jnp.zeros((HALO - (CONV_W - 1), CONV_DIM), F32)
        eu_ref[HALO - (CONV_W - 1):HALO, :] = cstate_ref[0]
        ep_ref[0:1, :] = jnp.zeros((1, POOL_DIM), F32)
        ep_ref[1:HALO, :] = pstate_ref[0]

    u = proj[:, IN_CC:IN_CV] * proj[:, IN_CV:IN_PU]
    eu_ref[HALO:HALO + tm, :] = u
    cw = convw_ref[...]
    y = u * cw[CONV_W - 1:CONV_W, :]
    for j in range(CONV_W - 1):
        back = CONV_W - 1 - j
        y = y + eu_ref[HALO - back:HALO - back + tm, :] * cw[j:j + 1, :]
    sp_ref[0, :, 0:CONV_DIM] = (proj[:, IN_CB:IN_CC] * y).astype(BF16)
    cnew_ref[0] = eu_ref[HALO + tm - (CONV_W - 1):HALO + tm, :]
    eu_ref[0:HALO, :] = eu_ref[tm:tm + HALO, :]

    pu = proj[:, IN_PU:IN_END]
    ep_ref[HALO:HALO + tm, :] = pu
    n_ext = tm + HALO
    s2_ref[1:n_ext, :] = ep_ref[1:n_ext, :] + ep_ref[0:n_ext - 1, :]
    s4_ref[3:n_ext, :] = s2_ref[3:n_ext, :] + s2_ref[1:n_ext - 2, :]
    s8_ref[7:n_ext, :] = s4_ref[7:n_ext, :] + s4_ref[3:n_ext - 4, :]
    s16 = s8_ref[HALO:n_ext, :] + s8_ref[HALO - 8:n_ext - 8, :]
    lane_grp = lax.broadcasted_iota(jnp.int32, (tm, POOL_DIM), 1) >> (POOL_GROUP_DIM.bit_length() - 1)
    wsum = jnp.where(lane_grp == 0, s2_ref[HALO:n_ext, :],
                     jnp.where(lane_grp == 1, s4_ref[HALO:n_ext, :],
                               jnp.where(lane_grp == 2, s8_ref[HALO:n_ext, :], s16)))
    assert POOL_WINDOWS == tuple(2 << g for g in range(POOL_GROUPS))
    win = jnp.left_shift(2, lane_grp)
    pos = pos0 + t * tm + lax.broadcasted_iota(jnp.int32, (tm, POOL_DIM), 0)
    cnt = jnp.minimum(win, pos + 1).astype(F32)
    dlt = (wsum / cnt - pu).astype(BF16)
    sp_ref[0, :, CONV_DIM:] = (_dot(dlt, poolw_ref[...]) * pools_ref[...]).astype(BF16)
    pnew_ref[0] = ep_ref[tm + 1:tm + HALO, :]
    ep_ref[0:HALO, :] = ep_ref[tm:tm + HALO, :]


def _mix_in(h, mod, layer, lw, cos_t, sin_t, conv_state, pool_state, *, tm, pos0, kr_transposed):
    n_b, n_t, d = h.shape
    grid = (n_b, n_t // tm)
    rows = lambda w: pl.BlockSpec((1, tm, w), lambda b, t: (b, t, 0))
    per_b = lambda r, w: pl.BlockSpec((1, r, w), lambda b, t: (b, 0, 0))
    weights = [lw[k] for k in ("norm_mix", "w_in_t", "q_norm", "w_uq", "kv_norm", "w_ukt",
                               "conv_w", "pool_bd", "pool_scale")]
    kr_spec = pl.BlockSpec((1, ROPE_DIM, tm), lambda b, t: (b, 0, t)) if kr_transposed else rows(ROPE_DIM)
    kr_shape = (n_b, ROPE_DIM, n_t) if kr_transposed else (n_b, n_t, ROPE_DIM)
    in_specs = [rows(d), per_b(N_ADA, d)] + [_layer_spec(w, layer) for w in weights] + [
        pl.BlockSpec((tm, LANES), lambda b, t: (t, 0)), pl.BlockSpec((tm, LANES), lambda b, t: (t, 0)),
        per_b(CONV_W - 1, CONV_DIM), per_b(POOL_MAX - 1, POOL_DIM),
    ]
    out_specs = [
        pl.BlockSpec((1, MLA_HEADS, tm, KEY_DIM), lambda b, t: (b, 0, t, 0)),
        rows(KEY_DIM), rows(KV_RANK), kr_spec, rows(CONV_DIM + POOL_DIM),
        per_b(CONV_W - 1, CONV_DIM), per_b(POOL_MAX - 1, POOL_DIM),
    ]
    out_shape = [
        jax.ShapeDtypeStruct((n_b, MLA_HEADS, n_t, KEY_DIM), BF16),
        jax.ShapeDtypeStruct((n_b, n_t, KEY_DIM), BF16),
        jax.ShapeDtypeStruct((n_b, n_t, KV_RANK), F32),
        jax.ShapeDtypeStruct(kr_shape, F32),
        jax.ShapeDtypeStruct((n_b, n_t, CONV_DIM + POOL_DIM), BF16),
        jax.ShapeDtypeStruct((n_b, CONV_W - 1, CONV_DIM), F32),
        jax.ShapeDtypeStruct((n_b, POOL_MAX - 1, POOL_DIM), F32),
    ]
    ext = pltpu.VMEM((tm + HALO, POOL_DIM), F32)
    return pl.pallas_call(
        functools.partial(_mix_in_kernel, tm=tm, pos0=pos0, kr_transposed=kr_transposed),
        grid=grid,
        in_specs=in_specs,
        out_specs=out_specs,
        out_shape=out_shape,
        scratch_shapes=[ext, ext, ext, ext, ext],
        compiler_params=pltpu.CompilerParams(
            dimension_semantics=("arbitrary", "arbitrary"), vmem_limit_bytes=VMEM_LIMIT),
        name="mix_in",
    )(h, mod, *weights, cos_t, sin_t, conv_state, pool_state)


def _attn_kernel(*refs, tq, tk, tk_past, past_len, group_rows):
    refs = list(refs)
    q_ref, knew_ref = refs.pop(0), refs.pop(0)
    if past_len:
        ckv_past_ref, krt_past_ref = refs.pop(0), refs.pop(0)
    wuv_ref, o_ref, m_ref, acc_ref = refs[:4]
    s_ref = None if past_len else refs[4]

    i = pl.program_id(1)
    m_rows = MLA_HEADS * tq
    n_groups = m_rows // group_rows
    m_ref[...] = jnp.full(m_ref.shape, NEG_BIG, F32)
    acc_ref[...] = jnp.zeros(acc_ref.shape, F32)

    def lane_tile(x, width):
        return x[:, :width] if width < LANES else jnp.concatenate([x] * (width // LANES), axis=1)

    def group(g):
        return slice(g * group_rows, (g + 1) * group_rows)

    def scores(g, kblk):
        if group_rows >= tq:
            heads = group_rows // tq
            q = q_ref[0, g * heads:(g + 1) * heads].reshape(group_rows, KEY_DIM)
        else:
            per_head = tq // group_rows
            r0 = (g % per_head) * group_rows
            q = q_ref[0, g // per_head, r0:r0 + group_rows, :]
        return lax.dot_general(q, kblk, (((1,), (1,)), ((), ())), preferred_element_type=F32)

    def absorb(g, s, kblk, mask):
        width = kblk.shape[0]
        rows = group(g)
        if mask is not None:
            s = jnp.where(mask, s, NEG_BIG)
        m_prev = m_ref[rows, :]
        m_new = jnp.maximum(m_prev, jnp.max(s, axis=1, keepdims=True))
        alpha = jnp.exp2(m_prev - m_new)
        p = jnp.exp2(s - lane_tile(m_new, width)).astype(BF16)
        acc_ref[rows, :] = lane_tile(alpha, KEY_DIM) * acc_ref[rows, :] + _dot(p, kblk)
        m_ref[rows, :] = m_new

    chunk_shift = CHUNK.bit_length() - 1

    def diag_mask(g, col0, width):
        r0, n_r = ((g * group_rows) % tq, group_rows) if group_rows < tq else (0, tq)
        if (col0 + width - 1) >> chunk_shift <= r0 >> chunk_shift:
            return "all"
        if col0 >> chunk_shift > (r0 + n_r - 1) >> chunk_shift:
            return "none"
        row = (lax.broadcasted_iota(jnp.int32, (group_rows, width), 0) + r0) & (tq - 1)
        col = lax.broadcasted_iota(jnp.int32, (group_rows, width), 1) + col0
        return (col >> chunk_shift) <= (row >> chunk_shift)

    def mask_arg(vis):
        return None if isinstance(vis, str) else vis

    def any_visible(vis):
        return not (isinstance(vis, str) and vis == "none")

    if past_len:
        def past_body(j, carry):
            start = pl.multiple_of(j * tk_past, tk_past)
            lat = ckv_past_ref[0, pl.ds(start, tk_past), :].astype(BF16)
            rot_t = krt_past_ref[0, :, pl.ds(start, tk_past)].astype(BF16)
            for g in range(n_groups):
                rows = group(g)
                q = q_ref[0].reshape(m_rows, KEY_DIM)[rows]
                s = lax.dot_general(q[:, :KV_RANK], lat, (((1,), (1,)), ((), ())), preferred_element_type=F32)
                s = s + _dot(q[:, KV_RANK:KV_RANK + ROPE_DIM], rot_t)
                m_prev = m_ref[rows, :]
                m_new = jnp.maximum(m_prev, jnp.max(s, axis=1, keepdims=True))
                alpha = jnp.exp2(m_prev - m_new)
                p = jnp.exp2(s - lane_tile(m_new, tk_past))
                denom = jnp.broadcast_to(jnp.sum(p, axis=1, keepdims=True), (group_rows, KEY_DIM - KV_RANK))
                acc_ref[rows, :] = lane_tile(alpha, KEY_DIM) * acc_ref[rows, :] + jnp.concatenate(
                    [_dot(p.astype(BF16), lat), denom], axis=-1)
                m_ref[rows, :] = m_new
            return carry
        lax.fori_loop(0, past_len // tk_past, past_body, 0)
        kblk = knew_ref[0]
        for g in range(n_groups):
            absorb(g, scores(g, kblk), kblk, mask_arg(diag_mask(g, 0, tq)))
    else:
        def new_block(j):
            return knew_ref[0, pl.ds(pl.multiple_of(j * tk, tk), tk), :]

        n_diag = tq // tk
        n_full = i * n_diag
        first = new_block(0)
        for g in range(n_groups):
            s_ref[group(g), :] = scores(g, first)

        def body(j, carry):
            kblk, knext = new_block(j), new_block(j + 1)
            for g in range(n_groups):
                s = s_ref[group(g), :]
                s_ref[group(g), :] = scores(g, knext)
                absorb(g, s, kblk, None)
            return carry
        lax.fori_loop(0, n_full, body, 0)

        for d in range(n_diag):
            kblk = new_block(n_full + d)
            knext = new_block(n_full + d + 1) if d + 1 < n_diag else None
            for g in range(n_groups):
                vis = diag_mask(g, d * tk, tk)
                if any_visible(vis):
                    s = s_ref[group(g), :]
                if knext is not None and any_visible(diag_mask(g, (d + 1) * tk, tk)):
                    s_ref[group(g), :] = scores(g, knext)
                if any_visible(vis):
                    absorb(g, s, kblk, mask_arg(vis))

    heads = []
    for hd in range(MLA_HEADS):
        acc = acc_ref[hd * tq:(hd + 1) * tq, :]
        denom = acc[:, KV_RANK + ROPE_DIM:KV_RANK + ROPE_DIM + 1]
        heads.append(_dot((acc[:, :KV_RANK] / denom).astype(BF16), wuv_ref[hd]))
    o_ref[0] = jnp.concatenate(heads, axis=-1).astype(BF16)


def _attn(q, k_new, past, layer, w_uv, *, tq, tk, tk_past, group_rows):
    n_b, _, n_t, _ = q.shape
    past_len = 0 if past is None else past[0].shape[2]
    assert (past_len % CHUNK == 0) and (tq % CHUNK == 0 or n_t == tq <= CHUNK)
    assert past_len == 0 or (n_t == tq and past_len % tk_past == 0)
    m_rows = MLA_HEADS * tq
    assert m_rows % group_rows == 0 and (group_rows % tq == 0 or tq % group_rows == 0)
    assert past_len or (tq % tk == 0 and tk % CHUNK == 0)
    scratch = [pltpu.VMEM((m_rows, LANES), F32), pltpu.VMEM((m_rows, KEY_DIM), F32)]
    if not past_len:
        scratch.append(pltpu.VMEM((m_rows, tk), F32))
    args = [q, k_new]
    specs = [pl.BlockSpec((1, MLA_HEADS, tq, KEY_DIM), lambda b, i: (b, 0, i, 0)),
             pl.BlockSpec((1, n_t, KEY_DIM), lambda b, i: (b, 0, 0), pipeline_mode=pl.Buffered(1))]
    if past_len:
        args += list(past)
        specs += [pl.BlockSpec((None, 1) + c.shape[2:], lambda b, i: (layer, b, 0, 0)) for c in past]
    args.append(w_uv)
    specs.append(_layer_spec(w_uv, layer))
    return pl.pallas_call(
        functools.partial(_attn_kernel, tq=tq, tk=tk, tk_past=tk_past, past_len=past_len,
                          group_rows=group_rows),
        grid=(n_b, n_t // tq),
        in_specs=specs,
        out_specs=pl.BlockSpec((1, tq, MLA_HEADS * V_DIM), lambda b, i: (b, i, 0)),
        out_shape=jax.ShapeDtypeStruct((n_b, n_t, MLA_HEADS * V_DIM), BF16),
        scratch_shapes=scratch,
        compiler_params=pltpu.CompilerParams(
            dimension_semantics=("arbitrary", "arbitrary"), vmem_limit_bytes=VMEM_LIMIT),
        name="attn",
    )(*args)


def _prep_weights(norm_ffn1, norm_mix, w_in, q_norm, w_uq, kv_norm, w_ukv, conv_w, pool_w, pool_scale,
                  w_out, norm_ffn2):
    n_l = w_in.shape[0]
    half = ROPE_DIM // 2
    packed = lambda w: jnp.concatenate([w, w[..., half:], w[..., :half]], axis=-1)
    vec = lambda v: v.reshape(n_l, 1, v.shape[-1])
    wq = w_uq.reshape(n_l, Q_RANK, MLA_HEADS, NOPE_DIM + ROPE_DIM)
    nope = wq[..., :NOPE_DIM].reshape(n_l, Q_RANK, MLA_HEADS * NOPE_DIM)
    rope_cols = [wq[:, :, hd, NOPE_DIM:] for hd in range(MLA_HEADS)]
    w_uq_p = jnp.concatenate([nope] + [packed(r) for r in rope_cols], axis=-1).astype(BF16)

    wkv = w_ukv.reshape(n_l, KV_RANK, MLA_HEADS, NOPE_DIM + V_DIM)
    w_ukt = jnp.transpose(wkv[..., :NOPE_DIM], (0, 2, 3, 1)).astype(BF16)
    w_uv = jnp.transpose(wkv[..., NOPE_DIM:], (0, 2, 1, 3)).astype(BF16)

    eye = jnp.eye(POOL_GROUPS, dtype=F32)
    pool_bd = (pool_w[:, :, :, None, :] * eye[None, :, None, :, None]).reshape(
        n_l, POOL_DIM, POOL_DIM).astype(BF16)

    return {
        "norm_ffn1": vec(norm_ffn1),
        "norm_mix": vec(norm_mix), "w_in_t": jnp.swapaxes(w_in, 1, 2).astype(BF16), "q_norm": vec(q_norm),
        "w_uq": w_uq_p,
        "kv_norm": vec(kv_norm), "w_ukt": w_ukt, "w_uv": w_uv,
        "conv_w": conv_w, "pool_bd": pool_bd, "pool_scale": vec(pool_scale),
        "w_out": w_out.astype(BF16),
        "norm_ffn2": vec(norm_ffn2),
    }


def _trunk_layer(h, mod, layer, lw, ffn1_w, ffn2_w, cast1, cast2, tables, conv_state, pool_state, past,
                 norm_final, *, bt, tt, tm, tq, group_rows, pos0):
    one_tile = h.shape[0] == bt and h.shape[1] == tt and not cast1 and not cast2
    if one_tile:
        h1 = _ffn_stream(h, mod, layer, lw["norm_ffn1"], *ffn1_w)
    else:
        h1, made1 = _ffn(h, mod, layer, lw["norm_ffn1"], *ffn1_w, bt=bt, tt=tt, cast=cast1)
        if ffn2_w is None:
            ffn2_w = made1
    q, k_new, ckv, kr, sp, conv_new, pool_new = _mix_in(
        h1, mod, layer, lw, tables[0], tables[1], conv_state, pool_state, tm=tm, pos0=pos0,
        kr_transposed=past is None)
    a = _attn(q, k_new, past, layer, lw["w_uv"], tq=tq, tk=min(tq, 512),
              tk_past=past[0].shape[2] if past is not None else 0, group_rows=group_rows)
    if one_tile:
        h3, made2 = _ffn_stream(h1, mod, layer, lw["norm_ffn2"], *ffn2_w, mix=(a, sp, lw["w_out"]),
                                norm_final=norm_final), ()
    else:
        h3, made2 = _ffn(h1, mod, layer, lw["norm_ffn2"], *ffn2_w, bt=bt, tt=tt,
                         mix=(a, sp, lw["w_out"]), norm_final=norm_final, cast=cast2)
    return (h3, ckv, kr, conv_new, pool_new), ffn2_w, made2


def kernel(x_prompt, x_sample, c_prompt, c_sample, cache_ckv, cache_krope, state_conv, state_pool, w_ada, b_ada, norm_ffn1, w_ffn1_gu, w_ffn1_down, norm_mix, w_in, q_norm, w_uq, kv_norm, w_ukv, conv_w, pool_w, pool_scale, w_out, norm_ffn2, w_ffn2_gu, w_ffn2_down, norm_final):
    bp, seq, _ = x_prompt.shape
    bs, dec_seq, _ = x_sample.shape
    past_len = cache_ckv.shape[2]
    assert bp + bs <= ADA_ROWS

    c_all = jnp.concatenate(
        [c_prompt, c_sample, jnp.zeros((ADA_ROWS - bp - bs, D_MODEL), F32)], axis=0)
    mods = _ada(c_all, w_ada, b_ada).reshape(DEPTH, ADA_ROWS, N_ADA, D_MODEL)

    tm_p = 512
    tables_p = _rope_tables(seq, 0, tm_p)
    tables_s = _rope_tables(dec_seq, past_len, dec_seq)
    lw = _prep_weights(norm_ffn1, norm_mix, w_in, q_norm, w_uq, kv_norm, w_ukv, conv_w, pool_w,
                       pool_scale, w_out, norm_ffn2)
    past = (cache_ckv, jnp.swapaxes(cache_krope, 2, 3))

    hp, hs = x_prompt, x_sample
    outs_p, outs_s = [], []
    zero_conv = jnp.zeros((bp, CONV_W - 1, CONV_DIM), F32)
    zero_pool = jnp.zeros((bp, POOL_MAX - 1, POOL_DIM), F32)
    ffn1_w = (_cast_layer(w_ffn1_gu, 0), _cast_layer(w_ffn1_down, 0))
    for l in range(DEPTH):
        nf = norm_final if l == DEPTH - 1 else None
        cast2 = ((w_ffn1_gu, l + 1), (w_ffn1_down, l + 1)) if l + 1 < DEPTH else ()
        op, ffn2_w, next_ffn1_w = _trunk_layer(
            hp, mods[l, :bp], l, lw, ffn1_w, None, ((w_ffn2_gu, l), (w_ffn2_down, l)), cast2,
            tables_p, zero_conv, zero_pool, None, nf,
            bt=1, tt=tm_p, tm=tm_p, tq=1024, group_rows=512, pos0=0)
        os_, _, _ = _trunk_layer(
            hs, mods[l, bp:bp + bs], l, lw, ffn1_w, ffn2_w, (), (),
            tables_s, state_conv[l], state_pool[l], past, nf,
            bt=bs, tt=dec_seq, tm=dec_seq, tq=dec_seq, group_rows=MLA_HEADS * dec_seq, pos0=past_len)
        hp, hs, ffn1_w = op[0], os_[0], next_ffn1_w
        outs_p.append(op[1:])
        outs_s.append(os_[1:])

    stack = lambda outs, k: jnp.stack([o[k] for o in outs])
    return (hp, hs,
            stack(outs_p, 0), jnp.swapaxes(stack(outs_p, 1), 2, 3), stack(outs_p, 2), stack(outs_p, 3),
            stack(outs_s, 0), stack(outs_s, 1), stack(outs_s, 2), stack(outs_s, 3))
```

```python
import functools

import jax
import jax.numpy as jnp
from jax import lax
from jax.experimental import pallas as pl
from jax.experimental.pallas import tpu as pltpu

F32 = jnp.float32
BF16 = jnp.bfloat16

D_MODEL = 1024
DEPTH = 2
CHUNK = 64
EPS = 1e-6
N_ADA = 9
D_FF = 2816
MLA_HEADS = 4
Q_RANK = 256
KV_RANK = 128
NOPE_DIM = 128
ROPE_DIM = 64
V_DIM = 128
ROPE_BASE = 10000.0
ATTN_SCALE = (NOPE_DIM + ROPE_DIM) ** -0.5
LOG2_E = 1.4426950408889634
CONV_DIM = 256
CONV_W = 3
POOL_WINDOWS = (2, 4, 8, 16)
POOL_GROUPS = 4
POOL_DIM = 256
POOL_GROUP_DIM = POOL_DIM // POOL_GROUPS
POOL_MAX = 16

LANES = 128
BF16_SUBLANES = 16
KEY_DIM = 2 * LANES
FF_CHUNK = 256
ADA_ROWS = 16
ADA_COLS = 2304
HALO = 16
NEG_BIG = -1e30
VMEM_LIMIT = 56 * 1024 * 1024

IN_CQ, IN_CKV, IN_KR, IN_CB, IN_CC, IN_CV, IN_PU, IN_END = 0, 256, 384, 512, 768, 1024, 1280, 1536
UQ_NOPE, UQ_ROPE, UQ_END = 0, 512, 1024


def _rms(x, g):
    return x * lax.rsqrt(jnp.mean(x * x, axis=-1, keepdims=True) + EPS) * g


def _silu(x):
    return x * jax.nn.sigmoid(x)


def _dot(a, b):
    return jnp.dot(a, b, preferred_element_type=F32)


def _const_spec(shape):
    nd = len(shape)
    return pl.BlockSpec(shape, lambda *_: (0,) * nd, pipeline_mode=pl.Buffered(1))


def _layer_spec(stacked, layer):
    rest = stacked.shape[1:]
    return pl.BlockSpec((None,) + rest, lambda *_: (layer,) + (0,) * len(rest),
                        pipeline_mode=pl.Buffered(1))


def _ada_kernel(c_ref, w_ref, b_ref, o_ref):
    a = _silu(c_ref[...]).astype(BF16)
    o_ref[0] = _dot(a, w_ref[0].astype(BF16)) + b_ref[0]


def _ada(c_all, w_ada, b_ada):
    n_l, _, n_cols = w_ada.shape
    return pl.pallas_call(
        _ada_kernel,
        grid=(n_l, n_cols // ADA_COLS),
        in_specs=[
            pl.BlockSpec((ADA_ROWS, D_MODEL), lambda l, j: (0, 0)),
            pl.BlockSpec((1, D_MODEL, ADA_COLS), lambda l, j: (l, 0, j)),
            pl.BlockSpec((1, 1, ADA_COLS), lambda l, j: (l, 0, j)),
        ],
        out_specs=pl.BlockSpec((1, ADA_ROWS, ADA_COLS), lambda l, j: (l, 0, j)),
        out_shape=jax.ShapeDtypeStruct((n_l, ADA_ROWS, n_cols), F32),
        compiler_params=pltpu.CompilerParams(
            dimension_semantics=("arbitrary", "arbitrary"), vmem_limit_bytes=VMEM_LIMIT),
        name="ada",
    )(c_all, w_ada, b_ada.reshape(n_l, 1, n_cols))


def _cast_kernel(src_ref, dst_ref):
    dst_ref[...] = src_ref[...].astype(BF16)


def _cast_layer(stacked, layer, n_steps=4):
    _, n_rows, n_cols = stacked.shape
    slab = n_rows // n_steps
    assert n_rows % n_steps == 0 and slab % BF16_SUBLANES == 0
    return pl.pallas_call(
        _cast_kernel,
        grid=(n_steps,),
        in_specs=[pl.BlockSpec((None, slab, n_cols), lambda t: (layer, t, 0))],
        out_specs=pl.BlockSpec((None, slab, n_cols), lambda t: (0, t, 0)),
        out_shape=jax.ShapeDtypeStruct((1, n_rows, n_cols), BF16),
        compiler_params=pltpu.CompilerParams(
            dimension_semantics=("arbitrary",), vmem_limit_bytes=VMEM_LIMIT),
        name="cast_layer",
    )(stacked)


def _rope_table_kernel(inv_ref, sign_ref, cos_ref, sin_ref, cos_row_ref, sin_row_ref, *, pos0, tm):
    t = pl.program_id(0)

    @pl.when(t == 0)
    def _():
        row = lax.broadcasted_iota(jnp.int32, (tm, LANES), 0).astype(F32)
        ang = row * inv_ref[...]
        cos_row_ref[...] = jnp.cos(ang)
        sin_row_ref[...] = jnp.sin(ang)

    start = (pos0 + t * tm).astype(F32) * inv_ref[...]
    cos_s, sin_s = jnp.cos(start), jnp.sin(start)
    cos_r, sin_r = cos_row_ref[...], sin_row_ref[...]
    cos_ref[...] = cos_s * cos_r - sin_s * sin_r
    sin_ref[...] = (sin_s * cos_r + cos_s * sin_r) * sign_ref[...]


def _rope_tables(n_rows, pos0, tm):
    half = ROPE_DIM // 2
    inv = ROPE_BASE ** (-jnp.arange(half, dtype=F32) / half)
    inv = jnp.tile(inv, LANES // half).reshape(1, LANES)
    sign = jnp.tile(jnp.concatenate([-jnp.ones((half,), F32), jnp.ones((half,), F32)]),
                    LANES // ROPE_DIM).reshape(1, LANES)
    return pl.pallas_call(
        functools.partial(_rope_table_kernel, pos0=pos0, tm=tm),
        grid=(n_rows // tm,),
        in_specs=[_const_spec((1, LANES)), _const_spec((1, LANES))],
        out_specs=[pl.BlockSpec((tm, LANES), lambda i: (i, 0))] * 2,
        out_shape=[jax.ShapeDtypeStruct((n_rows, LANES), F32)] * 2,
        scratch_shapes=[pltpu.VMEM((tm, LANES), F32)] * 2,
        compiler_params=pltpu.CompilerParams(dimension_semantics=("arbitrary",)),
        name="rope_tables",
    )(inv, sign)


def _ffn_kernel(*refs, has_mix, final_norm, n_cast):
    refs = list(refs)
    h_ref = refs.pop(0)
    if has_mix:
        a_ref, sp_ref, wo_ref = refs.pop(0), refs.pop(0), refs.pop(0)
    mod_ref, ng_ref, wgu_ref, wd_ref = refs.pop(0), refs.pop(0), refs.pop(0), refs.pop(0)
    if final_norm:
        nf_ref = refs.pop(0)
    cast_in = [refs.pop(0) for _ in range(n_cast)]
    o_ref = refs.pop(0)
    for src, dst in zip(cast_in, refs):
        dst[...] = src[...].astype(BF16)

    bt, tt, d = h_ref.shape
    m = bt * tt
    x = h_ref[...]
    mod = mod_ref[...]
    row0 = 0
    if has_mix:
        mix_w = a_ref.shape[-1]
        mix = _dot(a_ref[...].reshape(m, mix_w), wo_ref[0:mix_w, :])
        mix = mix + _dot(sp_ref[...].reshape(m, sp_ref.shape[-1]), wo_ref[mix_w:, :])
        x = x + mod[:, 5:6, :] * mix.reshape(bt, tt, d)
        row0 = 6
    shift, scale, gate = mod[:, row0:row0 + 1, :], mod[:, row0 + 1:row0 + 2, :], mod[:, row0 + 2:row0 + 3, :]
    n = _rms(x, ng_ref[...] * (1.0 + scale)) + shift
    nb = n.reshape(m, d).astype(BF16)
    acc = jnp.zeros((m, d), F32)
    for c in range(D_FF // FF_CHUNK):
        g = _dot(nb, wgu_ref[:, c * FF_CHUNK:(c + 1) * FF_CHUNK])
        u = _dot(nb, wgu_ref[:, D_FF + c * FF_CHUNK:D_FF + (c + 1) * FF_CHUNK])
        act = (_silu(g) * u).astype(BF16)
        acc = acc + _dot(act, wd_ref[c * FF_CHUNK:(c + 1) * FF_CHUNK, :])
    y = x + 0.5 * gate * acc.reshape(bt, tt, d)
    if final_norm:
        y = _rms(y, nf_ref[...])
    o_ref[...] = y


def _ffn(h, mod, layer, norm_g, w_gu, w_down, *, bt, tt, mix=None, norm_final=None, cast=()):
    n_b, n_t, d = h.shape
    grid = (n_b // bt, n_t // tt)
    tile = lambda w: pl.BlockSpec((bt, tt, w), lambda b, t: (b, t, 0))
    args, specs = [h], [tile(d)]
    if mix is not None:
        a, sp, w_out = mix
        args += [a, sp, w_out]
        specs += [tile(a.shape[-1]), tile(sp.shape[-1]), _layer_spec(w_out, layer)]
    args += [mod, norm_g, w_gu, w_down]
    specs += [pl.BlockSpec((bt, N_ADA, d), lambda b, t: (b, 0, 0)), _layer_spec(norm_g, layer),
              _layer_spec(w_gu, 0), _layer_spec(w_down, 0)]
    if norm_final is not None:
        args.append(norm_final.reshape(1, d))
        specs.append(_const_spec((1, d)))
    out_specs, out_shape = [tile(d)], [jax.ShapeDtypeStruct(h.shape, F32)]
    assert not cast or grid[0] == 1
    for stacked, src_layer in cast:
        _, n_rows, n_cols = stacked.shape
        share = next(k for k in (1, 2, 4, 8) if grid[1] % k == 0 and n_rows % (grid[1] // k) == 0
                     and (n_rows // (grid[1] // k)) % BF16_SUBLANES == 0)
        slab = n_rows // (grid[1] // share)
        args.append(stacked)
        specs.append(pl.BlockSpec((None, slab, n_cols),
                                  lambda b, t, _l=src_layer, _s=share: (_l, t // _s, 0)))
        out_specs.append(pl.BlockSpec((None, slab, n_cols), lambda b, t, _s=share: (0, t // _s, 0)))
        out_shape.append(jax.ShapeDtypeStruct((1, n_rows, n_cols), BF16))
    outs = pl.pallas_call(
        functools.partial(_ffn_kernel, has_mix=mix is not None, final_norm=norm_final is not None,
                          n_cast=len(cast)),
        grid=grid,
        in_specs=specs,
        out_specs=out_specs,
        out_shape=out_shape,
        compiler_params=pltpu.CompilerParams(
            dimension_semantics=("arbitrary", "arbitrary"), vmem_limit_bytes=VMEM_LIMIT),
        name="ffn_mix" if mix is not None else "ffn",
    )(*args)
    return outs[0], tuple(outs[1:])


def _mix_in_kernel(h_ref, mod_ref, ng_ref, wint_ref, qn_ref, wuq_ref, kvn_ref, wukt_ref,
                   convw_ref, poolw_ref, pools_ref, cos_ref, sin_ref, cstate_ref, pstate_ref,
                   q_ref, k_ref, ckv_ref, kr_ref, sp_ref, cnew_ref, pnew_ref,
                   eu_ref, ep_ref, s2_ref, s4_ref, s8_ref, *, tm, pos0, kr_transposed):
    t = pl.program_id(1)
    x = h_ref[0]
    mod = mod_ref[0]
    n = _rms(x, ng_ref[...] * (1.0 + mod[4:5, :])) + mod[3:4, :]
    nb = n.astype(BF16)
    half = ROPE_DIM // 2
    kr0 = Q_RANK + KV_RANK
    nt = lambda w: lax.dot_general(nb, w, (((1,), (1,)), ((), ())), preferred_element_type=F32)
    head = jnp.concatenate([wint_ref[0:kr0 + ROPE_DIM, :], wint_ref[kr0 + half:kr0 + ROPE_DIM, :],
                            wint_ref[kr0:kr0 + half, :]], axis=0)
    proj = jnp.concatenate([nt(head), nt(wint_ref[kr0 + ROPE_DIM:, :])], axis=-1)
    cos = cos_ref[...]
    sin = sin_ref[...]
    lane = lax.broadcasted_iota(jnp.int32, (tm, LANES), 1)

    def rope(packed):
        return packed * cos + pltpu.roll(packed, ROPE_DIM, axis=1) * sin

    qn = _rms(proj[:, IN_CQ:IN_CKV], qn_ref[...]).astype(BF16)
    qall = _dot(qn, wuq_ref[...])
    for hd in range(MLA_HEADS):
        q_nope = qall[:, UQ_NOPE + hd * NOPE_DIM:UQ_NOPE + (hd + 1) * NOPE_DIM].astype(BF16)
        q_lat = _dot(q_nope, wukt_ref[hd])
        q_rope = jnp.where(lane < ROPE_DIM, rope(qall[:, UQ_ROPE + hd * LANES:UQ_ROPE + (hd + 1) * LANES]), 0.0)
        q_ref[0, hd] = (jnp.concatenate([q_lat, q_rope], axis=-1) * (ATTN_SCALE * LOG2_E)).astype(BF16)

    ckv = _rms(proj[:, IN_CKV:IN_KR], kvn_ref[...])
    kr = rope(proj[:, IN_KR:IN_CB])
    ckv_ref[0] = ckv
    kr_ref[0] = jnp.transpose(kr)[:ROPE_DIM, :] if kr_transposed else kr[:, :ROPE_DIM]
    k_ref[0] = jnp.concatenate([ckv, jnp.where(lane < ROPE_DIM, kr, 1.0)], axis=-1).astype(BF16)

    @pl.when(t == 0)
    def _():
        eu_ref[0:HALO - (CONV_W - 1), :] = jnp.zeros((HALO - (CONV_W - 1), CONV_DIM), F32)
        eu_ref[HALO - (CONV_W - 1):HALO, :] = cstate_ref[0]
        ep_ref[0:1, :] = jnp.zeros((1, POOL_DIM), F32)
        ep_ref[1:HALO, :] = pstate_ref[0]

    u = proj[:, IN_CC:IN_CV] * proj[:, IN_CV:IN_PU]
    eu_ref[HALO:HALO + tm, :] = u
    cw = convw_ref[...]
    y = u * cw[CONV_W - 1:CONV_W, :]
    for j in range(CONV_W - 1):
        back = CONV_W - 1 - j
        y = y + eu_ref[HALO - back:HALO - back + tm, :] * cw[j:j + 1, :]
    sp_ref[0, :, 0:CONV_DIM] = (proj[:, IN_CB:IN_CC] * y).astype(BF16)
    cnew_ref[0] = eu_ref[HALO + tm - (CONV_W - 1):HALO + tm, :]
    eu_ref[0:HALO, :] = eu_ref[tm:tm + HALO, :]

    pu = proj[:, IN_PU:IN_END]
    ep_ref[HALO:HALO + tm, :] = pu
    n_ext = tm + HALO
    s2_ref[1:n_ext, :] = ep_ref[1:n_ext, :] + ep_ref[0:n_ext - 1, :]
    s4_ref[3:n_ext, :] = s2_ref[3:n_ext, :] + s2_ref[1:n_ext - 2, :]
    s8_ref[7:n_ext, :] = s4_ref[7:n_ext, :] + s4_ref[3:n_ext - 4, :]
    s16 = s8_ref[HALO:n_ext, :] + s8_ref[HALO - 8:n_ext - 8, :]
    lane_grp = lax.broadcasted_iota(jnp.int32, (tm, POOL_DIM), 1) >> (POOL_GROUP_DIM.bit_length() - 1)
    wsum = jnp.where(lane_grp == 0, s2_ref[HALO:n_ext, :],
                     jnp.where(lane_grp == 1, s4_ref[HALO:n_ext, :],
                               jnp.where(lane_grp == 2, s8_ref[HALO:n_ext, :], s16)))
    assert POOL_WINDOWS == tuple(2 << g for g in range(POOL_GROUPS))
    win = jnp.left_shift(2, lane_grp)
    pos = pos0 + t * tm + lax.broadcasted_iota(jnp.int32, (tm, POOL_DIM), 0)
    cnt = jnp.minimum(win, pos + 1).astype(F32)
    dlt = (wsum / cnt - pu).astype(BF16)
    sp_ref[0, :, CONV_DIM:] = (_dot(dlt, poolw_ref[...]) * pools_ref[...]).astype(BF16)
    pnew_ref[0] = ep_ref[tm + 1:tm + HALO, :]
    ep_ref[0:HALO, :] = ep_ref[tm:tm + HALO, :]


def _mix_in(h, mod, layer, lw, cos_t, sin_t, conv_state, pool_state, *, tm, pos0, kr_transposed):
    n_b, n_t, d = h.shape
    grid = (n_b, n_t // tm)
    rows = lambda w: pl.BlockSpec((1, tm, w), lambda b, t: (b, t, 0))
    per_b = lambda r, w: pl.BlockSpec((1, r, w), lambda b, t: (b, 0, 0))
    weights = [lw[k] for k in ("norm_mix", "w_in_t", "q_norm", "w_uq", "kv_norm", "w_ukt",
                               "conv_w", "pool_bd", "pool_scale")]
    kr_spec = pl.BlockSpec((1, ROPE_DIM, tm), lambda b, t: (b, 0, t)) if kr_transposed else rows(ROPE_DIM)
    kr_shape = (n_b, ROPE_DIM, n_t) if kr_transposed else (n_b, n_t, ROPE_DIM)
    in_specs = [rows(d), per_b(N_ADA, d)] + [_layer_spec(w, layer) for w in weights] + [
        pl.BlockSpec((tm, LANES), lambda b, t: (t, 0)), pl.BlockSpec((tm, LANES), lambda b, t: (t, 0)),
        per_b(CONV_W - 1, CONV_DIM), per_b(POOL_MAX - 1, POOL_DIM),
    ]
    out_specs = [
        pl.BlockSpec((1, MLA_HEADS, tm, KEY_DIM), lambda b, t: (b, 0, t, 0)),
        rows(KEY_DIM), rows(KV_RANK), kr_spec, rows(CONV_DIM + POOL_DIM),
        per_b(CONV_W - 1, CONV_DIM), per_b(POOL_MAX - 1, POOL_DIM),
    ]
    out_shape = [
        jax.ShapeDtypeStruct((n_b, MLA_HEADS, n_t, KEY_DIM), BF16),
        jax.ShapeDtypeStruct((n_b, n_t, KEY_DIM), BF16),
        jax.ShapeDtypeStruct((n_b, n_t, KV_RANK), F32),
        jax.ShapeDtypeStruct(kr_shape, F32),
        jax.ShapeDtypeStruct((n_b, n_t, CONV_DIM + POOL_DIM), BF16),
        jax.ShapeDtypeStruct((n_b, CONV_W - 1, CONV_DIM), F32),
        jax.ShapeDtypeStruct((n_b, POOL_MAX - 1, POOL_DIM), F32),
    ]
    ext = pltpu.VMEM((tm + HALO, POOL_DIM), F32)
    return pl.pallas_call(
        functools.partial(_mix_in_kernel, tm=tm, pos0=pos0, kr_transposed=kr_transposed),
        grid=grid,
        in_specs=in_specs,
        out_specs=out_specs,
        out_shape=out_shape,
        scratch_shapes=[ext, ext, ext, ext, ext],
        compiler_params=pltpu.CompilerParams(
            dimension_semantics=("arbitrary", "arbitrary"), vmem_limit_bytes=VMEM_LIMIT),
        name="mix_in",
    )(h, mod, *weights, cos_t, sin_t, conv_state, pool_state)


def _attn_kernel(*refs, tq, tk, tk_past, past_len, group_rows):
    refs = list(refs)
    q_ref, knew_ref = refs.pop(0), refs.pop(0)
    if past_len:
        ckv_past_ref, krt_past_ref = refs.pop(0), refs.pop(0)
    wuv_ref, o_ref, m_ref, acc_ref = refs[:4]
    s_ref = None if past_len else refs[4]

    i = pl.program_id(1)
    m_rows = MLA_HEADS * tq
    n_groups = m_rows // group_rows
    m_ref[...] = jnp.full(m_ref.shape, NEG_BIG, F32)
    acc_ref[...] = jnp.zeros(acc_ref.shape, F32)

    def lane_tile(x, width):
        return x[:, :width] if width < LANES else jnp.concatenate([x] * (width // LANES), axis=1)

    def group(g):
        return slice(g * group_rows, (g + 1) * group_rows)

    def scores(g, kblk):
        if group_rows >= tq:
            heads = group_rows // tq
            q = q_ref[0, g * heads:(g + 1) * heads].reshape(group_rows, KEY_DIM)
        else:
            per_head = tq // group_rows
            r0 = (g % per_head) * group_rows
            q = q_ref[0, g // per_head, r0:r0 + group_rows, :]
        return lax.dot_general(q, kblk, (((1,), (1,)), ((), ())), preferred_element_type=F32)

    def absorb(g, s, kblk, mask):
        width = kblk.shape[0]
        rows = group(g)
        if mask is not None:
            s = jnp.where(mask, s, NEG_BIG)
        m_prev = m_ref[rows, :]
        m_new = jnp.maximum(m_prev, jnp.max(s, axis=1, keepdims=True))
        alpha = jnp.exp2(m_prev - m_new)
        p = jnp.exp2(s - lane_tile(m_new, width)).astype(BF16)
        acc_ref[rows, :] = lane_tile(alpha, KEY_DIM) * acc_ref[rows, :] + _dot(p, kblk)
        m_ref[rows, :] = m_new

    chunk_shift = CHUNK.bit_length() - 1

    def diag_mask(g, col0, width):
        r0, n_r = ((g * group_rows) % tq, group_rows) if group_rows < tq else (0, tq)
        if (col0 + width - 1) >> chunk_shift <= r0 >> chunk_shift:
            return "all"
        if col0 >> chunk_shift > (r0 + n_r - 1) >> chunk_shift:
            return "none"
        row = (lax.broadcasted_iota(jnp.int32, (group_rows, width), 0) + r0) & (tq - 1)
        col = lax.broadcasted_iota(jnp.int32, (group_rows, width), 1) + col0
        return (col >> chunk_shift) <= (row >> chunk_shift)

    def mask_arg(vis):
        return None if isinstance(vis, str) else vis

    def any_visible(vis):
        return not (isinstance(vis, str) and vis == "none")

    if past_len:
        def past_body(j, carry):
            start = pl.multiple_of(j * tk_past, tk_past)
            lat = ckv_past_ref[0, pl.ds(start, tk_past), :].astype(BF16)
            rot_t = krt_past_ref[0, :, pl.ds(start, tk_past)].astype(BF16)
            for g in range(n_groups):
                rows = group(g)
                q = q_ref[0].reshape(m_rows, KEY_DIM)[rows]
                s = lax.dot_general(q[:, :KV_RANK], lat, (((1,), (1,)), ((), ())), preferred_element_type=F32)
                s = s + _dot(q[:, KV_RANK:KV_RANK + ROPE_DIM], rot_t)
                m_prev = m_ref[rows, :]
                m_new = jnp.maximum(m_prev, jnp.max(s, axis=1, keepdims=True))
                alpha = jnp.exp2(m_prev - m_new)
                p = jnp.exp2(s - lane_tile(m_new, tk_past))
                denom = jnp.broadcast_to(jnp.sum(p, axis=1, keepdims=True), (group_rows, KEY_DIM - KV_RANK))
                acc_ref[rows, :] = lane_tile(alpha, KEY_DIM) * acc_ref[rows, :] + jnp.concatenate(
                    [_dot(p.astype(BF16), lat), denom], axis=-1)
                m_ref[rows, :] = m_new
            return carry
        lax.fori_loop(0, past_len // tk_past, past_body, 0)
        kblk = knew_ref[0]
        for g in range(n_groups):
            absorb(g, scores(g, kblk), kblk, mask_arg(diag_mask(g, 0, tq)))
    else:
        def new_block(j):
            return knew_ref[0, pl.ds(pl.multiple_of(j * tk, tk), tk), :]

        n_diag = tq // tk
        n_full = i * n_diag
        first = new_block(0)
        for g in range(n_groups):
            s_ref[group(g), :] = scores(g, first)

        def body(j, carry):
            kblk, knext = new_block(j), new_block(j + 1)
            for g in range(n_groups):
                s = s_ref[group(g), :]
                s_ref[group(g), :] = scores(g, knext)
                absorb(g, s, kblk, None)
            return carry
        lax.fori_loop(0, n_full, body, 0)

        for d in range(n_diag):
            kblk = new_block(n_full + d)
            knext = new_block(n_full + d + 1) if d + 1 < n_diag else None
            for g in range(n_groups):
                vis = diag_mask(g, d * tk, tk)
                if any_visible(vis):
                    s = s_ref[group(g), :]
                if knext is not None and any_visible(diag_mask(g, (d + 1) * tk, tk)):
                    s_ref[group(g), :] = scores(g, knext)
                if any_visible(vis):
                    absorb(g, s, kblk, mask_arg(vis))

    heads = []
    for hd in range(MLA_HEADS):
        acc = acc_ref[hd * tq:(hd + 1) * tq, :]
        denom = acc[:, KV_RANK + ROPE_DIM:KV_RANK + ROPE_DIM + 1]
        heads.append(_dot((acc[:, :KV_RANK] / denom).astype(BF16), wuv_ref[hd]))
    o_ref[0] = jnp.concatenate(heads, axis=-1).astype(BF16)


def _attn(q, k_new, past, layer, w_uv, *, tq, tk, tk_past, group_rows):
    n_b, _, n_t, _ = q.shape
    past_len = 0 if past is None else past[0].shape[2]
    assert (past_len % CHUNK == 0) and (tq % CHUNK == 0 or n_t == tq <= CHUNK)
    assert past_len == 0 or (n_t == tq and past_len % tk_past == 0)
    m_rows = MLA_HEADS * tq
    assert m_rows % group_rows == 0 and (group_rows % tq == 0 or tq % group_rows == 0)
    assert past_len or (tq % tk == 0 and tk % CHUNK == 0)
    scratch = [pltpu.VMEM((m_rows, LANES), F32), pltpu.VMEM((m_rows, KEY_DIM), F32)]
    if not past_len:
        scratch.append(pltpu.VMEM((m_rows, tk), F32))
    args = [q, k_new]
    specs = [pl.BlockSpec((1, MLA_HEADS, tq, KEY_DIM), lambda b, i: (b, 0, i, 0)),
             pl.BlockSpec((1, n_t, KEY_DIM), lambda b, i: (b, 0, 0), pipeline_mode=pl.Buffered(1))]
    if past_len:
        args += list(past)
        specs += [pl.BlockSpec((None, 1) + c.shape[2:], lambda b, i: (layer, b, 0, 0)) for c in past]
    args.append(w_uv)
    specs.append(_layer_spec(w_uv, layer))
    return pl.pallas_call(
        functools.partial(_attn_kernel, tq=tq, tk=tk, tk_past=tk_past, past_len=past_len,
                          group_rows=group_rows),
        grid=(n_b, n_t // tq),
        in_specs=specs,
        out_specs=pl.BlockSpec((1, tq, MLA_HEADS * V_DIM), lambda b, i: (b, i, 0)),
        out_shape=jax.ShapeDtypeStruct((n_b, n_t, MLA_HEADS * V_DIM), BF16),
        scratch_shapes=scratch,
        compiler_params=pltpu.CompilerParams(
            dimension_semantics=("arbitrary", "arbitrary"), vmem_limit_bytes=VMEM_LIMIT),
        name="attn",
    )(*args)


def _prep_weights(norm_ffn1, norm_mix, w_in, q_norm, w_uq, kv_norm, w_ukv, conv_w, pool_w, pool_scale,
                  w_out, norm_ffn2):
    n_l = w_in.shape[0]
    half = ROPE_DIM // 2
    packed = lambda w: jnp.concatenate([w, w[..., half:], w[..., :half]], axis=-1)
    vec = lambda v: v.reshape(n_l, 1, v.shape[-1])
    wq = w_uq.reshape(n_l, Q_RANK, MLA_HEADS, NOPE_DIM + ROPE_DIM)
    nope = wq[..., :NOPE_DIM].reshape(n_l, Q_RANK, MLA_HEADS * NOPE_DIM)
    rope_cols = [wq[:, :, hd, NOPE_DIM:] for hd in range(MLA_HEADS)]
    w_uq_p = jnp.concatenate([nope] + [packed(r) for r in rope_cols], axis=-1).astype(BF16)

    wkv = w_ukv.reshape(n_l, KV_RANK, MLA_HEADS, NOPE_DIM + V_DIM)
    w_ukt = jnp.transpose(wkv[..., :NOPE_DIM], (0, 2, 3, 1)).astype(BF16)
    w_uv = jnp.transpose(wkv[..., NOPE_DIM:], (0, 2, 1, 3)).astype(BF16)

    eye = jnp.eye(POOL_GROUPS, dtype=F32)
    pool_bd = (pool_w[:, :, :, None, :] * eye[None, :, None, :, None]).reshape(
        n_l, POOL_DIM, POOL_DIM).astype(BF16)

    return {
        "norm_ffn1": vec(norm_ffn1),
        "norm_mix": vec(norm_mix), "w_in_t": jnp.swapaxes(w_in, 1, 2).astype(BF16), "q_norm": vec(q_norm),
        "w_uq": w_uq_p,
        "kv_norm": vec(kv_norm), "w_ukt": w_ukt, "w_uv": w_uv,
        "conv_w": conv_w, "pool_bd": pool_bd, "pool_scale": vec(pool_scale),
        "w_out": w_out.astype(BF16),
        "norm_ffn2": vec(norm_ffn2),
    }


def _trunk_layer(h, mod, layer, lw, ffn1_w, ffn2_w, cast1, cast2, tables, conv_state, pool_state, past,
                 norm_final, *, bt, tt, tm, tq, group_rows, pos0):
    h1, made1 = _ffn(h, mod, layer, lw["norm_ffn1"], *ffn1_w, bt=bt, tt=tt, cast=cast1)
    if ffn2_w is None:
        ffn2_w = made1
    q, k_new, ckv, kr, sp, conv_new, pool_new = _mix_in(
        h1, mod, layer, lw, tables[0], tables[1], conv_state, pool_state, tm=tm, pos0=pos0,
        kr_transposed=past is None)
    a = _attn(q, k_new, past, layer, lw["w_uv"], tq=tq, tk=min(tq, 512), tk_past=past[0].shape[2] if past is not None else 0,
              group_rows=group_rows)
    h3, made2 = _ffn(h1, mod, layer, lw["norm_ffn2"], *ffn2_w, bt=bt, tt=tt,
                     mix=(a, sp, lw["w_out"]), norm_final=norm_final, cast=cast2)
    return (h3, ckv, kr, conv_new, pool_new), ffn2_w, made2


def kernel(x_prompt, x_sample, c_prompt, c_sample, cache_ckv, cache_krope, state_conv, state_pool, w_ada, b_ada, norm_ffn1, w_ffn1_gu, w_ffn1_down, norm_mix, w_in, q_norm, w_uq, kv_norm, w_ukv, conv_w, pool_w, pool_scale, w_out, norm_ffn2, w_ffn2_gu, w_ffn2_down, norm_final):
    bp, seq, _ = x_prompt.shape
    bs, dec_seq, _ = x_sample.shape
    past_len = cache_ckv.shape[2]
    assert bp + bs <= ADA_ROWS

    c_all = jnp.concatenate(
        [c_prompt, c_sample, jnp.zeros((ADA_ROWS - bp - bs, D_MODEL), F32)], axis=0)
    mods = _ada(c_all, w_ada, b_ada).reshape(DEPTH, ADA_ROWS, N_ADA, D_MODEL)

    tm_p = 512
    tables_p = _rope_tables(seq, 0, 2 * tm_p)
    tables_s = _rope_tables(dec_seq, past_len, dec_seq)
    lw = _prep_weights(norm_ffn1, norm_mix, w_in, q_norm, w_uq, kv_norm, w_ukv, conv_w, pool_w,
                       pool_scale, w_out, norm_ffn2)
    past = (cache_ckv, jnp.swapaxes(cache_krope, 2, 3))

    hp, hs = x_prompt, x_sample
    outs_p, outs_s = [], []
    zero_conv = jnp.zeros((bp, CONV_W - 1, CONV_DIM), F32)
    zero_pool = jnp.zeros((bp, POOL_MAX - 1, POOL_DIM), F32)
    ffn1_w = (_cast_layer(w_ffn1_gu, 0), _cast_layer(w_ffn1_down, 0))
    for l in range(DEPTH):
        nf = norm_final if l == DEPTH - 1 else None
        cast2 = ((w_ffn1_gu, l + 1), (w_ffn1_down, l + 1)) if l + 1 < DEPTH else ()
        op, ffn2_w, next_ffn1_w = _trunk_layer(
            hp, mods[l, :bp], l, lw, ffn1_w, None, ((w_ffn2_gu, l), (w_ffn2_down, l)), cast2,
            tables_p, zero_conv, zero_pool, None, nf,
            bt=1, tt=tm_p, tm=tm_p, tq=1024, group_rows=512, pos0=0)
        os_, _, _ = _trunk_layer(
            hs, mods[l, bp:bp + bs], l, lw, ffn1_w, ffn2_w, (), (),
            tables_s, state_conv[l], state_pool[l], past, nf,
            bt=bs, tt=dec_seq, tm=dec_seq, tq=dec_seq, group_rows=MLA_HEADS * dec_seq, pos0=past_len)
        hp, hs, ffn1_w = op[0], os_[0], next_ffn1_w
        outs_p.append(op[1:])
        outs_s.append(os_[1:])

    stack = lambda outs, k: jnp.stack([o[k] for o in outs])
    return (hp, hs,
            stack(outs_p, 0), jnp.swapaxes(stack(outs_p, 1), 2, 3), stack(outs_p, 2), stack(outs_p, 3),
            stack(outs_s, 0), stack(outs_s, 1), stack(outs_s, 2), stack(outs_s, 3))
```
